```python
import math
import jax
import jax.numpy as jnp
from jax import lax
import numpy as np

D_MODEL = 1024
BATCH = 16
SEQ = 256
DEPTH = 4
DEC_BATCH = 4
DEC_SEQ = 2048
PAST_LEN = 256

GRID_W = 64
F32 = jnp.float32

MLA_HEADS = 8
QK_NOPE = 64
QK_ROPE = 32
QK_HEAD = QK_NOPE + QK_ROPE
V_HEAD = 64
Q_RANK = 256
KV_RANK = 128
ROPE_THETA = 10000.0
Q_BLOCK = 128

SSD_HEADS = 4
SSD_HEAD_DIM = 64
SSD_INNER = SSD_HEADS * SSD_HEAD_DIM
SSD_GROUPS = 2
SSD_STATE = 64
SSD_CONV = 5
SSD_CHUNK = 128

CM_CH = 256
CM_WIDTH = 31

D_FF = -(-8 * D_MODEL // (3 * 256)) * 256
N_MOD = 6

MLA_IN = Q_RANK + KV_RANK + QK_ROPE
SSD_XBC = SSD_INNER + 2 * SSD_GROUPS * SSD_STATE
SSD_IN = SSD_INNER + SSD_XBC + 2 * SSD_HEADS
CM_IN = 2 * CM_CH
OFF_SSD = MLA_IN
OFF_CM = MLA_IN + SSD_IN
IN_WIDTH = OFF_CM + CM_IN
MIX_WIDTH = MLA_HEADS * V_HEAD + SSD_INNER + CM_CH

kernel_name = 'hybrid_mla_ssd_conformer_dit_step'


def rmsnorm(x, g, eps=1e-6):
    xf = x.astype(F32)
    y = xf * lax.rsqrt(jnp.mean(xf * xf, axis=-1, keepdims=True) + eps)
    return (y * g.astype(F32)).astype(x.dtype)


def layernorm(x, g, b, eps=1e-5):
    xf = x.astype(F32)
    mu = jnp.mean(xf, axis=-1, keepdims=True)
    var = jnp.mean(jnp.square(xf - mu), axis=-1, keepdims=True)
    y = (xf - mu) * lax.rsqrt(var + eps)
    return (y * g.astype(F32) + b.astype(F32)).astype(x.dtype)


def dwconv(x, w, b):
    k = w.shape[0]
    y = lax.conv_general_dilated(x, w[:, None, :].astype(x.dtype), (1,), [(k // 2, k // 2)],
                                 dimension_numbers=('NWC', 'WIO', 'NWC'),
                                 feature_group_count=x.shape[-1])
    return y + b


def rope_tables(t):
    rows = t // GRID_W
    row = jnp.repeat(jnp.arange(rows), GRID_W).astype(F32)
    col = (jnp.arange(rows * GRID_W) % GRID_W).astype(F32)
    nf = QK_ROPE // 4
    inv = ROPE_THETA ** (-jnp.arange(nf, dtype=F32) / nf)
    ang = jnp.stack([row[:, None] * inv, col[:, None] * inv], axis=1)
    return jnp.cos(ang), jnp.sin(ang)


def apply_rope(x, cos, sin):
    xf = x.astype(F32).reshape(x.shape[:-1] + (2, 2, QK_ROPE // 4))
    x1, x2 = xf[..., 0, :], xf[..., 1, :]
    c, s = cos[:, None], sin[:, None]
    out = jnp.stack([x1 * c - x2 * s, x2 * c + x1 * s], axis=-2)
    return out.reshape(x.shape).astype(x.dtype)


def block_attention(q, k, v):
    b, tq, h, dk = q.shape
    nb = tq // Q_BLOCK
    scale = dk ** -0.5
    qb = q.reshape(b, nb, Q_BLOCK, h, dk).transpose(1, 0, 2, 3, 4)

    def one_block(qi):
        s = jnp.einsum('bqhd,bkhd->bhqk', qi, k).astype(F32) * scale
        p = jax.nn.softmax(s, axis=-1).astype(v.dtype)
        return jnp.einsum('bhqk,bkhd->bqhd', p, v)

    o = lax.map(one_block, qb)
    return o.transpose(1, 0, 2, 3, 4).reshape(b, tq, h, v.shape[-1])


def mla_kv(ckv, kr_h, w_ukv):
    b, t, _ = ckv.shape
    kv = (ckv @ w_ukv).reshape(b, t, MLA_HEADS, QK_NOPE + V_HEAD)
    k = jnp.concatenate([kv[..., :QK_NOPE],
                         jnp.broadcast_to(kr_h, (b, t, MLA_HEADS, QK_ROPE))], axis=-1)
    return k, kv[..., QK_NOPE:]


def mla_mixer(comb, lp, rope, ctx):
    b, t, _ = comb.shape
    q = rmsnorm(comb[..., :Q_RANK], lp['g_q']) @ lp['w_uq']
    q = q.reshape(b, t, MLA_HEADS, QK_HEAD)
    ckv = rmsnorm(comb[..., Q_RANK:Q_RANK + KV_RANK], lp['g_kv'])
    kr = comb[..., Q_RANK + KV_RANK:MLA_IN]
    q_nope, q_rope = q[..., :QK_NOPE], q[..., QK_NOPE:]
    kr_h = kr[:, :, None, :]
    if rope is not None:
        q_rope = apply_rope(q_rope, *rope)
        kr_h = apply_rope(kr_h, *rope)
    q = jnp.concatenate([q_nope, q_rope], axis=-1)
    k, v = mla_kv(ckv, kr_h, lp['w_ukv'])
    if ctx is not None:
        k_c, v_c = mla_kv(ctx[0], ctx[1][:, :, None, :], lp['w_ukv'])
        k = jnp.concatenate([k_c, k], axis=1)
        v = jnp.concatenate([v_c, v], axis=1)
    o = block_attention(q, k, v).reshape(b, t, MLA_HEADS * V_HEAD)
    return o, ckv, kr


def ssd_scan(x, dt, bm, cm, a, h0):
    bsz, t, h, p = x.shape
    n = bm.shape[-1]
    nc = t // SSD_CHUNK
    L = SSD_CHUNK
    xf = (x.astype(F32) * dt[..., None]).reshape(bsz, nc, L, h, p)
    bf = bm.astype(F32).reshape(bsz, nc, L, h, n)
    cf = cm.astype(F32).reshape(bsz, nc, L, h, n)
    acs = jnp.cumsum((dt * a).reshape(bsz, nc, L, h).transpose(0, 3, 1, 2), axis=-1)
    lower = jnp.tril(jnp.ones((L, L), bool))
    seg = jnp.exp(jnp.where(lower, acs[..., :, None] - acs[..., None, :], -jnp.inf))
    y_diag = jnp.einsum('bclhn,bcshn,bhcls,bcshp->bclhp', cf, bf, seg, xf)
    decay_to_end = jnp.exp(acs[..., -1:] - acs)
    chunk_states = jnp.einsum('bclhn,bhcl,bclhp->bchpn', bf, decay_to_end, xf)
    chunk_decay = jnp.exp(acs[..., -1])

    def step(hc, inp):
        s_c, d_c = inp
        return d_c[:, :, None, None] * hc + s_c, hc

    h_fin, h_in = lax.scan(step, h0.astype(F32),
                           (chunk_states.transpose(1, 0, 2, 3, 4), chunk_decay.transpose(2, 0, 1)))
    y_off = jnp.einsum('bclhn,cbhpn,bhcl->bclhp', cf, h_in, jnp.exp(acs))
    y = (y_diag + y_off).reshape(bsz, t, h, p)
    return y.astype(x.dtype), h_fin.astype(h0.dtype)


def ssd_mixer(comb, lp, h0):
    b, t, _ = comb.shape
    z = comb[..., :SSD_INNER]
    xbc = jax.nn.silu(dwconv(comb[..., SSD_INNER:SSD_INNER + SSD_XBC], lp['ssd_conv_w'], lp['ssd_conv_b']))
    xs = xbc[..., :SSD_INNER].reshape(b, t, SSD_HEADS, SSD_HEAD_DIM)
    gn = SSD_GROUPS * SSD_STATE
    rep = SSD_HEADS // SSD_GROUPS
    bm = jnp.repeat(xbc[..., SSD_INNER:SSD_INNER + gn].reshape(b, t, SSD_GROUPS, SSD_STATE), rep, axis=2)
    cm = jnp.repeat(xbc[..., SSD_INNER + gn:].reshape(b, t, SSD_GROUPS, SSD_STATE), rep, axis=2)
    dt_raw = comb[..., SSD_INNER + SSD_XBC:].reshape(b, t, 2, SSD_HEADS)
    y = jnp.zeros_like(xs)
    finals = []
    for d in range(2):
        dt = jax.nn.softplus(dt_raw[:, :, d].astype(F32) + lp['ssd_dt_bias'][d].astype(F32))
        a = -jnp.exp(lp['ssd_a_log'][d].astype(F32))
        args = (xs, dt, bm, cm)
        if d == 1:
            args = tuple(jnp.flip(u, axis=1) for u in args)
        yd, hd = ssd_scan(*args, a, h0[:, d])
        if d == 1:
            yd = jnp.flip(yd, axis=1)
        y = y + yd + lp['ssd_d'][d][:, None] * xs
        finals.append(hd)
    y = y.reshape(b, t, SSD_INNER)
    y = rmsnorm(y * jax.nn.silu(z), lp['ssd_norm_g'])
    return y, jnp.stack(finals, axis=1)


def conv_module(comb, lp):
    g = comb[..., :CM_CH] * jax.nn.sigmoid(comb[..., CM_CH:])
    g = dwconv(g, lp['cm_conv_w'], lp['cm_conv_b'])
    return jax.nn.silu(layernorm(g, lp['cm_ln_g'], lp['cm_ln_b']))


def layer(x, cond, lp, rope=None, ctx=None):
    mod = (jax.nn.silu(cond) @ lp['w_ada'] + lp['b_ada'])[:, None, :]
    sh1, sc1, g1, sh2, sc2, g2 = jnp.split(mod, N_MOD, axis=-1)
    h = rmsnorm(x, lp['g_mix']) * (1 + sc1) + sh1
    comb = h @ lp['w_in']
    attn, ckv, kr = mla_mixer(comb[..., :MLA_IN], lp, rope, None if ctx is None else ctx[:2])
    if ctx is None:
        h0 = jnp.zeros((x.shape[0], 2, SSD_HEADS, SSD_HEAD_DIM, SSD_STATE), x.dtype)
    else:
        h0 = ctx[2]
    ssm, h_fin = ssd_mixer(comb[..., OFF_SSD:OFF_CM], lp, h0)
    conv = conv_module(comb[..., OFF_CM:], lp)
    mixed = jnp.concatenate([attn, ssm, conv], axis=-1)
    x = x + g1 * (mixed @ lp['w_out'])
    h = rmsnorm(x, lp['g_ffn']) * (1 + sc2) + sh2
    ff = (jax.nn.silu(h @ lp['w_gate']) * (h @ lp['w_up'])) @ lp['w_down']
    x = x + g2 * ff
    return x, ckv, kr, h_fin


def setup_inputs(seed: int = 0) -> dict:
    key = jax.random.key(seed)
    ks = jax.random.split(key, 32)
    L = DEPTH

    def nrm(k, shape, s):
        return jax.random.normal(k, shape, F32) * s

    def gain(k, shape):
        return 1.0 + 0.02 * jax.random.normal(k, shape, F32)

    dt0 = jnp.exp(jax.random.uniform(ks[17], (L, 2, SSD_HEADS), F32,
                                     minval=math.log(1e-3), maxval=math.log(1e-1)))
    return dict(
        x_prompt=nrm(ks[0], (BATCH, SEQ, D_MODEL), 1.0),
        x_sample=nrm(ks[1], (DEC_BATCH, DEC_SEQ, D_MODEL), 1.0),
        c=nrm(ks[2], (DEC_BATCH, D_MODEL), 1.0),
        cache_ckv=nrm(ks[3], (DEC_BATCH, L, PAST_LEN, KV_RANK), 1.0),
        cache_krope=nrm(ks[4], (DEC_BATCH, L, PAST_LEN, QK_ROPE), 1.0),
        state_ssd=nrm(ks[5], (DEC_BATCH, L, 2, SSD_HEADS, SSD_HEAD_DIM, SSD_STATE), 0.5),
        c_ctx=nrm(ks[6], (D_MODEL,), 1.0),
        w_ada=nrm(ks[7], (L, D_MODEL, N_MOD * D_MODEL), 0.5 * D_MODEL ** -0.5),
        b_ada=nrm(ks[8], (L, N_MOD * D_MODEL), 0.02),
        g_mix=gain(ks[9], (L, D_MODEL)),
        w_in=nrm(ks[10], (L, D_MODEL, IN_WIDTH), D_MODEL ** -0.5),
        g_q=gain(ks[11], (L, Q_RANK)),
        w_uq=nrm(ks[12], (L, Q_RANK, MLA_HEADS * QK_HEAD), Q_RANK ** -0.5),
        g_kv=gain(ks[13], (L, KV_RANK)),
        w_ukv=nrm(ks[14], (L, KV_RANK, MLA_HEADS * (QK_NOPE + V_HEAD)), KV_RANK ** -0.5),
        ssd_conv_w=nrm(ks[15], (L, SSD_CONV, SSD_XBC), SSD_CONV ** -0.5),
        ssd_conv_b=nrm(ks[16], (L, SSD_XBC), 0.02),
        ssd_dt_bias=dt0 + jnp.log(-jnp.expm1(-dt0)),
        ssd_a_log=jnp.log(jax.random.uniform(ks[18], (L, 2, SSD_HEADS), F32, minval=1.0, maxval=16.0)),
        ssd_d=gain(ks[19], (L, 2, SSD_HEADS)),
        ssd_norm_g=gain(ks[20], (L, SSD_INNER)),
        cm_conv_w=nrm(ks[21], (L, CM_WIDTH, CM_CH), CM_WIDTH ** -0.5),
        cm_conv_b=nrm(ks[22], (L, CM_CH), 0.02),
        cm_ln_g=gain(ks[23], (L, CM_CH)),
        cm_ln_b=nrm(ks[24], (L, CM_CH), 0.02),
        w_out=nrm(ks[25], (L, MIX_WIDTH, D_MODEL), MIX_WIDTH ** -0.5),
        g_ffn=gain(ks[26], (L, D_MODEL)),
        w_gate=nrm(ks[27], (L, D_MODEL, D_FF), D_MODEL ** -0.5),
        w_up=nrm(ks[28], (L, D_MODEL, D_FF), D_MODEL ** -0.5),
        w_down=nrm(ks[29], (L, D_FF, D_MODEL), D_FF ** -0.5),
        g_final=gain(ks[30], (D_MODEL,)),
    )


def reference(x_prompt, x_sample, c, cache_ckv, cache_krope, state_ssd, c_ctx, w_ada, b_ada,
              g_mix, w_in, g_q, w_uq, g_kv, w_ukv, ssd_conv_w, ssd_conv_b, ssd_dt_bias,
              ssd_a_log, ssd_d, ssd_norm_g, cm_conv_w, cm_conv_b, cm_ln_g, cm_ln_b, w_out,
              g_ffn, w_gate, w_up, w_down, g_final):
    rope = rope_tables(x_sample.shape[1])
    cond_ctx = c_ctx[None, :]
    xp, xs = x_prompt, x_sample
    ckvs, krs, hss = [], [], []
    for l in range(DEPTH):
        lp = dict(w_ada=w_ada[l], b_ada=b_ada[l], g_mix=g_mix[l], w_in=w_in[l], g_q=g_q[l],
                  w_uq=w_uq[l], g_kv=g_kv[l], w_ukv=w_ukv[l], ssd_conv_w=ssd_conv_w[l],
                  ssd_conv_b=ssd_conv_b[l], ssd_dt_bias=ssd_dt_bias[l], ssd_a_log=ssd_a_log[l],
                  ssd_d=ssd_d[l], ssd_norm_g=ssd_norm_g[l], cm_conv_w=cm_conv_w[l],
                  cm_conv_b=cm_conv_b[l], cm_ln_g=cm_ln_g[l], cm_ln_b=cm_ln_b[l], w_out=w_out[l],
                  g_ffn=g_ffn[l], w_gate=w_gate[l], w_up=w_up[l], w_down=w_down[l])
        xp, ckv, kr, h_fin = layer(xp, cond_ctx, lp)
        ckvs.append(ckv)
        krs.append(kr)
        hss.append(h_fin)
        xs = layer(xs, c, lp, rope, (cache_ckv[:, l], cache_krope[:, l], state_ssd[:, l]))[0]
    y_prompt = rmsnorm(xp, g_final)
    y_sample = rmsnorm(xs, g_final)
    new_ckv = jnp.stack(ckvs, axis=1)
    new_krope = jnp.stack(krs, axis=1)
    new_ssd = jnp.stack(hss, axis=1)
    return (y_prompt, y_sample, new_ckv, new_krope, new_ssd)
```

```python
import functools

import jax
import jax.numpy as jnp
from jax import lax
from jax.experimental import pallas as pl
from jax.experimental.pallas import tpu as pltpu

F32 = jnp.float32
BF16 = jnp.bfloat16

D_MODEL = 1024
BATCH = 16
SEQ = 256
DEPTH = 4
DEC_BATCH = 4
DEC_SEQ = 2048
PAST_LEN = 256
GRID_W = 64

MLA_HEADS = 8
QK_NOPE = 64
QK_ROPE = 32
QK_HEAD = QK_NOPE + QK_ROPE
V_HEAD = 64
Q_RANK = 256
KV_RANK = 128
ROPE_THETA = 10000.0

SSD_HEADS = 4
SSD_HEAD_DIM = 64
SSD_INNER = SSD_HEADS * SSD_HEAD_DIM
SSD_GROUPS = 2
SSD_STATE = 64
SSD_CONV = 5
SSD_CHUNK = 128
SSD_XBC = SSD_INNER + 2 * SSD_GROUPS * SSD_STATE

CM_CH = 256
CM_WIDTH = 31

D_FF = 2816
N_MOD = 6

N_CTX = BATCH * SEQ
N_LAT = DEC_BATCH * DEC_SEQ
N_TOK = N_CTX + N_LAT

LANE = 128
HEAD_PAD = 128
VMEM_LIMIT = 48 * 1024 * 1024

TM_IN = 256
TM_FFN = 512
TQ_LAT = 512
FF_CHUNK = 256
CONV_ROWS = 128
CM_ROWS = 32

C_QLAT = 0
C_CKV = C_QLAT + Q_RANK
C_KR = C_CKV + KV_RANK
C_Z = C_KR + 2 * HEAD_PAD
C_XBC = C_Z + SSD_INNER
C_DT = C_XBC + SSD_XBC
C_CM = C_DT + LANE
IN_ARR = C_CM + 2 * CM_CH


def _params(sem):
    return pltpu.CompilerParams(dimension_semantics=sem, vmem_limit_bytes=VMEM_LIMIT)


def _dot(a, b):
    return jnp.dot(a, b, preferred_element_type=F32)


def _dot_nt(a, b):
    return lax.dot_general(a, b, (((1,), (1,)), ((), ())), preferred_element_type=F32)


def _rms(x, eps=1e-6):
    return x * lax.rsqrt(jnp.mean(x * x, axis=-1, keepdims=True) + eps)


def _silu(x):
    return x * jax.nn.sigmoid(x)


def _ada_kernel(cond_ref, w_ref, b_ref, o_ref):
    c = cond_ref[...]
    a = _silu(c).astype(BF16)
    o_ref[0] = _dot(a, w_ref[0].astype(BF16)) + b_ref[0]


def _ada_call(cond8, w_ada, b_ada):
    tn = 1536
    nt = (N_MOD * D_MODEL) // tn
    return pl.pallas_call(
        _ada_kernel,
        out_shape=jax.ShapeDtypeStruct((DEPTH, 8, N_MOD * D_MODEL), F32),
        grid=(DEPTH, nt),
        in_specs=[
            pl.BlockSpec((8, D_MODEL), lambda l, j: (0, 0)),
            pl.BlockSpec((1, D_MODEL, tn), lambda l, j: (l, 0, j)),
            pl.BlockSpec((1, 1, tn), lambda l, j: (l, 0, j)),
        ],
        out_specs=pl.BlockSpec((1, 8, tn), lambda l, j: (l, 0, j)),
        compiler_params=_params(("parallel", "parallel")),
        name="ada_mod",
    )(cond8, w_ada, b_ada.reshape(DEPTH, 1, N_MOD * D_MODEL))


def _mod_row(i, tm):
    return jnp.where(i < N_CTX // tm, 0, 1 + (i * tm - N_CTX) // DEC_SEQ)


def _inproj_kernel(x_ref, mod_ref, gmix_ref, win_ref, gq_ref, wuq_ref, gkv_ref, wukv_ref,
                   cos_ref, sin_ref,
                   q_out, k_out, v_out, ckv_out, kr_out, z_out, xbc_out, dt_out, cm_out):
    x = x_ref[...]
    sh1 = mod_ref[0, :, 0:D_MODEL]
    sc1 = mod_ref[0, :, D_MODEL:2 * D_MODEL]
    h = (_rms(x) * gmix_ref[...] * (1.0 + sc1) + sh1).astype(BF16)
    cos = cos_ref[...]
    sin = sin_ref[...]

    q_lat = _dot(h, win_ref[:, C_QLAT:C_QLAT + Q_RANK])
    qn = (_rms(q_lat) * gq_ref[...]).astype(BF16)
    qq = _dot(qn, wuq_ref[...])
    scale = QK_HEAD ** -0.5
    half = MLA_HEADS * HEAD_PAD
    for hd in range(MLA_HEADS):
        lo = hd * HEAD_PAD
        qh = qq[:, lo:lo + HEAD_PAD] * cos + qq[:, half + lo:half + lo + HEAD_PAD] * sin
        q_out[:, lo:lo + HEAD_PAD] = (qh * scale).astype(BF16)

    ckv = _rms(_dot(h, win_ref[:, C_CKV:C_CKV + KV_RANK])) * gkv_ref[...]
    ckv_out[...] = ckv
    kv = _dot(ckv.astype(BF16), wukv_ref[...])
    kr2 = _dot(h, win_ref[:, C_KR:C_KR + 2 * HEAD_PAD])
    kr = kr2[:, 0:HEAD_PAD]
    kr_out[...] = kr
    kr_rot = kr * cos + kr2[:, HEAD_PAD:2 * HEAD_PAD] * sin
    for hd in range(MLA_HEADS):
        lo = hd * HEAD_PAD
        k_out[:, lo:lo + HEAD_PAD] = (kv[:, lo:lo + HEAD_PAD] + kr_rot).astype(BF16)
    v_out[...] = kv[:, half:half + MLA_HEADS * V_HEAD].astype(BF16)

    z_out[...] = _dot(h, win_ref[:, C_Z:C_Z + SSD_INNER])
    xbc_out[...] = _dot(h, win_ref[:, C_XBC:C_XBC + SSD_XBC])
    dt_out[...] = _dot(h, win_ref[:, C_DT:C_DT + LANE])
    cm_out[...] = _dot(h, win_ref[:, C_CM:C_CM + 2 * CM_CH])


def _inproj_call(x, mod_l, gmix, win, gq, wuq, gkv, wukv, cos_t, sin_t):
    tm = TM_IN
    nt = N_TOK // tm
    lat_tiles = DEC_SEQ // tm

    def tok(i):
        return (i, 0)

    def const(i):
        return (0, 0)

    def rope_idx(i):
        return (jnp.where(i < N_CTX // tm, 0, 1 + (i - N_CTX // tm) % lat_tiles), 0)

    outs = [
        ((N_TOK, MLA_HEADS * HEAD_PAD), BF16),
        ((N_TOK, MLA_HEADS * HEAD_PAD), BF16),
        ((N_TOK, MLA_HEADS * V_HEAD), BF16),
        ((N_TOK, KV_RANK), F32),
        ((N_TOK, HEAD_PAD), F32),
        ((N_TOK, SSD_INNER), F32),
        ((N_TOK, SSD_XBC), F32),
        ((N_TOK, LANE), F32),
        ((N_TOK, 2 * CM_CH), F32),
    ]
    return pl.pallas_call(
        _inproj_kernel,
        out_shape=[jax.ShapeDtypeStruct(s, d) for s, d in outs],
        grid=(nt,),
        in_specs=[
            pl.BlockSpec((tm, D_MODEL), tok),
            pl.BlockSpec((1, 1, N_MOD * D_MODEL), lambda i: (_mod_row(i, tm), 0, 0)),
            pl.BlockSpec((1, D_MODEL), const),
            pl.BlockSpec((D_MODEL, IN_ARR), const),
            pl.BlockSpec((1, Q_RANK), const),
            pl.BlockSpec((Q_RANK, 2 * MLA_HEADS * HEAD_PAD), const),
            pl.BlockSpec((1, KV_RANK), const),
            pl.BlockSpec((KV_RANK, MLA_HEADS * (HEAD_PAD + V_HEAD)), const),
            pl.BlockSpec((tm, HEAD_PAD), rope_idx),
            pl.BlockSpec((tm, HEAD_PAD), rope_idx),
        ],
        out_specs=[pl.BlockSpec((tm, s[1]), tok) for s, _ in outs],
        compiler_params=_params(("parallel",)),
        name="in_proj",
    )(x, mod_l, gmix, win, gq, wuq, gkv, wukv, cos_t, sin_t)


def _ctxkv_kernel(ckv_ref, kr_ref, wukv_ref, place_ref, k_out, v_out):
    kv = _dot(ckv_ref[0, 0].astype(BF16), wukv_ref[0])
    krp = _dot(kr_ref[0, 0].astype(BF16), place_ref[...])
    half = MLA_HEADS * HEAD_PAD
    k_out[0, 0] = (kv[:, 0:half] + krp).astype(BF16)
    v_out[0, 0] = kv[:, half:half + MLA_HEADS * V_HEAD].astype(BF16)


def _ctxkv_call(cache_ckv, cache_krope, wukv_all, place):
    return pl.pallas_call(
        _ctxkv_kernel,
        out_shape=[
            jax.ShapeDtypeStruct((DEPTH, DEC_BATCH, PAST_LEN, MLA_HEADS * HEAD_PAD), BF16),
            jax.ShapeDtypeStruct((DEPTH, DEC_BATCH, PAST_LEN, MLA_HEADS * V_HEAD), BF16),
        ],
        grid=(DEPTH, DEC_BATCH),
        in_specs=[
            pl.BlockSpec((1, 1, PAST_LEN, KV_RANK), lambda l, b: (b, l, 0, 0)),
            pl.BlockSpec((1, 1, PAST_LEN, QK_ROPE), lambda l, b: (b, l, 0, 0)),
            pl.BlockSpec((1, KV_RANK, MLA_HEADS * (HEAD_PAD + V_HEAD)), lambda l, b: (l, 0, 0)),
            pl.BlockSpec((QK_ROPE, MLA_HEADS * HEAD_PAD), lambda l, b: (0, 0)),
        ],
        out_specs=[
            pl.BlockSpec((1, 1, PAST_LEN, MLA_HEADS * HEAD_PAD), lambda l, b: (l, b, 0, 0)),
            pl.BlockSpec((1, 1, PAST_LEN, MLA_HEADS * V_HEAD), lambda l, b: (l, b, 0, 0)),
        ],
        compiler_params=_params(("parallel", "parallel")),
        name="ctx_kv",
    )(cache_ckv, cache_krope, wukv_all, place)


def _attn_kernel(*refs, has_ctx):
    if has_ctx:
        q_ref, k_ref, v_ref, kc_ref, vc_ref, o_ref = refs
    else:
        q_ref, k_ref, v_ref, o_ref = refs
    outs = []
    for hh in range(2):
        lo = hh * HEAD_PAD
        qh = q_ref[:, lo:lo + HEAD_PAD]
        s = _dot_nt(qh, k_ref[:, lo:lo + HEAD_PAD])
        m = jnp.max(s, axis=-1, keepdims=True)
        if has_ctx:
            sc = _dot_nt(qh, kc_ref[0, :, lo:lo + HEAD_PAD])
            m = jnp.maximum(m, jnp.max(sc, axis=-1, keepdims=True))
        p = jnp.exp(s - m)
        den = jnp.sum(p, axis=-1, keepdims=True)
        o = _dot(p.astype(BF16), v_ref[...])
        if has_ctx:
            pc = jnp.exp(sc - m)
            den = den + jnp.sum(pc, axis=-1, keepdims=True)
            o = o + _dot(pc.astype(BF16), vc_ref[0])
        outs.append(o / den)
    lane = lax.broadcasted_iota(jnp.int32, outs[0].shape, 1)
    o_ref[...] = jnp.where(lane < V_HEAD, outs[0], outs[1]).astype(BF16)


def _attn_call(q, k, v, kc, vc, *, nb, t, tq, row0):
    has_ctx = kc is not None
    nq = t // tq
    npair = MLA_HEADS // 2
    qb0 = row0 // tq
    kb0 = row0 // t
    in_specs = [
        pl.BlockSpec((tq, 2 * HEAD_PAD), lambda b, hp, qi: (qb0 + b * nq + qi, hp)),
        pl.BlockSpec((t, 2 * HEAD_PAD), lambda b, hp, qi: (kb0 + b, hp)),
        pl.BlockSpec((t, 2 * V_HEAD), lambda b, hp, qi: (kb0 + b, hp)),
    ]
    args = [q, k, v]
    if has_ctx:
        in_specs += [
            pl.BlockSpec((1, PAST_LEN, 2 * HEAD_PAD), lambda b, hp, qi: (b, 0, hp)),
            pl.BlockSpec((1, PAST_LEN, 2 * V_HEAD), lambda b, hp, qi: (b, 0, hp)),
        ]
        args += [kc, vc]
    return pl.pallas_call(
        functools.partial(_attn_kernel, has_ctx=has_ctx),
        out_shape=jax.ShapeDtypeStruct((nb * t, MLA_HEADS * V_HEAD), BF16),
        grid=(nb, npair, nq),
        in_specs=in_specs,
        out_specs=pl.BlockSpec((tq, 2 * V_HEAD), lambda b, hp, qi: (b * nq + qi, hp)),
        compiler_params=_params(("parallel", "parallel", "arbitrary")),
        name="attn_lat" if has_ctx else "attn_ctx",
    )(*args)


def _split3(x):
    hi = x.astype(BF16)
    r = x - hi.astype(F32)
    mid = r.astype(BF16)
    lo = (r - mid.astype(F32)).astype(BF16)
    return hi, mid, lo


def _ssd_kernel(*refs, t, has_h0):
    if has_h0:
        (z_ref, xbc_ref, dt_ref, cw_ref, cb_ref, dtb_ref, alog_ref, d0_ref, d1_ref, gn_ref, h0_ref,
         y_out, hf_out, xpad, xc, yf, yb, st) = refs
    else:
        (z_ref, xbc_ref, dt_ref, cw_ref, cb_ref, dtb_ref, alog_ref, d0_ref, d1_ref, gn_ref,
         y_out, hf_out, xpad, xc, yf, yb, st) = refs
    L = SSD_CHUNK
    nc = t // L
    halo = 8

    xpad[0:halo, :] = jnp.zeros((halo, SSD_XBC), F32)
    xpad[halo + t:halo + t + halo, :] = jnp.zeros((halo, SSD_XBC), F32)
    xpad[halo:halo + t, :] = xbc_ref[...]
    cw = cw_ref[...]
    cb = cb_ref[...]
    r = CONV_ROWS
    for blk in range(t // r):
        base = blk * r
        acc = cb + cw[0:1, :] * xpad[base + halo - 2:base + halo - 2 + r, :]
        for kk in range(1, SSD_CONV):
            o = base + halo - 2 + kk
            acc = acc + cw[kk:kk + 1, :] * xpad[o:o + r, :]
        xc[base:base + r, :] = _silu(acc)

    if has_h0:
        st[...] = h0_ref[0]
    else:
        st[...] = jnp.zeros(st.shape, F32)

    a_all = -jnp.exp(alog_ref[...])
    dtb = dtb_ref[...]
    row_i = lax.broadcasted_iota(jnp.int32, (L, L), 0)
    col_i = lax.broadcasted_iota(jnp.int32, (L, L), 1)
    lane_lo = col_i < SSD_HEAD_DIM
    tri = [(row_i >= col_i), (row_i <= col_i)]
    tri_bf = [jnp.where(m, 1.0, 0.0).astype(BF16) for m in tri]
    grp_rows = [row_i < SSD_STATE, row_i >= SSD_STATE]

    def chunk(c, d, y_acc):
        r0 = pl.multiple_of(c * L, L)
        rows = pl.ds(r0, L)
        xs = xc[rows, 0:SSD_INNER]
        bm = xc[rows, SSD_INNER:SSD_INNER + LANE]
        cm = xc[rows, SSD_INNER + LANE:SSD_INNER + 2 * LANE].astype(BF16)
        dtp = jax.nn.softplus(dt_ref[rows, :] + dtb)
        adt = dtp * a_all
        hi, mid, lo = _split3(adt)
        acs = _dot(tri_bf[d], hi) + _dot(tri_bf[d], mid) + _dot(tri_bf[d], lo)
        acs_t = acs.T
        bt = bm.T
        for g in range(SSD_GROUPS):
            btm = jnp.where(grp_rows[g], bt, 0.0).astype(BF16)
            gmat = _dot(cm, btm)
            j0 = d * SSD_HEADS + 2 * g
            j1 = j0 + 1
            ys = []
            for j in (j0, j1):
                diff = acs[:, j:j + 1] - acs_t[j:j + 1, :]
                seg = jnp.exp(jnp.where(tri[d], diff, -jnp.inf))
                ys.append((gmat * seg).astype(BF16))
            dt_pair = jnp.where(lane_lo, dtp[:, j0:j0 + 1], dtp[:, j1:j1 + 1])
            acs_pair = jnp.where(lane_lo, acs[:, j0:j0 + 1], acs[:, j1:j1 + 1])
            xdt = xs[:, g * LANE:(g + 1) * LANE] * dt_pair
            xdt_bf = xdt.astype(BF16)
            y_pair = jnp.where(lane_lo, _dot(ys[0], xdt_bf), _dot(ys[1], xdt_bf))
            h_in = st[d, g]
            y_pair = y_pair + _dot(cm, h_in.astype(BF16)) * jnp.exp(acs_pair)
            tot = acs_pair[L - 1:L, :] if d == 0 else acs_pair[0:1, :]
            s_new = _dot(btm, (xdt * jnp.exp(tot - acs_pair)).astype(BF16))
            st[d, g] = jnp.exp(tot) * h_in + s_new
            y_acc[rows, g * LANE:(g + 1) * LANE] = y_pair

    def body(i, carry):
        chunk(i, 0, yf)
        chunk(nc - 1 - i, 1, yb)
        return carry

    lax.fori_loop(0, nc, body, 0)
    for d in range(2):
        for g in range(SSD_GROUPS):
            hf_out[0, d, g] = st[d, g, g * SSD_STATE:(g + 1) * SSD_STATE, :]

    d0 = d0_ref[...]
    d1 = d1_ref[...]
    gn = gn_ref[...]
    for blk in range(t // r):
        rows = slice(blk * r, (blk + 1) * r)
        xs = xc[rows, 0:SSD_INNER]
        y = yf[rows, :] + yb[rows, :] + d0 * xs + d1 * xs
        y = y * _silu(z_ref[rows, :])
        y_out[rows, :] = (_rms(y) * gn).astype(BF16)


def _ssd_call(z, xbc, dt, cw, cb, dtb, alog, d0, d1, gn, h0, *, nb, t, row0):
    has_h0 = h0 is not None
    b0 = row0 // t

    def tok(b):
        return (b0 + b, 0)

    def const(b):
        return (0, 0)

    in_specs = [
        pl.BlockSpec((t, SSD_INNER), tok),
        pl.BlockSpec((t, SSD_XBC), tok),
        pl.BlockSpec((t, LANE), tok),
        pl.BlockSpec((8, SSD_XBC), const),
        pl.BlockSpec((1, SSD_XBC), const),
        pl.BlockSpec((1, LANE), const),
        pl.BlockSpec((1, LANE), const),
        pl.BlockSpec((1, SSD_INNER), const),
        pl.BlockSpec((1, SSD_INNER), const),
        pl.BlockSpec((1, SSD_INNER), const),
    ]
    args = [z, xbc, dt, cw, cb, dtb, alog, d0, d1, gn]
    st_block = (1, 2, SSD_GROUPS, LANE, LANE)
    if has_h0:
        in_specs.append(pl.BlockSpec(st_block, lambda b: (b, 0, 0, 0, 0)))
        args.append(h0)
    return pl.pallas_call(
        functools.partial(_ssd_kernel, t=t, has_h0=has_h0),
        out_shape=[
            jax.ShapeDtypeStruct((nb * t, SSD_INNER), BF16),
            jax.ShapeDtypeStruct((nb, 2, SSD_GROUPS, SSD_STATE, LANE), F32),
        ],
        grid=(nb,),
        in_specs=in_specs,
        out_specs=[
            pl.BlockSpec((t, SSD_INNER), lambda b: (b, 0)),
            pl.BlockSpec((1, 2, SSD_GROUPS, SSD_STATE, LANE), lambda b: (b, 0, 0, 0, 0)),
        ],
        scratch_shapes=[
            pltpu.VMEM((t + 16, SSD_XBC), F32),
            pltpu.VMEM((t, SSD_XBC), F32),
            pltpu.VMEM((t, SSD_INNER), F32),
            pltpu.VMEM((t, SSD_INNER), F32),
            pltpu.VMEM((2, SSD_GROUPS, LANE, LANE), F32),
        ],
        compiler_params=_params(("parallel",)),
        name="ssd_lat" if has_h0 else "ssd_ctx",
    )(*args)


def _cm_kernel(cm_ref, w_ref, b_ref, lg_ref, lb_ref, o_ref, gpad, *, t):
    halo = 16
    gpad[0:halo, :] = jnp.zeros((halo, CM_CH), F32)
    gpad[halo + t:halo + t + halo, :] = jnp.zeros((halo, CM_CH), F32)
    r = CONV_ROWS
    for blk in range(t // r):
        rows = slice(blk * r, (blk + 1) * r)
        gpad[halo + blk * r:halo + (blk + 1) * r, :] = (
            cm_ref[rows, 0:CM_CH] * jax.nn.sigmoid(cm_ref[rows, CM_CH:2 * CM_CH]))
    w = w_ref[...]
    bias = b_ref[...]
    lg = lg_ref[...]
    lb = lb_ref[...]
    off0 = halo - CM_WIDTH // 2
    rb = CM_ROWS

    def body(blk, carry):
        base = pl.multiple_of(blk * rb, rb)
        win = gpad[pl.ds(base, rb + 2 * halo), :]
        acc = bias + w[0:1, :] * win[off0:off0 + rb, :]
        for kk in range(1, CM_WIDTH):
            acc = acc + w[kk:kk + 1, :] * win[off0 + kk:off0 + kk + rb, :]
        mu = jnp.mean(acc, axis=-1, keepdims=True)
        xc = acc - mu
        var = jnp.mean(xc * xc, axis=-1, keepdims=True)
        y = xc * lax.rsqrt(var + 1e-5) * lg + lb
        o_ref[pl.ds(base, rb), :] = _silu(y).astype(BF16)
        return carry

    lax.fori_loop(0, t // rb, body, 0)


def _cm_call(cm, w, b, lg, lb, *, nb, t, row0):
    b0 = row0 // t

    def const(i):
        return (0, 0)

    return pl.pallas_call(
        functools.partial(_cm_kernel, t=t),
        out_shape=jax.ShapeDtypeStruct((nb * t, CM_CH), BF16),
        grid=(nb,),
        in_specs=[
            pl.BlockSpec((t, 2 * CM_CH), lambda i: (b0 + i, 0)),
            pl.BlockSpec((32, CM_CH), const),
            pl.BlockSpec((1, CM_CH), const),
            pl.BlockSpec((1, CM_CH), const),
            pl.BlockSpec((1, CM_CH), const),
        ],
        out_specs=pl.BlockSpec((t, CM_CH), lambda i: (i, 0)),
        scratch_shapes=[pltpu.VMEM((t + 32, CM_CH), F32)],
        compiler_params=_params(("parallel",)),
        name="conv_module",
    )(cm, w, b, lg, lb)


def _ffn_kernel(x_ref, attn_ref, ssm_ref, conv_ref, mod_ref, wout_ref, gffn_ref, wg_ref, wu_ref, wd_ref,
                gfin_ref, o_ref, *, final):
    n_attn = MLA_HEADS * V_HEAD
    mixed = _dot(attn_ref[...], wout_ref[0:n_attn, :])
    mixed = mixed + _dot(ssm_ref[...], wout_ref[n_attn:n_attn + SSD_INNER, :])
    mixed = mixed + _dot(conv_ref[...], wout_ref[n_attn + SSD_INNER:n_attn + SSD_INNER + CM_CH, :])
    g1 = mod_ref[0, :, 2 * D_MODEL:3 * D_MODEL]
    sh2 = mod_ref[0, :, 3 * D_MODEL:4 * D_MODEL]
    sc2 = mod_ref[0, :, 4 * D_MODEL:5 * D_MODEL]
    g2 = mod_ref[0, :, 5 * D_MODEL:6 * D_MODEL]
    x = x_ref[...] + g1 * mixed
    h = (_rms(x) * gffn_ref[...] * (1.0 + sc2) + sh2).astype(BF16)
    ff = jnp.zeros(x.shape, F32)
    for j in range(D_FF // FF_CHUNK):
        cols = slice(j * FF_CHUNK, (j + 1) * FF_CHUNK)
        gate = _dot(h, wg_ref[:, cols])
        up = _dot(h, wu_ref[:, cols])
        ff = ff + _dot((_silu(gate) * up).astype(BF16), wd_ref[cols, :])
    x = x + g2 * ff
    if final:
        x = _rms(x) * gfin_ref[...]
    o_ref[...] = x


def _ffn_call(x, attn, ssm, conv, mod_l, wout, gffn, wg, wu, wd, gfin, *, final):
    tm = TM_FFN
    nt = N_TOK // tm

    def tok(i):
        return (i, 0)

    def const(i):
        return (0, 0)

    def resident(shape):
        return pl.BlockSpec(shape, const, pipeline_mode=pl.Buffered(1))

    return pl.pallas_call(
        functools.partial(_ffn_kernel, final=final),
        out_shape=jax.ShapeDtypeStruct((N_TOK, D_MODEL), F32),
        grid=(nt,),
        in_specs=[
            pl.BlockSpec((tm, D_MODEL), tok),
            pl.BlockSpec((tm, MLA_HEADS * V_HEAD), tok),
            pl.BlockSpec((tm, SSD_INNER), tok),
            pl.BlockSpec((tm, CM_CH), tok),
            pl.BlockSpec((1, 1, N_MOD * D_MODEL), lambda i: (_mod_row(i, tm), 0, 0)),
            resident((D_MODEL, D_MODEL)),
            pl.BlockSpec((1, D_MODEL), const),
            resident((D_MODEL, D_FF)),
            resident((D_MODEL, D_FF)),
            resident((D_FF, D_MODEL)),
            pl.BlockSpec((1, D_MODEL), const),
        ],
        out_specs=pl.BlockSpec((tm, D_MODEL), tok),
        compiler_params=_params(("parallel",)),
        name="out_ffn",
    )(x, attn, ssm, conv, mod_l, wout, gffn, wg, wu, wd, gfin)


def _rope_tables():
    rows = DEC_SEQ // GRID_W
    row = jnp.repeat(jnp.arange(rows), GRID_W).astype(F32)
    col = (jnp.arange(rows * GRID_W) % GRID_W).astype(F32)
    nf = QK_ROPE // 4
    inv = ROPE_THETA ** (-jnp.arange(nf, dtype=F32) / nf)
    ang = jnp.stack([row[:, None] * inv, col[:, None] * inv], axis=1)
    cos = jnp.cos(ang)
    sin = jnp.sin(ang)
    cos32 = jnp.stack([cos, cos], axis=2).reshape(DEC_SEQ, QK_ROPE)
    sin32 = jnp.stack([-sin, sin], axis=2).reshape(DEC_SEQ, QK_ROPE)
    ones = jnp.ones((DEC_SEQ, QK_NOPE), F32)
    pad1 = jnp.ones((DEC_SEQ, HEAD_PAD - QK_HEAD), F32)
    cos_t = jnp.concatenate([ones, cos32, pad1], axis=1)
    sin_t = jnp.concatenate([0.0 * ones, sin32, 0.0 * pad1], axis=1)
    cos_t = jnp.concatenate([jnp.ones((TM_IN, HEAD_PAD), F32), cos_t], axis=0)
    sin_t = jnp.concatenate([jnp.zeros((TM_IN, HEAD_PAD), F32), sin_t], axis=0)
    return cos_t, sin_t


def _swap_halves(w):
    s = w.shape[:-1]
    return jnp.flip(w.reshape(s + (2, 2, QK_ROPE // 4)), axis=-2).reshape(s + (QK_ROPE,))


def _arrange_weights(w_in, w_uq, w_ukv):
    L = DEPTH
    zeros = lambda n, rows: jnp.zeros((L, rows, n), F32)
    mla_in = Q_RANK + KV_RANK + QK_ROPE
    off_ssd = mla_in
    off_cm = mla_in + SSD_INNER + SSD_XBC + 2 * SSD_HEADS
    kr_w = w_in[:, :, Q_RANK + KV_RANK:mla_in]
    kr_pad = lambda w: jnp.concatenate(
        [zeros(QK_NOPE, D_MODEL), w, zeros(HEAD_PAD - QK_HEAD, D_MODEL)], axis=-1)
    win = jnp.concatenate([
        w_in[:, :, 0:Q_RANK + KV_RANK],
        kr_pad(kr_w), kr_pad(_swap_halves(kr_w)),
        w_in[:, :, off_ssd:off_ssd + SSD_INNER + SSD_XBC],
        w_in[:, :, off_ssd + SSD_INNER + SSD_XBC:off_cm], zeros(LANE - 2 * SSD_HEADS, D_MODEL),
        w_in[:, :, off_cm:],
    ], axis=-1).astype(BF16)

    uq = w_uq.reshape(L, Q_RANK, MLA_HEADS, QK_HEAD)
    z_n = jnp.zeros((L, Q_RANK, MLA_HEADS, QK_NOPE), F32)
    z_p = jnp.zeros((L, Q_RANK, MLA_HEADS, HEAD_PAD - QK_HEAD), F32)
    uq_a = jnp.concatenate([uq, z_p], axis=-1)
    uq_b = jnp.concatenate([z_n, _swap_halves(uq[..., QK_NOPE:]), z_p], axis=-1)
    wuq = jnp.concatenate([uq_a.reshape(L, Q_RANK, -1), uq_b.reshape(L, Q_RANK, -1)], axis=-1).astype(BF16)

    ukv = w_ukv.reshape(L, KV_RANK, MLA_HEADS, QK_NOPE + V_HEAD)
    z_k = jnp.zeros((L, KV_RANK, MLA_HEADS, HEAD_PAD - QK_NOPE), F32)
    k_part = jnp.concatenate([ukv[..., :QK_NOPE], z_k], axis=-1).reshape(L, KV_RANK, -1)
    v_part = ukv[..., QK_NOPE:].reshape(L, KV_RANK, -1)
    wukv = jnp.concatenate([k_part, v_part], axis=-1).astype(BF16)
    return win, wuq, wukv


def _kr_placement():
    src = jnp.arange(QK_ROPE)
    cols = jnp.arange(MLA_HEADS * HEAD_PAD)
    hit = (cols[None, :] % HEAD_PAD) == (QK_NOPE + src[:, None])
    return hit.astype(BF16)


def _pad_states(h0):
    b = h0.shape[0]
    s = h0.reshape(b, 2, SSD_GROUPS, 2, SSD_HEAD_DIM, SSD_STATE)
    s = s.transpose(0, 1, 2, 5, 3, 4).reshape(b, 2, SSD_GROUPS, SSD_STATE, LANE)
    zero = jnp.zeros_like(s[:, :, 0])
    g0 = jnp.concatenate([s[:, :, 0], zero], axis=-2)
    g1 = jnp.concatenate([zero, s[:, :, 1]], axis=-2)
    return jnp.stack([g0, g1], axis=2)


def _unpack_states(st):
    b = st.shape[0]
    s = st.reshape(b, 2, SSD_GROUPS, SSD_STATE, 2, SSD_HEAD_DIM).transpose(0, 1, 2, 4, 5, 3)
    return s.reshape(b, 2, SSD_HEADS, SSD_HEAD_DIM, SSD_STATE)


def kernel(x_prompt, x_sample, c, cache_ckv, cache_krope, state_ssd, c_ctx, w_ada, b_ada, g_mix, w_in, g_q, w_uq, g_kv, w_ukv, ssd_conv_w, ssd_conv_b, ssd_dt_bias, ssd_a_log, ssd_d, ssd_norm_g, cm_conv_w, cm_conv_b, cm_ln_g, cm_ln_b, w_out, g_ffn, w_gate, w_up, w_down, g_final):
    L = DEPTH
    win, wuq, wukv = _arrange_weights(w_in, w_uq, w_ukv)
    wout = w_out.astype(BF16)
    wg = w_gate.astype(BF16)
    wu = w_up.astype(BF16)
    wd = w_down.astype(BF16)
    cos_t, sin_t = _rope_tables()

    cond8 = jnp.concatenate([c_ctx[None, :], c, jnp.zeros((8 - 1 - DEC_BATCH, D_MODEL), F32)], axis=0)
    mod = _ada_call(cond8, w_ada, b_ada)
    mod = mod.reshape(L, 8, 1, N_MOD * D_MODEL)

    kc_all, vc_all = _ctxkv_call(cache_ckv, cache_krope, wukv, _kr_placement())

    pad8 = lambda w: jnp.concatenate([w, jnp.zeros((L, 8 - SSD_CONV, SSD_XBC), F32)], axis=1)
    ssd_cw = pad8(ssd_conv_w)
    cm_w = jnp.concatenate([cm_conv_w, jnp.zeros((L, 32 - CM_WIDTH, CM_CH), F32)], axis=1)
    lane_pad = lambda v: jnp.concatenate(
        [v.reshape(L, 1, 2 * SSD_HEADS), jnp.zeros((L, 1, LANE - 2 * SSD_HEADS), F32)], axis=-1)
    dtb = lane_pad(ssd_dt_bias)
    alog = lane_pad(ssd_a_log)
    d_rep = jnp.repeat(ssd_d, SSD_HEAD_DIM, axis=-1)
    h0_lat = _pad_states(state_ssd.transpose(1, 0, 2, 3, 4, 5).reshape(L * DEC_BATCH, 2, SSD_HEADS,
                                                                     SSD_HEAD_DIM, SSD_STATE))
    h0_lat = h0_lat.reshape(L, DEC_BATCH, 2, SSD_GROUPS, LANE, LANE)

    x = jnp.concatenate([x_prompt.reshape(N_CTX, D_MODEL), x_sample.reshape(N_LAT, D_MODEL)], axis=0)
    row = lambda v: v.reshape(1, -1)
    ckvs, krs, hss = [], [], []
    for l in range(L):
        q, k, v, ckv, kr, z, xbc, dt, cm = _inproj_call(
            x, mod[l], row(g_mix[l]), win[l], row(g_q[l]), wuq[l], row(g_kv[l]), wukv[l], cos_t, sin_t)

        attn_c = _attn_call(q, k, v, None, None, nb=BATCH, t=SEQ, tq=SEQ, row0=0)
        attn_l = _attn_call(q, k, v, kc_all[l], vc_all[l], nb=DEC_BATCH, t=DEC_SEQ, tq=TQ_LAT, row0=N_CTX)

        ssd_args = (z, xbc, dt, ssd_cw[l], row(ssd_conv_b[l]), dtb[l], alog[l],
                    row(d_rep[l, 0]), row(d_rep[l, 1]), row(ssd_norm_g[l]))
        ssm_c, hf_c = _ssd_call(*ssd_args, None, nb=BATCH, t=SEQ, row0=0)
        ssm_l, _ = _ssd_call(*ssd_args, h0_lat[l], nb=DEC_BATCH, t=DEC_SEQ, row0=N_CTX)

        cm_args = (cm, cm_w[l], row(cm_conv_b[l]), row(cm_ln_g[l]), row(cm_ln_b[l]))
        conv_c = _cm_call(*cm_args, nb=BATCH, t=SEQ, row0=0)
        conv_l = _cm_call(*cm_args, nb=DEC_BATCH, t=DEC_SEQ, row0=N_CTX)

        attn = jnp.concatenate([attn_c, attn_l], axis=0)
        ssm = jnp.concatenate([ssm_c, ssm_l], axis=0)
        conv = jnp.concatenate([conv_c, conv_l], axis=0)
        x = _ffn_call(x, attn, ssm, conv, mod[l], wout[l], row(g_ffn[l]), wg[l], wu[l], wd[l],
                      row(g_final), final=(l == L - 1))

        ckvs.append(ckv[:N_CTX].reshape(BATCH, SEQ, KV_RANK))
        krs.append(kr[:N_CTX, QK_NOPE:QK_HEAD].reshape(BATCH, SEQ, QK_ROPE))
        hss.append(_unpack_states(hf_c))

    y_prompt = x[:N_CTX].reshape(BATCH, SEQ, D_MODEL)
    y_sample = x[N_CTX:].reshape(DEC_BATCH, DEC_SEQ, D_MODEL)
    return (y_prompt, y_sample, jnp.stack(ckvs, axis=1), jnp.stack(krs, axis=1), jnp.stack(hss, axis=1))
```

```python
import functools

import numpy as np

import jax
import jax.numpy as jnp
from jax import lax
from jax.experimental import pallas as pl
from jax.experimental.pallas import tpu as pltpu

F32 = jnp.float32
BF16 = jnp.bfloat16

D_MODEL = 1024
BATCH = 16
SEQ = 256
DEPTH = 4
DEC_BATCH = 4
DEC_SEQ = 2048
PAST_LEN = 256
GRID_W = 64

MLA_HEADS = 8
QK_NOPE = 64
QK_ROPE = 32
QK_HEAD = QK_NOPE + QK_ROPE
V_HEAD = 64
Q_RANK = 256
KV_RANK = 128
ROPE_THETA = 10000.0

SSD_HEADS = 4
SSD_HEAD_DIM = 64
SSD_INNER = SSD_HEADS * SSD_HEAD_DIM
SSD_GROUPS = 2
SSD_STATE = 64
SSD_CONV = 5
SSD_CHUNK = 128
SSD_XBC = SSD_INNER + 2 * SSD_GROUPS * SSD_STATE

CM_CH = 256
CM_WIDTH = 31

D_FF = 2816
N_MOD = 6
IN_WIDTH = (Q_RANK + KV_RANK + QK_ROPE) + (SSD_INNER + SSD_XBC + 2 * SSD_HEADS) + 2 * CM_CH

N_CTX = BATCH * SEQ
N_LAT = DEC_BATCH * DEC_SEQ

LANE = 128
MXU_N = 256
HEAD_PAD = 128
HEADS_PER_STEP = MXU_N // V_HEAD
VMEM_LIMIT = 48 * 1024 * 1024

TM_IN = 256
TM_FFN = 512
TQ_LAT = 512
FF_CHUNK = 256
CONV_ROWS = 128
CM_ROWS = 64

C_QLAT = 0
C_CKV = C_QLAT + Q_RANK
C_KR = C_CKV + KV_RANK
C_Z = C_KR + 2 * HEAD_PAD
C_XBC = C_Z + SSD_INNER
C_DT = C_XBC + SSD_XBC
C_CM = C_DT + LANE
IN_ARR = C_CM + 2 * CM_CH


def _params(sem):
    return pltpu.CompilerParams(dimension_semantics=sem, vmem_limit_bytes=VMEM_LIMIT)


def _dot(a, b):
    return jnp.dot(a, b, preferred_element_type=F32)


def _dot_nt(a, b):
    return lax.dot_general(a, b, (((1,), (1,)), ((), ())), preferred_element_type=F32)


def _rms(x, eps=1e-6):
    return x * lax.rsqrt(jnp.mean(x * x, axis=-1, keepdims=True) + eps)


def _silu(x):
    return x * jax.nn.sigmoid(x)


def _place_kernel(w_ref, p_ref, o_ref):
    o_ref[0] = _dot(w_ref[0].astype(BF16), p_ref[...]).astype(BF16)


def _place_call(w, src, name):
    L, rows, cin = w.shape
    cout = src.shape[0]
    place = (jnp.arange(cin, dtype=jnp.int32)[:, None] == jnp.asarray(src, jnp.int32)[None, :]).astype(BF16)
    tr = min(rows, 256)
    return pl.pallas_call(
        _place_kernel,
        out_shape=jax.ShapeDtypeStruct((L, rows, cout), BF16),
        grid=(L, rows // tr),
        in_specs=[
            pl.BlockSpec((1, tr, cin), lambda l, i: (l, i, 0)),
            pl.BlockSpec((cin, cout), lambda l, i: (0, 0)),
        ],
        out_specs=pl.BlockSpec((1, tr, cout), lambda l, i: (l, i, 0)),
        compiler_params=_params(("parallel", "parallel")),
        name=name,
    )(w, place)


def _swap32():
    return np.arange(QK_ROPE).reshape(2, 2, QK_ROPE // 4)[:, ::-1, :].reshape(-1)


def _src_in():
    kr0 = Q_RANK + KV_RANK
    off_ssd = kr0 + QK_ROPE
    off_dt = off_ssd + SSD_INNER + SSD_XBC
    off_cm = off_dt + 2 * SSD_HEADS
    src = -np.ones((IN_ARR,), np.int64)
    src[0:C_KR] = np.arange(0, kr0)
    src[C_KR + QK_NOPE:C_KR + QK_HEAD] = kr0 + np.arange(QK_ROPE)
    src[C_KR + HEAD_PAD + QK_NOPE:C_KR + HEAD_PAD + QK_HEAD] = kr0 + _swap32()
    src[C_Z:C_DT] = np.arange(off_ssd, off_dt)
    src[C_DT:C_DT + 2 * SSD_HEADS] = np.arange(off_dt, off_cm)
    src[C_CM:IN_ARR] = np.arange(off_cm, IN_WIDTH)
    return src


def _src_uq():
    half = MLA_HEADS * HEAD_PAD
    src = -np.ones((2 * half,), np.int64)
    for h in range(MLA_HEADS):
        src[h * HEAD_PAD:h * HEAD_PAD + QK_HEAD] = h * QK_HEAD + np.arange(QK_HEAD)
        lo = half + h * HEAD_PAD + QK_NOPE
        src[lo:lo + QK_ROPE] = h * QK_HEAD + QK_NOPE + _swap32()
    return src


def _src_ukv():
    half = MLA_HEADS * HEAD_PAD
    src = -np.ones((half + MLA_HEADS * V_HEAD,), np.int64)
    for h in range(MLA_HEADS):
        base = h * (QK_NOPE + V_HEAD)
        src[h * HEAD_PAD:h * HEAD_PAD + QK_NOPE] = base + np.arange(QK_NOPE)
        src[half + h * V_HEAD:half + (h + 1) * V_HEAD] = base + QK_NOPE + np.arange(V_HEAD)
    return src


def _ada_kernel(cond_ref, w_ref, b_ref, o_ref):
    c = cond_ref[...]
    a = _silu(c).astype(BF16)
    o_ref[0] = _dot(a, w_ref[0].astype(BF16)) + b_ref[0]


def _ada_call(cond8, w_ada, b_ada):
    tn = 1536
    nt = (N_MOD * D_MODEL) // tn
    return pl.pallas_call(
        _ada_kernel,
        out_shape=jax.ShapeDtypeStruct((DEPTH, 8, N_MOD * D_MODEL), F32),
        grid=(DEPTH, nt),
        in_specs=[
            pl.BlockSpec((8, D_MODEL), lambda l, j: (0, 0)),
            pl.BlockSpec((1, D_MODEL, tn), lambda l, j: (l, 0, j)),
            pl.BlockSpec((1, 1, tn), lambda l, j: (l, 0, j)),
        ],
        out_specs=pl.BlockSpec((1, 8, tn), lambda l, j: (l, 0, j)),
        compiler_params=_params(("parallel", "parallel")),
        name="ada_mod",
    )(cond8, w_ada, b_ada.reshape(DEPTH, 1, N_MOD * D_MODEL))


def _mod_spec(tm, tmod, mod_row0):
    return pl.BlockSpec((1, 1, N_MOD * D_MODEL), lambda i: (mod_row0 + (i * tm) // tmod, 0, 0))


def _inproj_kernel(*refs, rope):
    if rope:
        (x_ref, mod_ref, gmix_ref, win_ref, gq_ref, wuq_ref, gkv_ref, wukv_ref, cos_ref, sin_ref,
         q_out, k_out, v_out, ckv_out, kr_out, z_out, xbc_out, dt_out, cm_out) = refs
        cos = cos_ref[...]
        sin = sin_ref[...]
    else:
        (x_ref, mod_ref, gmix_ref, win_ref, gq_ref, wuq_ref, gkv_ref, wukv_ref,
         q_out, k_out, v_out, ckv_out, kr_out, z_out, xbc_out, dt_out, cm_out) = refs
    x = x_ref[...]
    sh1 = mod_ref[0, :, 0:D_MODEL]
    sc1 = mod_ref[0, :, D_MODEL:2 * D_MODEL]
    h = (_rms(x) * gmix_ref[...] * (1.0 + sc1) + sh1).astype(BF16)

    q_lat = _dot(h, win_ref[:, C_QLAT:C_QLAT + Q_RANK])
    qn = (_rms(q_lat) * gq_ref[...]).astype(BF16)
    qq = _dot(qn, wuq_ref[...])
    scale = QK_HEAD ** -0.5
    half = MLA_HEADS * HEAD_PAD
    for hd in range(MLA_HEADS):
        lo = hd * HEAD_PAD
        qh = qq[:, lo:lo + HEAD_PAD]
        if rope:
            qh = qh * cos + qq[:, half + lo:half + lo + HEAD_PAD] * sin
        q_out[:, lo:lo + HEAD_PAD] = (qh * scale).astype(BF16)

    ckv = _rms(_dot(h, win_ref[:, C_CKV:C_CKV + KV_RANK])) * gkv_ref[...]
    ckv_out[...] = ckv
    kv = _dot(ckv.astype(BF16), wukv_ref[...])
    if rope:
        kr2 = _dot(h, win_ref[:, C_KR:C_KR + 2 * HEAD_PAD])
        kr = kr2[:, 0:HEAD_PAD]
        kr_rot = kr * cos + kr2[:, HEAD_PAD:2 * HEAD_PAD] * sin
    else:
        kr = _dot(h, win_ref[:, C_KR:C_KR + HEAD_PAD])
        kr_rot = kr
    kr_out[...] = kr
    for hd in range(MLA_HEADS):
        lo = hd * HEAD_PAD
        k_out[:, lo:lo + HEAD_PAD] = (kv[:, lo:lo + HEAD_PAD] + kr_rot).astype(BF16)
    v_out[...] = kv[:, half:half + MLA_HEADS * V_HEAD].astype(BF16)

    z_out[...] = _dot(h, win_ref[:, C_Z:C_Z + SSD_INNER])
    xbc_out[...] = _dot(h, win_ref[:, C_XBC:C_XBC + SSD_XBC])
    dt_out[...] = _dot(h, win_ref[:, C_DT:C_DT + LANE])
    cm_out[...] = _dot(h, win_ref[:, C_CM:C_CM + 2 * CM_CH])


def _inproj_call(x, mod_l, gmix, win, gq, wuq, gkv, wukv, rope_tabs, *, t, tmod, mod_row0):
    n = x.shape[0]
    tm = TM_IN
    rope = rope_tabs is not None
    half = MLA_HEADS * HEAD_PAD

    def tok(i):
        return (i, 0)

    def const(i):
        return (0, 0)

    outs = [
        ((n, half), BF16),
        ((n, half), BF16),
        ((n, MLA_HEADS * V_HEAD), BF16),
        ((n, KV_RANK), F32),
        ((n, HEAD_PAD), F32),
        ((n, SSD_INNER), F32),
        ((n, SSD_XBC), F32),
        ((n, LANE), F32),
        ((n, 2 * CM_CH), F32),
    ]
    in_specs = [
        pl.BlockSpec((tm, D_MODEL), tok),
        _mod_spec(tm, tmod, mod_row0),
        pl.BlockSpec((1, D_MODEL), const),
        pl.BlockSpec((D_MODEL, IN_ARR), const),
        pl.BlockSpec((1, Q_RANK), const),
        pl.BlockSpec((Q_RANK, 2 * half if rope else half), const),
        pl.BlockSpec((1, KV_RANK), const),
        pl.BlockSpec((KV_RANK, MLA_HEADS * (HEAD_PAD + V_HEAD)), const),
    ]
    args = [x, mod_l, gmix, win, gq, wuq, gkv, wukv]
    if rope:
        pos = lambda i: (i % (t // tm), 0)
        in_specs += [pl.BlockSpec((tm, HEAD_PAD), pos), pl.BlockSpec((tm, HEAD_PAD), pos)]
        args += list(rope_tabs)
    return pl.pallas_call(
        functools.partial(_inproj_kernel, rope=rope),
        out_shape=[jax.ShapeDtypeStruct(s, d) for s, d in outs],
        grid=(n // tm,),
        in_specs=in_specs,
        out_specs=[pl.BlockSpec((tm, s[1]), tok) for s, _ in outs],
        compiler_params=_params(("parallel",)),
        name="in_proj_lat" if rope else "in_proj_ctx",
    )(*args)


def _ctxkv_kernel(ckv_ref, kr_ref, wukv_ref, place_ref, k_out, v_out):
    kv = _dot(ckv_ref[0, 0].astype(BF16), wukv_ref[0])
    krp = _dot(kr_ref[0, 0].astype(BF16), place_ref[...])
    half = MLA_HEADS * HEAD_PAD
    k_out[0, 0] = (kv[:, 0:half] + krp).astype(BF16)
    v_out[0, 0] = kv[:, half:half + MLA_HEADS * V_HEAD].astype(BF16)


def _ctxkv_call(cache_ckv, cache_krope, wukv_all, place):
    return pl.pallas_call(
        _ctxkv_kernel,
        out_shape=[
            jax.ShapeDtypeStruct((DEPTH, DEC_BATCH, PAST_LEN, MLA_HEADS * HEAD_PAD), BF16),
            jax.ShapeDtypeStruct((DEPTH, DEC_BATCH, PAST_LEN, MLA_HEADS * V_HEAD), BF16),
        ],
        grid=(DEPTH, DEC_BATCH),
        in_specs=[
            pl.BlockSpec((1, 1, PAST_LEN, KV_RANK), lambda l, b: (b, l, 0, 0)),
            pl.BlockSpec((1, 1, PAST_LEN, QK_ROPE), lambda l, b: (b, l, 0, 0)),
            pl.BlockSpec((1, KV_RANK, MLA_HEADS * (HEAD_PAD + V_HEAD)), lambda l, b: (l, 0, 0)),
            pl.BlockSpec((QK_ROPE, MLA_HEADS * HEAD_PAD), lambda l, b: (0, 0)),
        ],
        out_specs=[
            pl.BlockSpec((1, 1, PAST_LEN, MLA_HEADS * HEAD_PAD), lambda l, b: (l, b, 0, 0)),
            pl.BlockSpec((1, 1, PAST_LEN, MLA_HEADS * V_HEAD), lambda l, b: (l, b, 0, 0)),
        ],
        compiler_params=_params(("parallel", "parallel")),
        name="ctx_kv",
    )(cache_ckv, cache_krope, wukv_all, place)


def _attn_kernel(*refs, has_ctx, ngroups):
    if has_ctx:
        q_ref, k_ref, v_ref, kc_ref, vc_ref, o_ref = refs
    else:
        q_ref, k_ref, v_ref, o_ref = refs
    gw = HEADS_PER_STEP * V_HEAD
    for grp in range(ngroups):
        vg = v_ref[:, grp * gw:(grp + 1) * gw]
        if has_ctx:
            vcg = vc_ref[0, :, grp * gw:(grp + 1) * gw]
        o_sel = None
        den_sel = None
        for hh in range(HEADS_PER_STEP):
            lo = (grp * HEADS_PER_STEP + hh) * HEAD_PAD
            qh = q_ref[:, lo:lo + HEAD_PAD]
            s = _dot_nt(qh, k_ref[:, lo:lo + HEAD_PAD])
            m = jnp.max(s, axis=-1, keepdims=True)
            if has_ctx:
                sc = _dot_nt(qh, kc_ref[0, :, lo:lo + HEAD_PAD])
                m = jnp.maximum(m, jnp.max(sc, axis=-1, keepdims=True))
            p = jnp.exp(s - m)
            den = jnp.sum(p, axis=-1, keepdims=True)
            o = _dot(p.astype(BF16), vg)
            if has_ctx:
                pc = jnp.exp(sc - m)
                den = den + jnp.sum(pc, axis=-1, keepdims=True)
                o = o + _dot(pc.astype(BF16), vcg)
            if hh == 0:
                o_sel = o
                den_sel = jnp.broadcast_to(den, o.shape)
            else:
                mine = lax.broadcasted_iota(jnp.int32, o.shape, 1) >= hh * V_HEAD
                o_sel = jnp.where(mine, o, o_sel)
                den_sel = jnp.where(mine, den, den_sel)
        o_ref[:, grp * gw:(grp + 1) * gw] = (o_sel / den_sel).astype(BF16)


def _attn_call(q, k, v, kc, vc, *, nb, t, tq, ngroups):
    has_ctx = kc is not None
    nq = t // tq
    nh = ngroups * HEADS_PER_STEP
    nsteps = MLA_HEADS // nh
    in_specs = [
        pl.BlockSpec((tq, nh * HEAD_PAD), lambda b, g, qi: (b * nq + qi, g)),
        pl.BlockSpec((t, nh * HEAD_PAD), lambda b, g, qi: (b, g)),
        pl.BlockSpec((t, nh * V_HEAD), lambda b, g, qi: (b, g)),
    ]
    args = [q, k, v]
    if has_ctx:
        in_specs += [
            pl.BlockSpec((1, PAST_LEN, nh * HEAD_PAD), lambda b, g, qi: (b, 0, g)),
            pl.BlockSpec((1, PAST_LEN, nh * V_HEAD), lambda b, g, qi: (b, 0, g)),
        ]
        args += [kc, vc]
    return pl.pallas_call(
        functools.partial(_attn_kernel, has_ctx=has_ctx, ngroups=ngroups),
        out_shape=jax.ShapeDtypeStruct((nb * t, MLA_HEADS * V_HEAD), BF16),
        grid=(nb, nsteps, nq),
        in_specs=in_specs,
        out_specs=pl.BlockSpec((tq, nh * V_HEAD), lambda b, g, qi: (b * nq + qi, g)),
        compiler_params=_params(("parallel", "parallel", "arbitrary")),
        name="attn_lat" if has_ctx else "attn_ctx",
    )(*args)


def _split3(x):
    hi = x.astype(BF16)
    r = x - hi.astype(F32)
    mid = r.astype(BF16)
    lo = (r - mid.astype(F32)).astype(BF16)
    return hi, mid, lo


def _ssd_kernel(*refs, t, has_h0):
    if has_h0:
        (z_ref, xbc_ref, dt_ref, cw_ref, cb_ref, dtb_ref, alog_ref, d0_ref, d1_ref, gn_ref, h0_ref,
         y_out, hf_out, xpad, xc, yf, yb, st) = refs
    else:
        (z_ref, xbc_ref, dt_ref, cw_ref, cb_ref, dtb_ref, alog_ref, d0_ref, d1_ref, gn_ref,
         y_out, hf_out, xpad, xc, yf, yb, st) = refs
    L = SSD_CHUNK
    nc = t // L
    halo = 8

    xpad[0:halo, :] = jnp.zeros((halo, SSD_XBC), F32)
    xpad[halo + t:halo + t + halo, :] = jnp.zeros((halo, SSD_XBC), F32)
    xpad[halo:halo + t, :] = xbc_ref[...]
    cw = cw_ref[...]
    cb = cb_ref[...]
    r = CONV_ROWS
    for blk in range(t // r):
        base = blk * r
        acc = cb + cw[0:1, :] * xpad[base + halo - 2:base + halo - 2 + r, :]
        for kk in range(1, SSD_CONV):
            o = base + halo - 2 + kk
            acc = acc + cw[kk:kk + 1, :] * xpad[o:o + r, :]
        xc[base:base + r, :] = _silu(acc)

    if has_h0:
        st[...] = h0_ref[0]
    else:
        st[...] = jnp.zeros(st.shape, F32)

    a_all = -jnp.exp(alog_ref[...])
    dtb = dtb_ref[...]
    row_i = lax.broadcasted_iota(jnp.int32, (L, L), 0)
    col_i = lax.broadcasted_iota(jnp.int32, (L, L), 1)
    lane_lo = col_i < SSD_HEAD_DIM
    tri = [(row_i >= col_i), (row_i <= col_i)]
    tri_bf = [jnp.where(m, 1.0, 0.0).astype(BF16) for m in tri]
    grp_rows = [row_i < SSD_STATE, row_i >= SSD_STATE]

    def chunk(c, d, y_acc):
        r0 = pl.multiple_of(c * L, L)
        rows = pl.ds(r0, L)
        xs = xc[rows, 0:SSD_INNER]
        bm = xc[rows, SSD_INNER:SSD_INNER + LANE]
        cm = xc[rows, SSD_INNER + LANE:SSD_INNER + 2 * LANE].astype(BF16)
        dtp = jax.nn.softplus(dt_ref[rows, :] + dtb)
        adt = dtp * a_all
        hi, mid, lo = _split3(adt)
        acs = _dot(tri_bf[d], hi) + _dot(tri_bf[d], mid) + _dot(tri_bf[d], lo)
        acs_t = acs.T
        bt = bm.T
        for g in range(SSD_GROUPS):
            btm = jnp.where(grp_rows[g], bt, 0.0).astype(BF16)
            gmat = _dot(cm, btm)
            j0 = d * SSD_HEADS + 2 * g
            j1 = j0 + 1
            ys = []
            for j in (j0, j1):
                diff = acs[:, j:j + 1] - acs_t[j:j + 1, :]
                seg = jnp.exp(jnp.where(tri[d], diff, -jnp.inf))
                ys.append((gmat * seg).astype(BF16))
            dt_pair = jnp.where(lane_lo, dtp[:, j0:j0 + 1], dtp[:, j1:j1 + 1])
            acs_pair = jnp.where(lane_lo, acs[:, j0:j0 + 1], acs[:, j1:j1 + 1])
            xdt = xs[:, g * LANE:(g + 1) * LANE] * dt_pair
            xdt_bf = xdt.astype(BF16)
            y_pair = jnp.where(lane_lo, _dot(ys[0], xdt_bf), _dot(ys[1], xdt_bf))
            h_in = st[d, g]
            y_pair = y_pair + _dot(cm, h_in.astype(BF16)) * jnp.exp(acs_pair)
            tot = acs_pair[L - 1:L, :] if d == 0 else acs_pair[0:1, :]
            s_new = _dot(btm, (xdt * jnp.exp(tot - acs_pair)).astype(BF16))
            st[d, g] = jnp.exp(tot) * h_in + s_new
            y_acc[rows, g * LANE:(g + 1) * LANE] = y_pair

    def body(i, carry):
        chunk(i, 0, yf)
        chunk(nc - 1 - i, 1, yb)
        return carry

    lax.fori_loop(0, nc, body, 0)
    for d in range(2):
        for g in range(SSD_GROUPS):
            hf_out[0, d, g] = st[d, g, g * SSD_STATE:(g + 1) * SSD_STATE, :]

    d0 = d0_ref[...]
    d1 = d1_ref[...]
    gn = gn_ref[...]
    for blk in range(t // r):
        rows = slice(blk * r, (blk + 1) * r)
        xs = xc[rows, 0:SSD_INNER]
        y = yf[rows, :] + yb[rows, :] + d0 * xs + d1 * xs
        y = y * _silu(z_ref[rows, :])
        y_out[rows, :] = (_rms(y) * gn).astype(BF16)


def _ssd_call(z, xbc, dt, cw, cb, dtb, alog, d0, d1, gn, h0, *, nb, t):
    has_h0 = h0 is not None

    def tok(b):
        return (b, 0)

    def const(b):
        return (0, 0)

    in_specs = [
        pl.BlockSpec((t, SSD_INNER), tok),
        pl.BlockSpec((t, SSD_XBC), tok),
        pl.BlockSpec((t, LANE), tok),
        pl.BlockSpec((8, SSD_XBC), const),
        pl.BlockSpec((1, SSD_XBC), const),
        pl.BlockSpec((1, LANE), const),
        pl.BlockSpec((1, LANE), const),
        pl.BlockSpec((1, SSD_INNER), const),
        pl.BlockSpec((1, SSD_INNER), const),
        pl.BlockSpec((1, SSD_INNER), const),
    ]
    args = [z, xbc, dt, cw, cb, dtb, alog, d0, d1, gn]
    if has_h0:
        in_specs.append(pl.BlockSpec((1, 2, SSD_GROUPS, LANE, LANE), lambda b: (b, 0, 0, 0, 0)))
        args.append(h0)
    return pl.pallas_call(
        functools.partial(_ssd_kernel, t=t, has_h0=has_h0),
        out_shape=[
            jax.ShapeDtypeStruct((nb * t, SSD_INNER), BF16),
            jax.ShapeDtypeStruct((nb, 2, SSD_GROUPS, SSD_STATE, LANE), F32),
        ],
        grid=(nb,),
        in_specs=in_specs,
        out_specs=[
            pl.BlockSpec((t, SSD_INNER), tok),
            pl.BlockSpec((1, 2, SSD_GROUPS, SSD_STATE, LANE), lambda b: (b, 0, 0, 0, 0)),
        ],
        scratch_shapes=[
            pltpu.VMEM((t + 16, SSD_XBC), F32),
            pltpu.VMEM((t, SSD_XBC), F32),
            pltpu.VMEM((t, SSD_INNER), F32),
            pltpu.VMEM((t, SSD_INNER), F32),
            pltpu.VMEM((2, SSD_GROUPS, LANE, LANE), F32),
        ],
        compiler_params=_params(("parallel",)),
        name="ssd_lat" if has_h0 else "ssd_ctx",
    )(*args)


def _cm_kernel(cm_ref, w_ref, b_ref, lg_ref, lb_ref, o_ref, gpad, *, t):
    halo = 16
    gpad[0:halo, :] = jnp.zeros((halo, CM_CH), F32)
    gpad[halo + t:halo + t + halo, :] = jnp.zeros((halo, CM_CH), F32)
    r = CONV_ROWS
    for blk in range(t // r):
        rows = slice(blk * r, (blk + 1) * r)
        gpad[halo + blk * r:halo + (blk + 1) * r, :] = (
            cm_ref[rows, 0:CM_CH] * jax.nn.sigmoid(cm_ref[rows, CM_CH:2 * CM_CH]))
    w = w_ref[...]
    bias = b_ref[...]
    lg = lg_ref[...]
    lb = lb_ref[...]
    rb = CM_ROWS
    sub = 8
    assert halo - CM_WIDTH // 2 == 1 and CM_WIDTH <= 4 * sub

    def body(blk, carry):
        base = pl.multiple_of(blk * rb, rb)
        acc = bias
        for j in range(sub):
            pj = None
            for a in range(4):
                kk = sub * a + j
                if kk < CM_WIDTH:
                    term = w[kk:kk + 1, :] * gpad[pl.ds(base + sub * a, rb + sub), :]
                    pj = term if pj is None else pj + term
            acc = acc + pj[j + 1:j + 1 + rb, :]
        mu = jnp.mean(acc, axis=-1, keepdims=True)
        xc = acc - mu
        var = jnp.mean(xc * xc, axis=-1, keepdims=True)
        y = xc * lax.rsqrt(var + 1e-5) * lg + lb
        o_ref[pl.ds(base, rb), :] = _silu(y).astype(BF16)
        return carry

    lax.fori_loop(0, t // rb, body, 0)


def _cm_call(cm, w, b, lg, lb, *, nb, t):
    def const(i):
        return (0, 0)

    return pl.pallas_call(
        functools.partial(_cm_kernel, t=t),
        out_shape=jax.ShapeDtypeStruct((nb * t, CM_CH), BF16),
        grid=(nb,),
        in_specs=[
            pl.BlockSpec((t, 2 * CM_CH), lambda i: (i, 0)),
            pl.BlockSpec((32, CM_CH), const),
            pl.BlockSpec((1, CM_CH), const),
            pl.BlockSpec((1, CM_CH), const),
            pl.BlockSpec((1, CM_CH), const),
        ],
        out_specs=pl.BlockSpec((t, CM_CH), lambda i: (i, 0)),
        scratch_shapes=[pltpu.VMEM((t + 32, CM_CH), F32)],
        compiler_params=_params(("parallel",)),
        name="conv_module",
    )(cm, w, b, lg, lb)


def _ffn_kernel(x_ref, attn_ref, ssm_ref, conv_ref, mod_ref, wout_ref, gffn_ref, wg_ref, wu_ref, wd_ref,
                gfin_ref, o_ref, *, final):
    n_attn = MLA_HEADS * V_HEAD
    mixed = _dot(attn_ref[...], wout_ref[0:n_attn, :])
    mixed = mixed + _dot(ssm_ref[...], wout_ref[n_attn:n_attn + SSD_INNER, :])
    mixed = mixed + _dot(conv_ref[...], wout_ref[n_attn + SSD_INNER:n_attn + SSD_INNER + CM_CH, :])
    g1 = mod_ref[0, :, 2 * D_MODEL:3 * D_MODEL]
    sh2 = mod_ref[0, :, 3 * D_MODEL:4 * D_MODEL]
    sc2 = mod_ref[0, :, 4 * D_MODEL:5 * D_MODEL]
    g2 = mod_ref[0, :, 5 * D_MODEL:6 * D_MODEL]
    x = x_ref[...] + g1 * mixed
    h = (_rms(x) * gffn_ref[...] * (1.0 + sc2) + sh2).astype(BF16)
    ff = jnp.zeros(x.shape, F32)
    for j in range(D_FF // FF_CHUNK):
        cols = slice(j * FF_CHUNK, (j + 1) * FF_CHUNK)
        gate = _dot(h, wg_ref[:, cols])
        up = _dot(h, wu_ref[:, cols])
        ff = ff + _dot((_silu(gate) * up).astype(BF16), wd_ref[cols, :])
    x = x + g2 * ff
    if final:
        x = _rms(x) * gfin_ref[...]
    o_ref[...] = x


def _ffn_call(x, attn, ssm, conv, mod_l, wout, gffn, wg, wu, wd, gfin, *, t, tmod, mod_row0, final):
    n = x.shape[0]
    tm = TM_FFN

    def tok(i):
        return (i, 0)

    def const(i):
        return (0, 0)

    def resident(shape):
        return pl.BlockSpec(shape, const, pipeline_mode=pl.Buffered(1))

    return pl.pallas_call(
        functools.partial(_ffn_kernel, final=final),
        out_shape=jax.ShapeDtypeStruct((n, D_MODEL), F32),
        grid=(n // tm,),
        in_specs=[
            pl.BlockSpec((tm, D_MODEL), tok),
            pl.BlockSpec((tm, MLA_HEADS * V_HEAD), tok),
            pl.BlockSpec((tm, SSD_INNER), tok),
            pl.BlockSpec((tm, CM_CH), tok),
            _mod_spec(tm, tmod, mod_row0),
            resident((D_MODEL, D_MODEL)),
            pl.BlockSpec((1, D_MODEL), const),
            resident((D_MODEL, D_FF)),
            resident((D_MODEL, D_FF)),
            resident((D_FF, D_MODEL)),
            pl.BlockSpec((1, D_MODEL), const),
        ],
        out_specs=pl.BlockSpec((tm, D_MODEL), tok),
        compiler_params=_params(("parallel",)),
        name="out_ffn",
    )(x, attn, ssm, conv, mod_l, wout, gffn, wg, wu, wd, gfin)


def _rope_tables():
    rows = DEC_SEQ // GRID_W
    row = jnp.repeat(jnp.arange(rows), GRID_W).astype(F32)
    col = (jnp.arange(rows * GRID_W) % GRID_W).astype(F32)
    nf = QK_ROPE // 4
    inv = ROPE_THETA ** (-jnp.arange(nf, dtype=F32) / nf)
    ang = jnp.stack([row[:, None] * inv, col[:, None] * inv], axis=1)
    cos = jnp.cos(ang)
    sin = jnp.sin(ang)
    cos32 = jnp.stack([cos, cos], axis=2).reshape(DEC_SEQ, QK_ROPE)
    sin32 = jnp.stack([-sin, sin], axis=2).reshape(DEC_SEQ, QK_ROPE)
    ones = jnp.ones((DEC_SEQ, QK_NOPE), F32)
    pad1 = jnp.ones((DEC_SEQ, HEAD_PAD - QK_HEAD), F32)
    cos_t = jnp.concatenate([ones, cos32, pad1], axis=1)
    sin_t = jnp.concatenate([0.0 * ones, sin32, 0.0 * pad1], axis=1)
    return cos_t, sin_t


def _kr_placement():
    src = jnp.arange(QK_ROPE)
    cols = jnp.arange(MLA_HEADS * HEAD_PAD)
    hit = (cols[None, :] % HEAD_PAD) == (QK_NOPE + src[:, None])
    return hit.astype(BF16)


def _pad_states(h0):
    b = h0.shape[0]
    s = h0.reshape(b, 2, SSD_GROUPS, 2, SSD_HEAD_DIM, SSD_STATE)
    s = s.transpose(0, 1, 2, 5, 3, 4).reshape(b, 2, SSD_GROUPS, SSD_STATE, LANE)
    zero = jnp.zeros_like(s[:, :, 0])
    g0 = jnp.concatenate([s[:, :, 0], zero], axis=-2)
    g1 = jnp.concatenate([zero, s[:, :, 1]], axis=-2)
    return jnp.stack([g0, g1], axis=2)


def _unpack_states(st):
    b = st.shape[0]
    s = st.reshape(b, 2, SSD_GROUPS, SSD_STATE, 2, SSD_HEAD_DIM).transpose(0, 1, 2, 4, 5, 3)
    return s.reshape(b, 2, SSD_HEADS, SSD_HEAD_DIM, SSD_STATE)


def kernel(x_prompt, x_sample, c, cache_ckv, cache_krope, state_ssd, c_ctx, w_ada, b_ada, g_mix, w_in, g_q, w_uq, g_kv, w_ukv, ssd_conv_w, ssd_conv_b, ssd_dt_bias, ssd_a_log, ssd_d, ssd_norm_g, cm_conv_w, cm_conv_b, cm_ln_g, cm_ln_b, w_out, g_ffn, w_gate, w_up, w_down, g_final):
    L = DEPTH
    win = _place_call(w_in, _src_in(), "place_w_in")
    wuq = _place_call(w_uq, _src_uq(), "place_w_uq")
    wukv = _place_call(w_ukv, _src_ukv(), "place_w_ukv")
    wout = w_out.astype(BF16)
    wg = w_gate.astype(BF16)
    wu = w_up.astype(BF16)
    wd = w_down.astype(BF16)
    rope_tabs = _rope_tables()

    cond8 = jnp.concatenate([c_ctx[None, :], c, jnp.zeros((8 - 1 - DEC_BATCH, D_MODEL), F32)], axis=0)
    mod = _ada_call(cond8, w_ada, b_ada)
    mod = mod.reshape(L, 8, 1, N_MOD * D_MODEL)

    kc_all, vc_all = _ctxkv_call(cache_ckv, cache_krope, wukv, _kr_placement())

    ssd_cw = jnp.concatenate([ssd_conv_w, jnp.zeros((L, 8 - SSD_CONV, SSD_XBC), F32)], axis=1)
    cm_w = jnp.concatenate([cm_conv_w, jnp.zeros((L, 32 - CM_WIDTH, CM_CH), F32)], axis=1)
    lane_pad = lambda v: jnp.concatenate(
        [v.reshape(L, 1, 2 * SSD_HEADS), jnp.zeros((L, 1, LANE - 2 * SSD_HEADS), F32)], axis=-1)
    dtb = lane_pad(ssd_dt_bias)
    alog = lane_pad(ssd_a_log)
    d_rep = jnp.repeat(ssd_d, SSD_HEAD_DIM, axis=-1)
    h0_lat = _pad_states(state_ssd.transpose(1, 0, 2, 3, 4, 5).reshape(L * DEC_BATCH, 2, SSD_HEADS,
                                                                     SSD_HEAD_DIM, SSD_STATE))
    h0_lat = h0_lat.reshape(L, DEC_BATCH, 2, SSD_GROUPS, LANE, LANE)

    streams = [
        dict(x=x_prompt.reshape(N_CTX, D_MODEL), nb=BATCH, t=SEQ, tmod=N_CTX, mod_row0=0, tq=SEQ, ngroups=2,
             rope=None),
        dict(x=x_sample.reshape(N_LAT, D_MODEL), nb=DEC_BATCH, t=DEC_SEQ, tmod=DEC_SEQ, mod_row0=1, tq=TQ_LAT,
             ngroups=1, rope=rope_tabs),
    ]
    row = lambda v: v.reshape(1, -1)
    ckvs, krs, hss = [], [], []
    for l in range(L):
        for si, s in enumerate(streams):
            lat = si == 1
            nb, t = s["nb"], s["t"]
            q, k, v, ckv, kr, z, xbc, dt, cm = _inproj_call(
                s["x"], mod[l], row(g_mix[l]), win[l], row(g_q[l]), wuq[l],
                row(g_kv[l]), wukv[l], s["rope"], t=t, tmod=s["tmod"], mod_row0=s["mod_row0"])
            attn = _attn_call(q, k, v, kc_all[l] if lat else None, vc_all[l] if lat else None,
                              nb=nb, t=t, tq=s["tq"], ngroups=s["ngroups"])
            ssm, hf = _ssd_call(z, xbc, dt, ssd_cw[l], row(ssd_conv_b[l]), dtb[l], alog[l],
                                row(d_rep[l, 0]), row(d_rep[l, 1]), row(ssd_norm_g[l]),
                                h0_lat[l] if lat else None, nb=nb, t=t)
            conv = _cm_call(cm, cm_w[l], row(cm_conv_b[l]), row(cm_ln_g[l]), row(cm_ln_b[l]), nb=nb, t=t)
            s["x"] = _ffn_call(s["x"], attn, ssm, conv, mod[l], wout[l], row(g_ffn[l]), wg[l], wu[l], wd[l],
                               row(g_final), t=t, tmod=s["tmod"], mod_row0=s["mod_row0"], final=(l == L - 1))
            if not lat:
                ckvs.append(ckv.reshape(BATCH, SEQ, KV_RANK))
                krs.append(kr[:, QK_NOPE:QK_HEAD].reshape(BATCH, SEQ, QK_ROPE))
                hss.append(_unpack_states(hf))

    y_prompt = streams[0]["x"].reshape(BATCH, SEQ, D_MODEL)
    y_sample = streams[1]["x"].reshape(DEC_BATCH, DEC_SEQ, D_MODEL)
    return (y_prompt, y_sample, jnp.stack(ckvs, axis=1), jnp.stack(krs, axis=1), jnp.stack(hss, axis=1))
```

```python
import functools

import numpy as np

import jax
import jax.numpy as jnp
from jax import lax
from jax.experimental import pallas as pl
from jax.experimental.pallas import tpu as pltpu

F32 = jnp.float32
BF16 = jnp.bfloat16

D_MODEL = 1024
BATCH = 16
SEQ = 256
DEPTH = 4
DEC_BATCH = 4
DEC_SEQ = 2048
PAST_LEN = 256
GRID_W = 64

MLA_HEADS = 8
QK_NOPE = 64
QK_ROPE = 32
QK_HEAD = QK_NOPE + QK_ROPE
V_HEAD = 64
Q_RANK = 256
KV_RANK = 128
ROPE_THETA = 10000.0

SSD_HEADS = 4
SSD_HEAD_DIM = 64
SSD_INNER = SSD_HEADS * SSD_HEAD_DIM
SSD_GROUPS = 2
SSD_STATE = 64
SSD_CONV = 5
SSD_CHUNK = 128
SSD_XBC = SSD_INNER + 2 * SSD_GROUPS * SSD_STATE

CM_CH = 256
CM_WIDTH = 31

D_FF = 2816
N_MOD = 6
IN_WIDTH = (Q_RANK + KV_RANK + QK_ROPE) + (SSD_INNER + SSD_XBC + 2 * SSD_HEADS) + 2 * CM_CH

N_CTX = BATCH * SEQ
N_LAT = DEC_BATCH * DEC_SEQ

SUBLANE = 8
LANE = 128
MXU_N = 256
HEAD_PAD = 128
HEADS_PER_STEP = MXU_N // V_HEAD
VMEM_LIMIT = 48 * 1024 * 1024

TM_IN = 512
TM_FFN = 512
TQ_LAT = 512
FF_CHUNK = 256
CONV_ROWS = 128
CM_ROWS = 64
SSD_CONV_ROWS = 32

C_QLAT = 0
C_CKV = C_QLAT + Q_RANK
C_KR = C_CKV + KV_RANK
C_Z = C_KR + 2 * HEAD_PAD
C_XBC = C_Z + SSD_INNER
C_DT = C_XBC + SSD_XBC
C_CM = C_DT + LANE
IN_ARR = C_CM + 2 * CM_CH


def _params(sem):
    return pltpu.CompilerParams(dimension_semantics=sem, vmem_limit_bytes=VMEM_LIMIT)


def _dot(a, b):
    return jnp.dot(a, b, preferred_element_type=F32)


def _dot_nt(a, b):
    return lax.dot_general(a, b, (((1,), (1,)), ((), ())), preferred_element_type=F32)


def _rms(x, eps=1e-6):
    return x * lax.rsqrt(jnp.mean(x * x, axis=-1, keepdims=True) + eps)


def _silu(x):
    return x * jax.nn.sigmoid(x)


def _layer_spec(shape, l, single_buffer=False):
    nd = len(shape)
    kw = dict(pipeline_mode=pl.Buffered(1)) if single_buffer else {}
    return pl.BlockSpec((None,) + tuple(shape), lambda *_: (l,) + (0,) * nd, **kw)


def _vec(v):
    return v.reshape(DEPTH, 1, -1)


def _place_kernel(w_ref, p_ref, o_ref):
    o_ref[0] = _dot(w_ref[0].astype(BF16), p_ref[...]).astype(BF16)


def _place_call(w, src, name):
    L, rows, cin = w.shape
    cout = src.shape[0]
    place = (jnp.arange(cin, dtype=jnp.int32)[:, None] == jnp.asarray(src, jnp.int32)[None, :]).astype(BF16)
    tr = min(rows, 256)
    return pl.pallas_call(
        _place_kernel,
        out_shape=jax.ShapeDtypeStruct((L, rows, cout), BF16),
        grid=(L, rows // tr),
        in_specs=[
            pl.BlockSpec((1, tr, cin), lambda l, i: (l, i, 0)),
            pl.BlockSpec((cin, cout), lambda l, i: (0, 0)),
        ],
        out_specs=pl.BlockSpec((1, tr, cout), lambda l, i: (l, i, 0)),
        compiler_params=_params(("parallel", "parallel")),
        name=name,
    )(w, place)


def _swap32():
    return np.arange(QK_ROPE).reshape(2, 2, QK_ROPE // 4)[:, ::-1, :].reshape(-1)


def _src_in():
    kr0 = Q_RANK + KV_RANK
    off_ssd = kr0 + QK_ROPE
    off_dt = off_ssd + SSD_INNER + SSD_XBC
    off_cm = off_dt + 2 * SSD_HEADS
    src = -np.ones((IN_ARR,), np.int64)
    src[0:C_KR] = np.arange(0, kr0)
    src[C_KR + QK_NOPE:C_KR + QK_HEAD] = kr0 + np.arange(QK_ROPE)
    src[C_KR + HEAD_PAD + QK_NOPE:C_KR + HEAD_PAD + QK_HEAD] = kr0 + _swap32()
    src[C_Z:C_DT] = np.arange(off_ssd, off_dt)
    src[C_DT:C_DT + 2 * SSD_HEADS] = np.arange(off_dt, off_cm)
    src[C_CM:IN_ARR] = np.arange(off_cm, IN_WIDTH)
    return src


def _src_uq():
    half = MLA_HEADS * HEAD_PAD
    src = -np.ones((2 * half,), np.int64)
    for h in range(MLA_HEADS):
        src[h * HEAD_PAD:h * HEAD_PAD + QK_HEAD] = h * QK_HEAD + np.arange(QK_HEAD)
        lo = half + h * HEAD_PAD + QK_NOPE
        src[lo:lo + QK_ROPE] = h * QK_HEAD + QK_NOPE + _swap32()
    return src


def _src_ukv():
    half = MLA_HEADS * HEAD_PAD
    src = -np.ones((half + MLA_HEADS * V_HEAD,), np.int64)
    for h in range(MLA_HEADS):
        base = h * (QK_NOPE + V_HEAD)
        src[h * HEAD_PAD:h * HEAD_PAD + QK_NOPE] = base + np.arange(QK_NOPE)
        src[half + h * V_HEAD:half + (h + 1) * V_HEAD] = base + QK_NOPE + np.arange(V_HEAD)
    return src


def _ada_kernel(cond_ref, w_ref, b_ref, o_ref):
    c = cond_ref[...]
    a = _silu(c).astype(BF16)
    o_ref[0] = _dot(a, w_ref[0].astype(BF16)) + b_ref[0]


def _ada_call(cond8, w_ada, b_ada):
    tn = 1536
    nt = (N_MOD * D_MODEL) // tn
    return pl.pallas_call(
        _ada_kernel,
        out_shape=jax.ShapeDtypeStruct((DEPTH, 8, N_MOD * D_MODEL), F32),
        grid=(DEPTH, nt),
        in_specs=[
            pl.BlockSpec((8, D_MODEL), lambda l, j: (0, 0)),
            pl.BlockSpec((1, D_MODEL, tn), lambda l, j: (l, 0, j)),
            pl.BlockSpec((1, 1, tn), lambda l, j: (l, 0, j)),
        ],
        out_specs=pl.BlockSpec((1, 8, tn), lambda l, j: (l, 0, j)),
        compiler_params=_params(("parallel", "parallel")),
        name="ada_mod",
    )(cond8, w_ada, b_ada.reshape(DEPTH, 1, N_MOD * D_MODEL))


def _mod_spec(l, tm, tmod, mod_row0):
    return pl.BlockSpec((None, 1, 1, N_MOD * D_MODEL), lambda i: (l, mod_row0 + (i * tm) // tmod, 0, 0))


def _inproj_kernel(*refs, rope):
    if rope:
        (x_ref, mod_ref, gmix_ref, win_ref, gq_ref, wuq_ref, gkv_ref, wukv_ref, cos_ref, sin_ref,
         q_out, k_out, v_out, ckv_out, kr_out, z_out, xbc_out, dt_out, cm_out) = refs
        cos = cos_ref[...]
        sin = sin_ref[...]
    else:
        (x_ref, mod_ref, gmix_ref, win_ref, gq_ref, wuq_ref, gkv_ref, wukv_ref,
         q_out, k_out, v_out, ckv_out, kr_out, z_out, xbc_out, dt_out, cm_out) = refs
    x = x_ref[...]
    sh1 = mod_ref[0, :, 0:D_MODEL]
    sc1 = mod_ref[0, :, D_MODEL:2 * D_MODEL]
    h = (_rms(x) * gmix_ref[...] * (1.0 + sc1) + sh1).astype(BF16)

    q_lat = _dot(h, win_ref[:, C_QLAT:C_QLAT + Q_RANK])
    qn = (_rms(q_lat) * gq_ref[...]).astype(BF16)
    qq = _dot(qn, wuq_ref[...])
    scale = QK_HEAD ** -0.5
    half = MLA_HEADS * HEAD_PAD
    for hd in range(MLA_HEADS):
        lo = hd * HEAD_PAD
        qh = qq[:, lo:lo + HEAD_PAD]
        if rope:
            qh = qh * cos + qq[:, half + lo:half + lo + HEAD_PAD] * sin
        q_out[:, lo:lo + HEAD_PAD] = (qh * scale).astype(BF16)

    ckv = _rms(_dot(h, win_ref[:, C_CKV:C_CKV + KV_RANK])) * gkv_ref[...]
    ckv_out[...] = ckv
    kv = _dot(ckv.astype(BF16), wukv_ref[...])
    if rope:
        kr2 = _dot(h, win_ref[:, C_KR:C_KR + 2 * HEAD_PAD])
        kr = kr2[:, 0:HEAD_PAD]
        kr_rot = kr * cos + kr2[:, HEAD_PAD:2 * HEAD_PAD] * sin
    else:
        kr = _dot(h, win_ref[:, C_KR:C_KR + HEAD_PAD])
        kr_rot = kr
    kr_out[...] = kr
    for hd in range(MLA_HEADS):
        lo = hd * HEAD_PAD
        k_out[:, lo:lo + HEAD_PAD] = (kv[:, lo:lo + HEAD_PAD] + kr_rot).astype(BF16)
    v_out[...] = kv[:, half:half + MLA_HEADS * V_HEAD].astype(BF16)

    z_out[...] = _dot(h, win_ref[:, C_Z:C_Z + SSD_INNER])
    xbc_out[...] = _dot(h, win_ref[:, C_XBC:C_XBC + SSD_XBC])
    dt_out[...] = _dot(h, win_ref[:, C_DT:C_DT + LANE])
    cm_out[...] = _dot(h, win_ref[:, C_CM:C_CM + 2 * CM_CH])


def _inproj_call(l, x, mod, gmix, win, gq, wuq, gkv, wukv, rope_tabs, *, t, tmod, mod_row0):
    n = x.shape[0]
    tm = TM_IN
    rope = rope_tabs is not None
    half = MLA_HEADS * HEAD_PAD

    def tok(i):
        return (i, 0)

    outs = [
        ((n, half), BF16),
        ((n, half), BF16),
        ((n, MLA_HEADS * V_HEAD), BF16),
        ((n, KV_RANK), F32),
        ((n, HEAD_PAD), F32),
        ((n, SSD_INNER), F32),
        ((n, SSD_XBC), F32),
        ((n, LANE), F32),
        ((n, 2 * CM_CH), F32),
    ]
    in_specs = [
        pl.BlockSpec((tm, D_MODEL), tok),
        _mod_spec(l, tm, tmod, mod_row0),
        _layer_spec((1, D_MODEL), l),
        _layer_spec((D_MODEL, IN_ARR), l, single_buffer=True),
        _layer_spec((1, Q_RANK), l),
        _layer_spec((Q_RANK, 2 * half if rope else half), l, single_buffer=True),
        _layer_spec((1, KV_RANK), l),
        _layer_spec((KV_RANK, MLA_HEADS * (HEAD_PAD + V_HEAD)), l, single_buffer=True),
    ]
    args = [x, mod, gmix, win, gq, wuq, gkv, wukv]
    if rope:
        pos = lambda i: (i % (t // tm), 0)
        in_specs += [pl.BlockSpec((tm, HEAD_PAD), pos), pl.BlockSpec((tm, HEAD_PAD), pos)]
        args += list(rope_tabs)
    return pl.pallas_call(
        functools.partial(_inproj_kernel, rope=rope),
        out_shape=[jax.ShapeDtypeStruct(s, d) for s, d in outs],
        grid=(n // tm,),
        in_specs=in_specs,
        out_specs=[pl.BlockSpec((tm, s[1]), tok) for s, _ in outs],
        compiler_params=_params(("parallel",)),
        name="in_proj_lat" if rope else "in_proj_ctx",
    )(*args)


def _ctxkv_kernel(ckv_ref, kr_ref, wukv_ref, place_ref, k_out, v_out):
    kv = _dot(ckv_ref[0, 0].astype(BF16), wukv_ref[0])
    krp = _dot(kr_ref[0, 0].astype(BF16), place_ref[...])
    half = MLA_HEADS * HEAD_PAD
    k_out[0, 0] = (kv[:, 0:half] + krp).astype(BF16)
    v_out[0, 0] = kv[:, half:half + MLA_HEADS * V_HEAD].astype(BF16)


def _ctxkv_call(cache_ckv, cache_krope, wukv_all, place):
    return pl.pallas_call(
        _ctxkv_kernel,
        out_shape=[
            jax.ShapeDtypeStruct((DEPTH, DEC_BATCH, PAST_LEN, MLA_HEADS * HEAD_PAD), BF16),
            jax.ShapeDtypeStruct((DEPTH, DEC_BATCH, PAST_LEN, MLA_HEADS * V_HEAD), BF16),
        ],
        grid=(DEPTH, DEC_BATCH),
        in_specs=[
            pl.BlockSpec((1, 1, PAST_LEN, KV_RANK), lambda l, b: (b, l, 0, 0)),
            pl.BlockSpec((1, 1, PAST_LEN, QK_ROPE), lambda l, b: (b, l, 0, 0)),
            pl.BlockSpec((1, KV_RANK, MLA_HEADS * (HEAD_PAD + V_HEAD)), lambda l, b: (l, 0, 0)),
            pl.BlockSpec((QK_ROPE, MLA_HEADS * HEAD_PAD), lambda l, b: (0, 0)),
        ],
        out_specs=[
            pl.BlockSpec((1, 1, PAST_LEN, MLA_HEADS * HEAD_PAD), lambda l, b: (l, b, 0, 0)),
            pl.BlockSpec((1, 1, PAST_LEN, MLA_HEADS * V_HEAD), lambda l, b: (l, b, 0, 0)),
        ],
        compiler_params=_params(("parallel", "parallel")),
        name="ctx_kv",
    )(cache_ckv, cache_krope, wukv_all, place)


def _attn_kernel(*refs, has_ctx, ngroups):
    if has_ctx:
        q_ref, k_ref, v_ref, kc_ref, vc_ref, o_ref = refs
    else:
        q_ref, k_ref, v_ref, o_ref = refs
    gw = HEADS_PER_STEP * V_HEAD
    for grp in range(ngroups):
        vg = v_ref[:, grp * gw:(grp + 1) * gw]
        if has_ctx:
            vcg = vc_ref[0, :, grp * gw:(grp + 1) * gw]
        o_sel = None
        den_sel = None
        for hh in range(HEADS_PER_STEP):
            lo = (grp * HEADS_PER_STEP + hh) * HEAD_PAD
            qh = q_ref[:, lo:lo + HEAD_PAD]
            s = _dot_nt(qh, k_ref[:, lo:lo + HEAD_PAD])
            m = jnp.max(s, axis=-1, keepdims=True)
            if has_ctx:
                sc = _dot_nt(qh, kc_ref[0, :, lo:lo + HEAD_PAD])
                m = jnp.maximum(m, jnp.max(sc, axis=-1, keepdims=True))
            p = jnp.exp(s - m)
            den = jnp.sum(p, axis=-1, keepdims=True)
            o = _dot(p.astype(BF16), vg)
            if has_ctx:
                pc = jnp.exp(sc - m)
                den = den + jnp.sum(pc, axis=-1, keepdims=True)
                o = o + _dot(pc.astype(BF16), vcg)
            if hh == 0:
                o_sel = o
                den_sel = jnp.broadcast_to(den, o.shape)
            else:
                mine = lax.broadcasted_iota(jnp.int32, o.shape, 1) >= hh * V_HEAD
                o_sel = jnp.where(mine, o, o_sel)
                den_sel = jnp.where(mine, den, den_sel)
        o_ref[:, grp * gw:(grp + 1) * gw] = (o_sel / den_sel).astype(BF16)


def _attn_call(l, q, k, v, kc, vc, *, nb, t, tq, ngroups):
    has_ctx = kc is not None
    nq = t // tq
    nh = ngroups * HEADS_PER_STEP
    nsteps = MLA_HEADS // nh
    in_specs = [
        pl.BlockSpec((tq, nh * HEAD_PAD), lambda b, g, qi: (b * nq + qi, g)),
        pl.BlockSpec((t, nh * HEAD_PAD), lambda b, g, qi: (b, g)),
        pl.BlockSpec((t, nh * V_HEAD), lambda b, g, qi: (b, g)),
    ]
    args = [q, k, v]
    if has_ctx:
        in_specs += [
            pl.BlockSpec((None, 1, PAST_LEN, nh * HEAD_PAD), lambda b, g, qi: (l, b, 0, g)),
            pl.BlockSpec((None, 1, PAST_LEN, nh * V_HEAD), lambda b, g, qi: (l, b, 0, g)),
        ]
        args += [kc, vc]
    return pl.pallas_call(
        functools.partial(_attn_kernel, has_ctx=has_ctx, ngroups=ngroups),
        out_shape=jax.ShapeDtypeStruct((nb * t, MLA_HEADS * V_HEAD), BF16),
        grid=(nb, nsteps, nq),
        in_specs=in_specs,
        out_specs=pl.BlockSpec((tq, nh * V_HEAD), lambda b, g, qi: (b * nq + qi, g)),
        compiler_params=_params(("parallel", "parallel", "arbitrary")),
        name="attn_lat" if has_ctx else "attn_ctx",
    )(*args)


def _split3(x):
    hi = x.astype(BF16)
    r = x - hi.astype(F32)
    mid = r.astype(BF16)
    lo = (r - mid.astype(F32)).astype(BF16)
    return hi, mid, lo


def _ssd_kernel(*refs, t, has_h0):
    if has_h0:
        (z_ref, xbc_ref, dt_ref, cw_ref, cb_ref, dtb_ref, alog_ref, d0_ref, d1_ref, gn_ref, h0_ref,
         y_out, hf_out, xpad, xc, dtp_s, acs_s, acst_s, ysc, ssc, dec_s, eacs_s, hin_s, st) = refs
    else:
        (z_ref, xbc_ref, dt_ref, cw_ref, cb_ref, dtb_ref, alog_ref, d0_ref, d1_ref, gn_ref,
         y_out, hf_out, xpad, xc, dtp_s, acs_s, acst_s, ysc, ssc, dec_s, eacs_s, hin_s, st) = refs
    L = SSD_CHUNK
    W = SSD_INNER
    nc = t // L
    halo = SUBLANE

    xpad[0:halo, :] = jnp.zeros((halo, SSD_XBC), F32)
    xpad[halo + t:halo + t + halo, :] = jnp.zeros((halo, SSD_XBC), F32)
    xpad[halo:halo + t, :] = xbc_ref[...]
    cw = cw_ref[...]
    cb = cb_ref[...]
    rc = SSD_CONV_ROWS
    nwin = rc + 2 * halo

    def conv(blk, carry):
        base = pl.multiple_of(blk * rc, rc)
        win = xpad[pl.ds(base, nwin), :]
        acc = cb + cw[SSD_CONV // 2:SSD_CONV // 2 + 1, :] * win[halo:halo + rc, :]
        for kk in range(SSD_CONV):
            if kk != SSD_CONV // 2:
                first = halo - SSD_CONV // 2 + kk
                acc = acc + cw[kk:kk + 1, :] * pltpu.roll(win, nwin - first, axis=0)[0:rc, :]
        xc[pl.ds(base, rc), :] = _silu(acc)
        return carry

    lax.fori_loop(0, t // rc, conv, 0, unroll=2)
    r = CONV_ROWS

    a_all = -jnp.exp(alog_ref[...])
    dtb = dtb_ref[...]
    row_i = lax.broadcasted_iota(jnp.int32, (L, L), 0)
    col_i = lax.broadcasted_iota(jnp.int32, (L, L), 1)
    tri = [(row_i >= col_i), (row_i <= col_i)]
    tri_bf = [jnp.where(m, 1.0, 0.0).astype(BF16) for m in tri]
    grp_rows = [row_i < SSD_STATE, row_i >= SSD_STATE]
    lane_w = lax.broadcasted_iota(jnp.int32, (L, W), 1)
    own_block = (lax.broadcasted_iota(jnp.int32, (L, W), 0) // SSD_STATE) == (lane_w // LANE)

    def chunk_rows(c):
        return pl.ds(pl.multiple_of(c * L, L), L)

    lane_lo = col_i < SSD_HEAD_DIM

    def columns(m, d):
        return [jnp.broadcast_to(m[:, j:j + 1], (L, LANE)) for j in range(d * SSD_HEADS, (d + 1) * SSD_HEADS)]

    def expand(cols):
        return jnp.concatenate([jnp.where(lane_lo, cols[0], cols[1]), jnp.where(lane_lo, cols[2], cols[3])],
                               axis=1)

    def pre(c, carry):
        rows = chunk_rows(c)
        dtp = jax.nn.softplus(dt_ref[rows, :] + dtb)
        dtp_s[rows, :] = dtp
        x3 = jnp.concatenate(_split3(dtp * a_all), axis=1)
        rf = _dot(tri_bf[0], x3)
        rb = _dot(tri_bf[1], x3)
        acs_f = rf[:, 0:LANE] + rf[:, LANE:2 * LANE] + rf[:, 2 * LANE:3 * LANE]
        acs_b = rb[:, 0:LANE] + rb[:, LANE:2 * LANE] + rb[:, 2 * LANE:3 * LANE]
        acs = jnp.where(col_i < SSD_HEADS, acs_f, acs_b)
        acs_s[rows, :] = acs
        acst_s[pl.ds(pl.multiple_of(c * SUBLANE, SUBLANE), SUBLANE), :] = acs.T[0:SUBLANE, :]
        return carry

    lax.fori_loop(0, nc, pre, 0, unroll=2)

    def main(c, carry):
        rows = chunk_rows(c)
        xs = xc[rows, 0:W]
        cm = xc[rows, W + LANE:W + 2 * LANE].astype(BF16)
        bt = xc[rows, W:W + LANE].T
        btm2 = jnp.concatenate([jnp.where(grp_rows[0], bt, 0.0), jnp.where(grp_rows[1], bt, 0.0)],
                               axis=1).astype(BF16)
        g2 = _dot(cm, btm2)
        bt_bf = bt.astype(BF16)
        dtp = dtp_s[rows, :]
        acs = acs_s[rows, :]
        acst = acst_s[pl.ds(pl.multiple_of(c * SUBLANE, SUBLANE), SUBLANE), :]
        ysum = None
        for d in range(2):
            acs_cols = columns(acs, d)
            acs_b = expand(acs_cols)
            eacs_s[c, d] = jnp.exp(acs_b)
            xdt = xs * expand(columns(dtp, d))
            xdt_bf = xdt.astype(BF16)
            yd = None
            for h in range(SSD_HEADS):
                j = d * SSD_HEADS + h
                g = h // (SSD_HEADS // SSD_GROUPS)
                diff = acs_cols[h] - acst[j:j + 1, :]
                seg = jnp.exp(jnp.where(tri[d], diff, -jnp.inf))
                res = _dot((g2[:, g * L:(g + 1) * L] * seg).astype(BF16), xdt_bf)
                yd = res if h == 0 else jnp.where(lane_w >= h * SSD_HEAD_DIM, res, yd)
            ysum = yd if d == 0 else ysum + yd
            tot = acs_b[L - 1:L, :] if d == 0 else acs_b[0:1, :]
            s_new = _dot(bt_bf, (xdt * jnp.exp(tot - acs_b)).astype(BF16))
            ssc[c, d] = jnp.where(own_block, s_new, 0.0)
            dec_s[c, d] = jnp.broadcast_to(jnp.exp(tot), (SUBLANE, W))
        ysc[rows, :] = ysum
        return carry

    lax.fori_loop(0, nc, main, 0, unroll=2)

    if has_h0:
        st[...] = h0_ref[0]
    else:
        st[...] = jnp.zeros(st.shape, F32)

    def rec(i, carry):
        for d, c in ((0, i), (1, nc - 1 - i)):
            h_in = st[d]
            hin_s[c, d] = h_in.astype(BF16)
            st[d] = dec_s[c, d][0:1, :] * h_in + ssc[c, d]
        return carry

    lax.fori_loop(0, nc, rec, 0)
    for d in range(2):
        for g in range(SSD_GROUPS):
            hf_out[0, d, g] = st[d, g * SSD_STATE:(g + 1) * SSD_STATE, g * LANE:(g + 1) * LANE]

    d0 = d0_ref[...]
    d1 = d1_ref[...]
    gn = gn_ref[...]

    def fin(c, carry):
        rows = chunk_rows(c)
        xs = xc[rows, 0:W]
        cm = xc[rows, W + LANE:W + 2 * LANE].astype(BF16)
        y = ysc[rows, :] + d0 * xs + d1 * xs
        for d in range(2):
            y = y + _dot(cm, hin_s[c, d]) * eacs_s[c, d]
        y = y * _silu(z_ref[rows, :])
        y_out[rows, :] = (_rms(y) * gn).astype(BF16)
        return carry

    lax.fori_loop(0, nc, fin, 0, unroll=2)


def _ssd_call(l, z, xbc, dt, cw, cb, dtb, alog, d0, d1, gn, h0, *, nb, t):
    has_h0 = h0 is not None
    nc = t // SSD_CHUNK

    def tok(b):
        return (b, 0)

    in_specs = [
        pl.BlockSpec((t, SSD_INNER), tok),
        pl.BlockSpec((t, SSD_XBC), tok),
        pl.BlockSpec((t, LANE), tok),
        _layer_spec((SUBLANE, SSD_XBC), l),
        _layer_spec((1, SSD_XBC), l),
        _layer_spec((1, LANE), l),
        _layer_spec((1, LANE), l),
        _layer_spec((1, SSD_INNER), l),
        _layer_spec((1, SSD_INNER), l),
        _layer_spec((1, SSD_INNER), l),
    ]
    args = [z, xbc, dt, cw, cb, dtb, alog, d0, d1, gn]
    if has_h0:
        in_specs.append(pl.BlockSpec((None, 1, 2, LANE, SSD_INNER), lambda b: (l, b, 0, 0, 0)))
        args.append(h0)
    return pl.pallas_call(
        functools.partial(_ssd_kernel, t=t, has_h0=has_h0),
        out_shape=[
            jax.ShapeDtypeStruct((nb * t, SSD_INNER), BF16),
            jax.ShapeDtypeStruct((nb, 2, SSD_GROUPS, SSD_STATE, LANE), F32),
        ],
        grid=(nb,),
        in_specs=in_specs,
        out_specs=[
            pl.BlockSpec((t, SSD_INNER), tok),
            pl.BlockSpec((1, 2, SSD_GROUPS, SSD_STATE, LANE), lambda b: (b, 0, 0, 0, 0)),
        ],
        scratch_shapes=[
            pltpu.VMEM((t + 2 * SUBLANE, SSD_XBC), F32),
            pltpu.VMEM((t, SSD_XBC), F32),
            pltpu.VMEM((t, LANE), F32),
            pltpu.VMEM((t, LANE), F32),
            pltpu.VMEM((nc * SUBLANE, LANE), F32),
            pltpu.VMEM((t, SSD_INNER), F32),
            pltpu.VMEM((nc, 2, LANE, SSD_INNER), F32),
            pltpu.VMEM((nc, 2, SUBLANE, SSD_INNER), F32),
            pltpu.VMEM((nc, 2, LANE, SSD_INNER), F32),
            pltpu.VMEM((nc, 2, LANE, SSD_INNER), BF16),
            pltpu.VMEM((2, LANE, SSD_INNER), F32),
        ],
        compiler_params=_params(("parallel",)),
        name="ssd_lat" if has_h0 else "ssd_ctx",
    )(*args)


def _cm_kernel(cm_ref, w_ref, b_ref, lg_ref, lb_ref, o_ref, gpad, *, t):
    halo = 16
    gpad[0:halo, :] = jnp.zeros((halo, CM_CH), F32)
    gpad[halo + t:halo + t + halo, :] = jnp.zeros((halo, CM_CH), F32)
    r = CONV_ROWS
    for blk in range(t // r):
        rows = slice(blk * r, (blk + 1) * r)
        gpad[halo + blk * r:halo + (blk + 1) * r, :] = (
            cm_ref[rows, 0:CM_CH] * jax.nn.sigmoid(cm_ref[rows, CM_CH:2 * CM_CH]))
    w = w_ref[...]
    bias = b_ref[...]
    lg = lg_ref[...]
    lb = lb_ref[...]
    rb = CM_ROWS
    sub = SUBLANE
    assert halo - CM_WIDTH // 2 == 1 and CM_WIDTH <= 4 * sub

    def body(blk, carry):
        base = pl.multiple_of(blk * rb, rb)
        acc = bias
        for j in range(sub):
            pj = None
            for a in range(4):
                kk = sub * a + j
                if kk < CM_WIDTH:
                    term = w[kk:kk + 1, :] * gpad[pl.ds(base + sub * a, rb + sub), :]
                    pj = term if pj is None else pj + term
            acc = acc + pj[j + 1:j + 1 + rb, :]
        mu = jnp.mean(acc, axis=-1, keepdims=True)
        xc = acc - mu
        var = jnp.mean(xc * xc, axis=-1, keepdims=True)
        y = xc * lax.rsqrt(var + 1e-5) * lg + lb
        o_ref[pl.ds(base, rb), :] = _silu(y).astype(BF16)
        return carry

    lax.fori_loop(0, t // rb, body, 0)


def _cm_call(l, cm, w, b, lg, lb, *, nb, t):
    return pl.pallas_call(
        functools.partial(_cm_kernel, t=t),
        out_shape=jax.ShapeDtypeStruct((nb * t, CM_CH), BF16),
        grid=(nb,),
        in_specs=[
            pl.BlockSpec((t, 2 * CM_CH), lambda i: (i, 0)),
            _layer_spec((4 * SUBLANE, CM_CH), l),
            _layer_spec((1, CM_CH), l),
            _layer_spec((1, CM_CH), l),
            _layer_spec((1, CM_CH), l),
        ],
        out_specs=pl.BlockSpec((t, CM_CH), lambda i: (i, 0)),
        scratch_shapes=[pltpu.VMEM((t + 32, CM_CH), F32)],
        compiler_params=_params(("parallel",)),
        name="conv_module",
    )(cm, w, b, lg, lb)


def _ffn_kernel(x_ref, attn_ref, ssm_ref, conv_ref, mod_ref, wout_ref, gffn_ref, wg_ref, wu_ref, wd_ref,
                gfin_ref, o_ref, *, final):
    n_attn = MLA_HEADS * V_HEAD
    mixed = _dot(attn_ref[...], wout_ref[0:n_attn, :])
    mixed = mixed + _dot(ssm_ref[...], wout_ref[n_attn:n_attn + SSD_INNER, :])
    mixed = mixed + _dot(conv_ref[...], wout_ref[n_attn + SSD_INNER:n_attn + SSD_INNER + CM_CH, :])
    g1 = mod_ref[0, :, 2 * D_MODEL:3 * D_MODEL]
    sh2 = mod_ref[0, :, 3 * D_MODEL:4 * D_MODEL]
    sc2 = mod_ref[0, :, 4 * D_MODEL:5 * D_MODEL]
    g2 = mod_ref[0, :, 5 * D_MODEL:6 * D_MODEL]
    x = x_ref[...] + g1 * mixed
    h = (_rms(x) * gffn_ref[...] * (1.0 + sc2) + sh2).astype(BF16)
    ff = jnp.zeros(x.shape, F32)
    for j in range(D_FF // FF_CHUNK):
        cols = slice(j * FF_CHUNK, (j + 1) * FF_CHUNK)
        gate = _dot(h, wg_ref[:, cols])
        up = _dot(h, wu_ref[:, cols])
        ff = ff + _dot((_silu(gate) * up).astype(BF16), wd_ref[cols, :])
    x = x + g2 * ff
    if final:
        x = _rms(x) * gfin_ref[...]
    o_ref[...] = x


def _ffn_call(l, x, attn, ssm, conv, mod, wout, gffn, wg, wu, wd, gfin, *, tmod, mod_row0, final):
    n = x.shape[0]
    tm = TM_FFN

    def tok(i):
        return (i, 0)

    return pl.pallas_call(
        functools.partial(_ffn_kernel, final=final),
        out_shape=jax.ShapeDtypeStruct((n, D_MODEL), F32),
        grid=(n // tm,),
        in_specs=[
            pl.BlockSpec((tm, D_MODEL), tok),
            pl.BlockSpec((tm, MLA_HEADS * V_HEAD), tok),
            pl.BlockSpec((tm, SSD_INNER), tok),
            pl.BlockSpec((tm, CM_CH), tok),
            _mod_spec(l, tm, tmod, mod_row0),
            _layer_spec((D_MODEL, D_MODEL), l, single_buffer=True),
            _layer_spec((1, D_MODEL), l),
            _layer_spec((D_MODEL, D_FF), l, single_buffer=True),
            _layer_spec((D_MODEL, D_FF), l, single_buffer=True),
            _layer_spec((D_FF, D_MODEL), l, single_buffer=True),
            pl.BlockSpec((1, D_MODEL), lambda i: (0, 0)),
        ],
        out_specs=pl.BlockSpec((tm, D_MODEL), tok),
        compiler_params=_params(("parallel",)),
        name="out_ffn",
    )(x, attn, ssm, conv, mod, wout, gffn, wg, wu, wd, gfin)


def _rope_tables():
    rows = DEC_SEQ // GRID_W
    row = jnp.repeat(jnp.arange(rows), GRID_W).astype(F32)
    col = (jnp.arange(rows * GRID_W) % GRID_W).astype(F32)
    nf = QK_ROPE // 4
    inv = ROPE_THETA ** (-jnp.arange(nf, dtype=F32) / nf)
    ang = jnp.stack([row[:, None] * inv, col[:, None] * inv], axis=1)
    cos = jnp.cos(ang)
    sin = jnp.sin(ang)
    cos32 = jnp.stack([cos, cos], axis=2).reshape(DEC_SEQ, QK_ROPE)
    sin32 = jnp.stack([-sin, sin], axis=2).reshape(DEC_SEQ, QK_ROPE)
    ones = jnp.ones((DEC_SEQ, QK_NOPE), F32)
    pad1 = jnp.ones((DEC_SEQ, HEAD_PAD - QK_HEAD), F32)
    cos_t = jnp.concatenate([ones, cos32, pad1], axis=1)
    sin_t = jnp.concatenate([0.0 * ones, sin32, 0.0 * pad1], axis=1)
    return cos_t, sin_t


def _kr_placement():
    src = jnp.arange(QK_ROPE)
    cols = jnp.arange(MLA_HEADS * HEAD_PAD)
    hit = (cols[None, :] % HEAD_PAD) == (QK_NOPE + src[:, None])
    return hit.astype(BF16)


def _pack_states(h0):
    lead = h0.shape[:-3]
    s = jnp.moveaxis(h0, -1, -3).reshape(lead + (SSD_STATE, SSD_INNER))
    lane_group = jnp.arange(SSD_INNER) // LANE
    parts = [jnp.where(lane_group == g, s, 0.0) for g in range(SSD_GROUPS)]
    return jnp.concatenate(parts, axis=-2)


def _unpack_states(st):
    b = st.shape[0]
    s = st.reshape(b, 2, SSD_GROUPS, SSD_STATE, 2, SSD_HEAD_DIM).transpose(0, 1, 2, 4, 5, 3)
    return s.reshape(b, 2, SSD_HEADS, SSD_HEAD_DIM, SSD_STATE)


def kernel(x_prompt, x_sample, c, cache_ckv, cache_krope, state_ssd, c_ctx, w_ada, b_ada, g_mix, w_in, g_q, w_uq, g_kv, w_ukv, ssd_conv_w, ssd_conv_b, ssd_dt_bias, ssd_a_log, ssd_d, ssd_norm_g, cm_conv_w, cm_conv_b, cm_ln_g, cm_ln_b, w_out, g_ffn, w_gate, w_up, w_down, g_final):
    L = DEPTH
    win = _place_call(w_in.astype(BF16), _src_in(), "place_w_in")
    wuq = _place_call(w_uq, _src_uq(), "place_w_uq")
    wukv = _place_call(w_ukv, _src_ukv(), "place_w_ukv")
    wout = w_out.astype(BF16)
    wg = w_gate.astype(BF16)
    wu = w_up.astype(BF16)
    wd = w_down.astype(BF16)
    rope_tabs = _rope_tables()

    cond8 = jnp.concatenate([c_ctx[None, :], c, jnp.zeros((8 - 1 - DEC_BATCH, D_MODEL), F32)], axis=0)
    mod = _ada_call(cond8, w_ada, b_ada)
    mod = mod.reshape(L, 8, 1, N_MOD * D_MODEL)

    kc_all, vc_all = _ctxkv_call(cache_ckv, cache_krope, wukv, _kr_placement())

    ssd_cw = jnp.concatenate([ssd_conv_w, jnp.zeros((L, SUBLANE - SSD_CONV, SSD_XBC), F32)], axis=1)
    cm_w = jnp.concatenate([cm_conv_w, jnp.zeros((L, 4 * SUBLANE - CM_WIDTH, CM_CH), F32)], axis=1)
    lane_pad = lambda v: jnp.concatenate(
        [v.reshape(L, 1, 2 * SSD_HEADS), jnp.zeros((L, 1, LANE - 2 * SSD_HEADS), F32)], axis=-1)
    dtb = lane_pad(ssd_dt_bias)
    alog = lane_pad(ssd_a_log)
    d_rep = jnp.repeat(ssd_d, SSD_HEAD_DIM, axis=-1)
    d0 = d_rep[:, 0:1, :]
    d1 = d_rep[:, 1:2, :]
    h0_lat = _pack_states(jnp.swapaxes(state_ssd, 0, 1))
    gmix, gq, gkv, gffn = _vec(g_mix), _vec(g_q), _vec(g_kv), _vec(g_ffn)
    scb, sng = _vec(ssd_conv_b), _vec(ssd_norm_g)
    ccb, clg, clb = _vec(cm_conv_b), _vec(cm_ln_g), _vec(cm_ln_b)
    gfin = g_final.reshape(1, D_MODEL)

    streams = [
        dict(x=x_prompt.reshape(N_CTX, D_MODEL), nb=BATCH, t=SEQ, tmod=N_CTX, mod_row0=0, tq=SEQ, ngroups=2,
             rope=None),
        dict(x=x_sample.reshape(N_LAT, D_MODEL), nb=DEC_BATCH, t=DEC_SEQ, tmod=DEC_SEQ, mod_row0=1, tq=TQ_LAT,
             ngroups=1, rope=rope_tabs),
    ]
    ckvs, krs, hss = [], [], []
    for l in range(L):
        for si, s in enumerate(streams):
            lat = si == 1
            nb, t = s["nb"], s["t"]
            q, k, v, ckv, kr, z, xbc, dt, cm = _inproj_call(
                l, s["x"], mod, gmix, win, gq, wuq, gkv, wukv, s["rope"],
                t=t, tmod=s["tmod"], mod_row0=s["mod_row0"])
            attn = _attn_call(l, q, k, v, kc_all if lat else None, vc_all if lat else None,
                              nb=nb, t=t, tq=s["tq"], ngroups=s["ngroups"])
            ssm, hf = _ssd_call(l, z, xbc, dt, ssd_cw, scb, dtb, alog, d0, d1, sng,
                                h0_lat if lat else None, nb=nb, t=t)
            conv = _cm_call(l, cm, cm_w, ccb, clg, clb, nb=nb, t=t)
            s["x"] = _ffn_call(l, s["x"], attn, ssm, conv, mod, wout, gffn, wg, wu, wd, gfin,
                               tmod=s["tmod"], mod_row0=s["mod_row0"], final=(l == L - 1))
            if not lat:
                ckvs.append(ckv.reshape(BATCH, SEQ, KV_RANK))
                krs.append(kr[:, QK_NOPE:QK_HEAD].reshape(BATCH, SEQ, QK_ROPE))
                hss.append(_unpack_states(hf))

    y_prompt = streams[0]["x"].reshape(BATCH, SEQ, D_MODEL)
    y_sample = streams[1]["x"].reshape(DEC_BATCH, DEC_SEQ, D_MODEL)
    return (y_prompt, y_sample, jnp.stack(ckvs, axis=1), jnp.stack(krs, axis=1), jnp.stack(hss, axis=1))
```

```python
import functools

import numpy as np

import jax
import jax.numpy as jnp
from jax import lax
from jax.experimental import pallas as pl
from jax.experimental.pallas import tpu as pltpu

F32 = jnp.float32
BF16 = jnp.bfloat16

D_MODEL = 1024
BATCH = 16
SEQ = 256
DEPTH = 4
DEC_BATCH = 4
DEC_SEQ = 2048
PAST_LEN = 256
GRID_W = 64

MLA_HEADS = 8
QK_NOPE = 64
QK_ROPE = 32
QK_HEAD = QK_NOPE + QK_ROPE
V_HEAD = 64
Q_RANK = 256
KV_RANK = 128
ROPE_THETA = 10000.0
LOG2_E = 1.4426950408889634

SSD_HEADS = 4
SSD_HEAD_DIM = 64
SSD_INNER = SSD_HEADS * SSD_HEAD_DIM
SSD_GROUPS = 2
SSD_STATE = 64
SSD_CONV = 5
SSD_CHUNK = 128
SSD_XBC = SSD_INNER + 2 * SSD_GROUPS * SSD_STATE

CM_CH = 256
CM_WIDTH = 31

D_FF = 2816
N_MOD = 6
IN_WIDTH = (Q_RANK + KV_RANK + QK_ROPE) + (SSD_INNER + SSD_XBC + 2 * SSD_HEADS) + 2 * CM_CH

N_CTX = BATCH * SEQ
N_LAT = DEC_BATCH * DEC_SEQ

SUBLANE = 8
LANE = 128
MXU_N = 256
HEAD_PAD = 128
HEADS_PER_STEP = MXU_N // V_HEAD
VMEM_LIMIT = 48 * 1024 * 1024

TM_IN = 512
TM_FFN = 512
TQ_LAT = 512
FF_CHUNK = 256
CONV_ROWS = 128
CM_ROWS = 64
SSD_CONV_ROWS = 32

C_QLAT = 0
C_CKV = C_QLAT + Q_RANK
C_DT = C_CKV + KV_RANK
C_KR = C_DT + LANE
C_Z = C_KR + 2 * HEAD_PAD
C_XBC = C_Z + SSD_INNER
C_CM = C_XBC + SSD_XBC
IN_ARR = C_CM + 2 * CM_CH
IN_PAD = -(-IN_WIDTH // LANE) * LANE


def _params(sem):
    return pltpu.CompilerParams(dimension_semantics=sem, vmem_limit_bytes=VMEM_LIMIT)


def _dot(a, b):
    return jnp.dot(a, b, preferred_element_type=F32)


def _dot_nt(a, b):
    return lax.dot_general(a, b, (((1,), (1,)), ((), ())), preferred_element_type=F32)


def _rms(x, eps=1e-6):
    return x * lax.rsqrt(jnp.mean(x * x, axis=-1, keepdims=True) + eps)


def _silu(x):
    return x * jax.nn.sigmoid(x)


def _layer_spec(shape, l, single_buffer=False):
    nd = len(shape)
    kw = dict(pipeline_mode=pl.Buffered(1)) if single_buffer else {}
    return pl.BlockSpec((None,) + tuple(shape), lambda *_: (l,) + (0,) * nd, **kw)


def _vec(v):
    return v.reshape(DEPTH, 1, -1)


def _place_kernel(w_ref, p_ref, o_ref):
    o_ref[0] = _dot(w_ref[0].astype(BF16), p_ref[...]).astype(BF16)


def _place_call(w, src, name):
    L, rows, cin = w.shape
    cout = src.shape[0]
    place = (jnp.arange(cin, dtype=jnp.int32)[:, None] == jnp.asarray(src, jnp.int32)[None, :]).astype(BF16)
    tr = min(rows, 256)
    return pl.pallas_call(
        _place_kernel,
        out_shape=jax.ShapeDtypeStruct((L, rows, cout), BF16),
        grid=(L, rows // tr),
        in_specs=[
            pl.BlockSpec((1, tr, cin), lambda l, i: (l, i, 0)),
            pl.BlockSpec((cin, cout), lambda l, i: (0, 0)),
        ],
        out_specs=pl.BlockSpec((1, tr, cout), lambda l, i: (l, i, 0)),
        compiler_params=_params(("parallel", "parallel")),
        name=name,
    )(w, place)


def _swap32():
    return np.arange(QK_ROPE).reshape(2, 2, QK_ROPE // 4)[:, ::-1, :].reshape(-1)


def _src_in():
    kr0 = Q_RANK + KV_RANK
    off_ssd = kr0 + QK_ROPE
    off_dt = off_ssd + SSD_INNER + SSD_XBC
    off_cm = off_dt + 2 * SSD_HEADS
    src = -np.ones((IN_ARR,), np.int64)
    src[0:C_DT] = np.arange(0, kr0)
    src[C_DT:C_DT + 2 * SSD_HEADS] = np.arange(off_dt, off_cm)
    src[C_KR + QK_NOPE:C_KR + QK_HEAD] = kr0 + np.arange(QK_ROPE)
    src[C_KR + HEAD_PAD + QK_NOPE:C_KR + HEAD_PAD + QK_HEAD] = kr0 + _swap32()
    src[C_Z:C_CM] = np.arange(off_ssd, off_dt)
    src[C_CM:IN_ARR] = np.arange(off_cm, IN_WIDTH)
    return src


def _src_uq():
    half = MLA_HEADS * HEAD_PAD
    src = -np.ones((2 * half,), np.int64)
    for h in range(MLA_HEADS):
        src[h * HEAD_PAD:h * HEAD_PAD + QK_HEAD] = h * QK_HEAD + np.arange(QK_HEAD)
        lo = half + h * HEAD_PAD + QK_NOPE
        src[lo:lo + QK_ROPE] = h * QK_HEAD + QK_NOPE + _swap32()
    return src


def _src_ukv():
    half = MLA_HEADS * HEAD_PAD
    src = -np.ones((half + MLA_HEADS * V_HEAD,), np.int64)
    for h in range(MLA_HEADS):
        base = h * (QK_NOPE + V_HEAD)
        src[h * HEAD_PAD:h * HEAD_PAD + QK_NOPE] = base + np.arange(QK_NOPE)
        src[half + h * V_HEAD:half + (h + 1) * V_HEAD] = base + QK_NOPE + np.arange(V_HEAD)
    return src


def _ada_kernel(cond_ref, w_ref, b_ref, o_ref):
    c = cond_ref[...]
    a = _silu(c).astype(BF16)
    o_ref[0] = _dot(a, w_ref[0].astype(BF16)) + b_ref[0]


def _ada_call(cond8, w_ada, b_ada):
    tn = 1536
    nt = (N_MOD * D_MODEL) // tn
    return pl.pallas_call(
        _ada_kernel,
        out_shape=jax.ShapeDtypeStruct((DEPTH, 8, N_MOD * D_MODEL), F32),
        grid=(DEPTH, nt),
        in_specs=[
            pl.BlockSpec((8, D_MODEL), lambda l, j: (0, 0)),
            pl.BlockSpec((1, D_MODEL, tn), lambda l, j: (l, 0, j)),
            pl.BlockSpec((1, 1, tn), lambda l, j: (l, 0, j)),
        ],
        out_specs=pl.BlockSpec((1, 8, tn), lambda l, j: (l, 0, j)),
        compiler_params=_params(("parallel", "parallel")),
        name="ada_mod",
    )(cond8, w_ada, b_ada.reshape(DEPTH, 1, N_MOD * D_MODEL))


def _mod_spec(l, tm, tmod, mod_row0):
    return pl.BlockSpec((None, 1, 1, N_MOD * D_MODEL), lambda i: (l, mod_row0 + (i * tm) // tmod, 0, 0))


def _inproj_kernel(*refs, rope):
    if rope:
        (x_ref, mod_ref, gmix_ref, win_ref, gq_ref, wuq_ref, gkv_ref, wukv_ref, cos_ref, sin_ref,
         q_out, k_out, v_out, ckv_out, kr_out, z_out, xbc_out, dt_out, cm_out) = refs
        cos = cos_ref[...]
        sin = sin_ref[...]
    else:
        (x_ref, mod_ref, gmix_ref, win_ref, gq_ref, wuq_ref, gkv_ref, wukv_ref,
         q_out, k_out, v_out, ckv_out, kr_out, z_out, xbc_out, dt_out, cm_out) = refs
    x = x_ref[...]
    sh1 = mod_ref[0, :, 0:D_MODEL]
    sc1 = mod_ref[0, :, D_MODEL:2 * D_MODEL]
    h = (_rms(x) * gmix_ref[...] * (1.0 + sc1) + sh1).astype(BF16)

    q_lat = _dot(h, win_ref[:, C_QLAT:C_QLAT + Q_RANK])
    qn = (_rms(q_lat) * gq_ref[...]).astype(BF16)
    qq = _dot(qn, wuq_ref[...])
    scale = QK_HEAD ** -0.5 * LOG2_E
    half = MLA_HEADS * HEAD_PAD
    for hd in range(MLA_HEADS):
        lo = hd * HEAD_PAD
        qh = qq[:, lo:lo + HEAD_PAD]
        if rope:
            qh = qh * cos + qq[:, half + lo:half + lo + HEAD_PAD] * sin
        q_out[:, lo:lo + HEAD_PAD] = (qh * scale).astype(BF16)

    ncol = C_KR - C_CKV + (2 if rope else 1) * HEAD_PAD
    mid = _dot(h, win_ref[:, C_CKV:C_CKV + ncol])
    ckv = _rms(mid[:, 0:KV_RANK]) * gkv_ref[...]
    ckv_out[...] = ckv
    dt_out[...] = mid[:, C_DT - C_CKV:C_DT - C_CKV + LANE]
    kr = mid[:, C_KR - C_CKV:C_KR - C_CKV + HEAD_PAD]
    kr_out[...] = kr
    kr_rot = kr
    if rope:
        kr_rot = kr * cos + mid[:, C_KR - C_CKV + HEAD_PAD:C_KR - C_CKV + 2 * HEAD_PAD] * sin
    kv = _dot(ckv.astype(BF16), wukv_ref[...])
    for hd in range(MLA_HEADS):
        lo = hd * HEAD_PAD
        k_out[:, lo:lo + HEAD_PAD] = (kv[:, lo:lo + HEAD_PAD] + kr_rot).astype(BF16)
    v_out[...] = kv[:, half:half + MLA_HEADS * V_HEAD].astype(BF16)

    z_out[...] = _dot(h, win_ref[:, C_Z:C_Z + SSD_INNER])
    xbc_out[...] = _dot(h, win_ref[:, C_XBC:C_XBC + SSD_XBC])
    cm_out[...] = _dot(h, win_ref[:, C_CM:C_CM + 2 * CM_CH])


def _inproj_call(l, x, mod, gmix, win, gq, wuq, gkv, wukv, rope_tabs, *, t, tmod, mod_row0):
    n = x.shape[0]
    tm = TM_IN
    rope = rope_tabs is not None
    half = MLA_HEADS * HEAD_PAD

    def tok(i):
        return (i, 0)

    outs = [
        ((n, half), BF16),
        ((n, half), BF16),
        ((n, MLA_HEADS * V_HEAD), BF16),
        ((n, KV_RANK), F32),
        ((n, HEAD_PAD), F32),
        ((n, SSD_INNER), F32),
        ((n, SSD_XBC), F32),
        ((n, LANE), F32),
        ((n, 2 * CM_CH), F32),
    ]
    in_specs = [
        pl.BlockSpec((tm, D_MODEL), tok),
        _mod_spec(l, tm, tmod, mod_row0),
        _layer_spec((1, D_MODEL), l),
        _layer_spec((D_MODEL, IN_ARR), l, single_buffer=True),
        _layer_spec((1, Q_RANK), l),
        _layer_spec((Q_RANK, 2 * half if rope else half), l, single_buffer=True),
        _layer_spec((1, KV_RANK), l),
        _layer_spec((KV_RANK, MLA_HEADS * (HEAD_PAD + V_HEAD)), l, single_buffer=True),
    ]
    args = [x, mod, gmix, win, gq, wuq, gkv, wukv]
    if rope:
        pos = lambda i: (i % (t // tm), 0)
        in_specs += [pl.BlockSpec((tm, HEAD_PAD), pos), pl.BlockSpec((tm, HEAD_PAD), pos)]
        args += list(rope_tabs)
    return pl.pallas_call(
        functools.partial(_inproj_kernel, rope=rope),
        out_shape=[jax.ShapeDtypeStruct(s, d) for s, d in outs],
        grid=(n // tm,),
        in_specs=in_specs,
        out_specs=[pl.BlockSpec((tm, s[1]), tok) for s, _ in outs],
        compiler_params=_params(("parallel",)),
        name="in_proj_lat" if rope else "in_proj_ctx",
    )(*args)


def _ctxkv_kernel(ckv_ref, kr_ref, wukv_ref, place_ref, k_out, v_out):
    kv = _dot(ckv_ref[0, 0].astype(BF16), wukv_ref[0])
    krp = _dot(kr_ref[0, 0].astype(BF16), place_ref[...])
    half = MLA_HEADS * HEAD_PAD
    k_out[0, 0] = (kv[:, 0:half] + krp).astype(BF16)
    v_out[0, 0] = kv[:, half:half + MLA_HEADS * V_HEAD].astype(BF16)


def _ctxkv_call(cache_ckv, cache_krope, wukv_all, place):
    return pl.pallas_call(
        _ctxkv_kernel,
        out_shape=[
            jax.ShapeDtypeStruct((DEPTH, DEC_BATCH, PAST_LEN, MLA_HEADS * HEAD_PAD), BF16),
            jax.ShapeDtypeStruct((DEPTH, DEC_BATCH, PAST_LEN, MLA_HEADS * V_HEAD), BF16),
        ],
        grid=(DEPTH, DEC_BATCH),
        in_specs=[
            pl.BlockSpec((1, 1, PAST_LEN, KV_RANK), lambda l, b: (b, l, 0, 0)),
            pl.BlockSpec((1, 1, PAST_LEN, QK_ROPE), lambda l, b: (b, l, 0, 0)),
            pl.BlockSpec((1, KV_RANK, MLA_HEADS * (HEAD_PAD + V_HEAD)), lambda l, b: (l, 0, 0)),
            pl.BlockSpec((QK_ROPE, MLA_HEADS * HEAD_PAD), lambda l, b: (0, 0)),
        ],
        out_specs=[
            pl.BlockSpec((1, 1, PAST_LEN, MLA_HEADS * HEAD_PAD), lambda l, b: (l, b, 0, 0)),
            pl.BlockSpec((1, 1, PAST_LEN, MLA_HEADS * V_HEAD), lambda l, b: (l, b, 0, 0)),
        ],
        compiler_params=_params(("parallel", "parallel")),
        name="ctx_kv",
    )(cache_ckv, cache_krope, wukv_all, place)


def _attn_kernel(*refs, has_ctx, ngroups):
    if has_ctx:
        q_ref, k_ref, v_ref, kc_ref, vc_ref, o_ref = refs
    else:
        q_ref, k_ref, v_ref, o_ref = refs
    gw = HEADS_PER_STEP * V_HEAD
    for grp in range(ngroups):
        vg = v_ref[:, grp * gw:(grp + 1) * gw]
        if has_ctx:
            vcg = vc_ref[0, :, grp * gw:(grp + 1) * gw]
        o_sel = None
        den_sel = None
        for hh in range(HEADS_PER_STEP):
            lo = (grp * HEADS_PER_STEP + hh) * HEAD_PAD
            qh = q_ref[:, lo:lo + HEAD_PAD]
            s = _dot_nt(qh, k_ref[:, lo:lo + HEAD_PAD])
            m = jnp.max(s, axis=-1, keepdims=True)
            if has_ctx:
                sc = _dot_nt(qh, kc_ref[0, :, lo:lo + HEAD_PAD])
                m = jnp.maximum(m, jnp.max(sc, axis=-1, keepdims=True))
            p = jnp.exp2(s - m)
            den = jnp.sum(p, axis=-1, keepdims=True)
            o = _dot(p.astype(BF16), vg)
            if has_ctx:
                pc = jnp.exp2(sc - m)
                den = den + jnp.sum(pc, axis=-1, keepdims=True)
                o = o + _dot(pc.astype(BF16), vcg)
            if hh == 0:
                o_sel = o
                den_sel = jnp.broadcast_to(den, o.shape)
            else:
                mine = lax.broadcasted_iota(jnp.int32, o.shape, 1) >= hh * V_HEAD
                o_sel = jnp.where(mine, o, o_sel)
                den_sel = jnp.where(mine, den, den_sel)
        o_ref[:, grp * gw:(grp + 1) * gw] = (o_sel / den_sel).astype(BF16)


def _attn_call(l, q, k, v, kc, vc, *, nb, t, tq, ngroups):
    has_ctx = kc is not None
    nq = t // tq
    nh = ngroups * HEADS_PER_STEP
    nsteps = MLA_HEADS // nh
    in_specs = [
        pl.BlockSpec((tq, nh * HEAD_PAD), lambda b, g, qi: (b * nq + qi, g)),
        pl.BlockSpec((t, nh * HEAD_PAD), lambda b, g, qi: (b, g)),
        pl.BlockSpec((t, nh * V_HEAD), lambda b, g, qi: (b, g)),
    ]
    args = [q, k, v]
    if has_ctx:
        in_specs += [
            pl.BlockSpec((None, 1, PAST_LEN, nh * HEAD_PAD), lambda b, g, qi: (l, b, 0, g)),
            pl.BlockSpec((None, 1, PAST_LEN, nh * V_HEAD), lambda b, g, qi: (l, b, 0, g)),
        ]
        args += [kc, vc]
    return pl.pallas_call(
        functools.partial(_attn_kernel, has_ctx=has_ctx, ngroups=ngroups),
        out_shape=jax.ShapeDtypeStruct((nb * t, MLA_HEADS * V_HEAD), BF16),
        grid=(nb, nsteps, nq),
        in_specs=in_specs,
        out_specs=pl.BlockSpec((tq, nh * V_HEAD), lambda b, g, qi: (b * nq + qi, g)),
        compiler_params=_params(("parallel", "parallel", "arbitrary")),
        name="attn_lat" if has_ctx else "attn_ctx",
    )(*args)


def _split3(x):
    hi = x.astype(BF16)
    r = x - hi.astype(F32)
    mid = r.astype(BF16)
    lo = (r - mid.astype(F32)).astype(BF16)
    return hi, mid, lo


def _ssd_kernel(*refs, t, has_h0):
    if has_h0:
        (z_ref, xbc_ref, dt_ref, cw_ref, cb_ref, dtb_ref, alog_ref, d0_ref, d1_ref, gn_ref, h0_ref,
         y_out, hf_out, xpad, xc, dtp_s, acs_s, acst_s, ysc, ssc, dec_s, eacs_s, hin_s, st) = refs
    else:
        (z_ref, xbc_ref, dt_ref, cw_ref, cb_ref, dtb_ref, alog_ref, d0_ref, d1_ref, gn_ref,
         y_out, hf_out, xpad, xc, dtp_s, acs_s, acst_s, ysc, ssc, dec_s, eacs_s, hin_s, st) = refs
    L = SSD_CHUNK
    W = SSD_INNER
    nc = t // L
    halo = SUBLANE

    xpad[0:halo, :] = jnp.zeros((halo, SSD_XBC), F32)
    xpad[halo + t:halo + t + halo, :] = jnp.zeros((halo, SSD_XBC), F32)
    xpad[halo:halo + t, :] = xbc_ref[...]
    cw = cw_ref[...]
    cb = cb_ref[...]
    rc = SSD_CONV_ROWS
    nwin = rc + 2 * halo

    def conv(blk, carry):
        base = pl.multiple_of(blk * rc, rc)
        win = xpad[pl.ds(base, nwin), :]
        acc = cb + cw[SSD_CONV // 2:SSD_CONV // 2 + 1, :] * win[halo:halo + rc, :]
        for kk in range(SSD_CONV):
            if kk != SSD_CONV // 2:
                first = halo - SSD_CONV // 2 + kk
                acc = acc + cw[kk:kk + 1, :] * pltpu.roll(win, nwin - first, axis=0)[0:rc, :]
        xc[pl.ds(base, rc), :] = _silu(acc)
        return carry

    lax.fori_loop(0, t // rc, conv, 0, unroll=2)
    r = CONV_ROWS

    a_all = -jnp.exp(alog_ref[...])
    dtb = dtb_ref[...]
    row_i = lax.broadcasted_iota(jnp.int32, (L, L), 0)
    col_i = lax.broadcasted_iota(jnp.int32, (L, L), 1)
    tri = [(row_i >= col_i), (row_i <= col_i)]
    tri_bf = [jnp.where(m, 1.0, 0.0).astype(BF16) for m in tri]
    grp_rows = [row_i < SSD_STATE, row_i >= SSD_STATE]
    lane_w = lax.broadcasted_iota(jnp.int32, (L, W), 1)
    own_block = (lax.broadcasted_iota(jnp.int32, (L, W), 0) // SSD_STATE) == (lane_w // LANE)

    def chunk_rows(c):
        return pl.ds(pl.multiple_of(c * L, L), L)

    lane_lo = col_i < SSD_HEAD_DIM

    def columns(m, d):
        return [jnp.broadcast_to(m[:, j:j + 1], (L, LANE)) for j in range(d * SSD_HEADS, (d + 1) * SSD_HEADS)]

    def expand(cols):
        return jnp.concatenate([jnp.where(lane_lo, cols[0], cols[1]), jnp.where(lane_lo, cols[2], cols[3])],
                               axis=1)

    def pre(c, carry):
        rows = chunk_rows(c)
        dtp = jax.nn.softplus(dt_ref[rows, :] + dtb)
        dtp_s[rows, :] = dtp
        x3 = jnp.concatenate(_split3(dtp * a_all), axis=1)
        rf = _dot(tri_bf[0], x3)
        rb = _dot(tri_bf[1], x3)
        acs_f = rf[:, 0:LANE] + rf[:, LANE:2 * LANE] + rf[:, 2 * LANE:3 * LANE]
        acs_b = rb[:, 0:LANE] + rb[:, LANE:2 * LANE] + rb[:, 2 * LANE:3 * LANE]
        acs = jnp.where(col_i < SSD_HEADS, acs_f, acs_b)
        acs_s[rows, :] = acs
        acst_s[pl.ds(pl.multiple_of(c * SUBLANE, SUBLANE), SUBLANE), :] = acs.T[0:SUBLANE, :]
        return carry

    lax.fori_loop(0, nc, pre, 0, unroll=2)

    def main(c, carry):
        rows = chunk_rows(c)
        xs = xc[rows, 0:W]
        cm = xc[rows, W + LANE:W + 2 * LANE].astype(BF16)
        bt = xc[rows, W:W + LANE].T
        btm2 = jnp.concatenate([jnp.where(grp_rows[0], bt, 0.0), jnp.where(grp_rows[1], bt, 0.0)],
                               axis=1).astype(BF16)
        g2 = _dot(cm, btm2)
        bt_bf = bt.astype(BF16)
        dtp = dtp_s[rows, :]
        acs = acs_s[rows, :]
        acst = acst_s[pl.ds(pl.multiple_of(c * SUBLANE, SUBLANE), SUBLANE), :]
        ysum = None
        for d in range(2):
            acs_cols = columns(acs, d)
            acs_b = expand(acs_cols)
            eacs_s[c, d] = jnp.exp(acs_b)
            xdt = xs * expand(columns(dtp, d))
            xdt_bf = xdt.astype(BF16)
            yd = None
            for h in range(SSD_HEADS):
                j = d * SSD_HEADS + h
                g = h // (SSD_HEADS // SSD_GROUPS)
                diff = acs_cols[h] - acst[j:j + 1, :]
                seg = jnp.exp(jnp.where(tri[d], diff, -jnp.inf))
                res = _dot((g2[:, g * L:(g + 1) * L] * seg).astype(BF16), xdt_bf)
                yd = res if h == 0 else jnp.where(lane_w >= h * SSD_HEAD_DIM, res, yd)
            ysum = yd if d == 0 else ysum + yd
            tot = acs_b[L - 1:L, :] if d == 0 else acs_b[0:1, :]
            s_new = _dot(bt_bf, (xdt * jnp.exp(tot - acs_b)).astype(BF16))
            ssc[c, d] = jnp.where(own_block, s_new, 0.0)
            dec_s[c, d] = jnp.broadcast_to(jnp.exp(tot), (SUBLANE, W))
        ysc[rows, :] = ysum
        return carry

    lax.fori_loop(0, nc, main, 0, unroll=2)

    if has_h0:
        st[...] = h0_ref[0]
    else:
        st[...] = jnp.zeros(st.shape, F32)

    def rec(i, carry):
        for d, c in ((0, i), (1, nc - 1 - i)):
            h_in = st[d]
            hin_s[c, d] = h_in.astype(BF16)
            st[d] = dec_s[c, d][0:1, :] * h_in + ssc[c, d]
        return carry

    lax.fori_loop(0, nc, rec, 0)
    for d in range(2):
        for g in range(SSD_GROUPS):
            hf_out[0, d, g] = st[d, g * SSD_STATE:(g + 1) * SSD_STATE, g * LANE:(g + 1) * LANE]

    d0 = d0_ref[...]
    d1 = d1_ref[...]
    gn = gn_ref[...]

    def fin(c, carry):
        rows = chunk_rows(c)
        xs = xc[rows, 0:W]
        cm = xc[rows, W + LANE:W + 2 * LANE].astype(BF16)
        y = ysc[rows, :] + d0 * xs + d1 * xs
        for d in range(2):
            y = y + _dot(cm, hin_s[c, d]) * eacs_s[c, d]
        y = y * _silu(z_ref[rows, :])
        y_out[rows, :] = (_rms(y) * gn).astype(BF16)
        return carry

    lax.fori_loop(0, nc, fin, 0, unroll=2)


def _ssd_call(l, z, xbc, dt, cw, cb, dtb, alog, d0, d1, gn, h0, *, nb, t):
    has_h0 = h0 is not None
    nc = t // SSD_CHUNK

    def tok(b):
        return (b, 0)

    in_specs = [
        pl.BlockSpec((t, SSD_INNER), tok),
        pl.BlockSpec((t, SSD_XBC), tok),
        pl.BlockSpec((t, LANE), tok),
        _layer_spec((SUBLANE, SSD_XBC), l),
        _layer_spec((1, SSD_XBC), l),
        _layer_spec((1, LANE), l),
        _layer_spec((1, LANE), l),
        _layer_spec((1, SSD_INNER), l),
        _layer_spec((1, SSD_INNER), l),
        _layer_spec((1, SSD_INNER), l),
    ]
    args = [z, xbc, dt, cw, cb, dtb, alog, d0, d1, gn]
    if has_h0:
        in_specs.append(pl.BlockSpec((None, 1, 2, LANE, SSD_INNER), lambda b: (l, b, 0, 0, 0)))
        args.append(h0)
    return pl.pallas_call(
        functools.partial(_ssd_kernel, t=t, has_h0=has_h0),
        out_shape=[
            jax.ShapeDtypeStruct((nb * t, SSD_INNER), BF16),
            jax.ShapeDtypeStruct((nb, 2, SSD_GROUPS, SSD_STATE, LANE), F32),
        ],
        grid=(nb,),
        in_specs=in_specs,
        out_specs=[
            pl.BlockSpec((t, SSD_INNER), tok),
            pl.BlockSpec((1, 2, SSD_GROUPS, SSD_STATE, LANE), lambda b: (b, 0, 0, 0, 0)),
        ],
        scratch_shapes=[
            pltpu.VMEM((t + 2 * SUBLANE, SSD_XBC), F32),
            pltpu.VMEM((t, SSD_XBC), F32),
            pltpu.VMEM((t, LANE), F32),
            pltpu.VMEM((t, LANE), F32),
            pltpu.VMEM((nc * SUBLANE, LANE), F32),
            pltpu.VMEM((t, SSD_INNER), F32),
            pltpu.VMEM((nc, 2, LANE, SSD_INNER), F32),
            pltpu.VMEM((nc, 2, SUBLANE, SSD_INNER), F32),
            pltpu.VMEM((nc, 2, LANE, SSD_INNER), F32),
            pltpu.VMEM((nc, 2, LANE, SSD_INNER), BF16),
            pltpu.VMEM((2, LANE, SSD_INNER), F32),
        ],
        compiler_params=_params(("parallel",)),
        name="ssd_lat" if has_h0 else "ssd_ctx",
    )(*args)


def _cm_kernel(cm_ref, w_ref, b_ref, lg_ref, lb_ref, o_ref, gpad, *, t):
    halo = 16
    gpad[0:halo, :] = jnp.zeros((halo, CM_CH), F32)
    gpad[halo + t:halo + t + halo, :] = jnp.zeros((halo, CM_CH), F32)
    r = CONV_ROWS
    for blk in range(t // r):
        rows = slice(blk * r, (blk + 1) * r)
        gpad[halo + blk * r:halo + (blk + 1) * r, :] = (
            cm_ref[rows, 0:CM_CH] * jax.nn.sigmoid(cm_ref[rows, CM_CH:2 * CM_CH]))
    w = w_ref[...]
    bias = b_ref[...]
    lg = lg_ref[...]
    lb = lb_ref[...]
    rb = CM_ROWS
    sub = SUBLANE
    assert halo - CM_WIDTH // 2 == 1 and CM_WIDTH <= 4 * sub

    def body(blk, carry):
        base = pl.multiple_of(blk * rb, rb)
        acc = bias
        for j in range(sub):
            pj = None
            for a in range(4):
                kk = sub * a + j
                if kk < CM_WIDTH:
                    term = w[kk:kk + 1, :] * gpad[pl.ds(base + sub * a, rb + sub), :]
                    pj = term if pj is None else pj + term
            acc = acc + pj[j + 1:j + 1 + rb, :]
        mu = jnp.mean(acc, axis=-1, keepdims=True)
        xc = acc - mu
        var = jnp.mean(xc * xc, axis=-1, keepdims=True)
        y = xc * lax.rsqrt(var + 1e-5) * lg + lb
        o_ref[pl.ds(base, rb), :] = _silu(y).astype(BF16)
        return carry

    lax.fori_loop(0, t // rb, body, 0, unroll=2)


def _cm_call(l, cm, w, b, lg, lb, *, nb, t):
    return pl.pallas_call(
        functools.partial(_cm_kernel, t=t),
        out_shape=jax.ShapeDtypeStruct((nb * t, CM_CH), BF16),
        grid=(nb,),
        in_specs=[
            pl.BlockSpec((t, 2 * CM_CH), lambda i: (i, 0)),
            _layer_spec((4 * SUBLANE, CM_CH), l),
            _layer_spec((1, CM_CH), l),
            _layer_spec((1, CM_CH), l),
            _layer_spec((1, CM_CH), l),
        ],
        out_specs=pl.BlockSpec((t, CM_CH), lambda i: (i, 0)),
        scratch_shapes=[pltpu.VMEM((t + 32, CM_CH), F32)],
        compiler_params=_params(("parallel",)),
        name="conv_module",
    )(cm, w, b, lg, lb)


def _ffn_kernel(x_ref, attn_ref, ssm_ref, conv_ref, mod_ref, wout_ref, gffn_ref, wg_ref, wu_ref, wd_ref,
                gfin_ref, o_ref, *, final):
    n_attn = MLA_HEADS * V_HEAD
    mixed = _dot(attn_ref[...], wout_ref[0:n_attn, :])
    mixed = mixed + _dot(ssm_ref[...], wout_ref[n_attn:n_attn + SSD_INNER, :])
    mixed = mixed + _dot(conv_ref[...], wout_ref[n_attn + SSD_INNER:n_attn + SSD_INNER + CM_CH, :])
    g1 = mod_ref[0, :, 2 * D_MODEL:3 * D_MODEL]
    sh2 = mod_ref[0, :, 3 * D_MODEL:4 * D_MODEL]
    sc2 = mod_ref[0, :, 4 * D_MODEL:5 * D_MODEL]
    g2 = mod_ref[0, :, 5 * D_MODEL:6 * D_MODEL]
    x = x_ref[...] + g1 * mixed
    h = (_rms(x) * gffn_ref[...] * (1.0 + sc2) + sh2).astype(BF16)
    ff = jnp.zeros(x.shape, F32)
    for j in range(D_FF // FF_CHUNK):
        cols = slice(j * FF_CHUNK, (j + 1) * FF_CHUNK)
        gate = _dot(h, wg_ref[:, cols])
        up = _dot(h, wu_ref[:, cols])
        ff = ff + _dot((_silu(gate) * up).astype(BF16), wd_ref[cols, :])
    x = x + g2 * ff
    if final:
        x = _rms(x) * gfin_ref[...]
    o_ref[...] = x


def _ffn_call(l, x, attn, ssm, conv, mod, wout, gffn, wg, wu, wd, gfin, *, tmod, mod_row0, final):
    n = x.shape[0]
    tm = TM_FFN

    def tok(i):
        return (i, 0)

    return pl.pallas_call(
        functools.partial(_ffn_kernel, final=final),
        out_shape=jax.ShapeDtypeStruct((n, D_MODEL), F32),
        grid=(n // tm,),
        in_specs=[
            pl.BlockSpec((tm, D_MODEL), tok),
            pl.BlockSpec((tm, MLA_HEADS * V_HEAD), tok),
            pl.BlockSpec((tm, SSD_INNER), tok),
            pl.BlockSpec((tm, CM_CH), tok),
            _mod_spec(l, tm, tmod, mod_row0),
            _layer_spec((D_MODEL, D_MODEL), l, single_buffer=True),
            _layer_spec((1, D_MODEL), l),
            _layer_spec((D_MODEL, D_FF), l, single_buffer=True),
            _layer_spec((D_MODEL, D_FF), l, single_buffer=True),
            _layer_spec((D_FF, D_MODEL), l, single_buffer=True),
            pl.BlockSpec((1, D_MODEL), lambda i: (0, 0)),
        ],
        out_specs=pl.BlockSpec((tm, D_MODEL), tok),
        compiler_params=_params(("parallel",)),
        name="out_ffn",
    )(x, attn, ssm, conv, mod, wout, gffn, wg, wu, wd, gfin)


def _rope_tables():
    rows = DEC_SEQ // GRID_W
    row = jnp.repeat(jnp.arange(rows), GRID_W).astype(F32)
    col = (jnp.arange(rows * GRID_W) % GRID_W).astype(F32)
    nf = QK_ROPE // 4
    inv = ROPE_THETA ** (-jnp.arange(nf, dtype=F32) / nf)
    ang = jnp.stack([row[:, None] * inv, col[:, None] * inv], axis=1)
    cos = jnp.cos(ang)
    sin = jnp.sin(ang)
    cos32 = jnp.stack([cos, cos], axis=2).reshape(DEC_SEQ, QK_ROPE)
    sin32 = jnp.stack([-sin, sin], axis=2).reshape(DEC_SEQ, QK_ROPE)
    ones = jnp.ones((DEC_SEQ, QK_NOPE), F32)
    pad1 = jnp.ones((DEC_SEQ, HEAD_PAD - QK_HEAD), F32)
    cos_t = jnp.concatenate([ones, cos32, pad1], axis=1)
    sin_t = jnp.concatenate([0.0 * ones, sin32, 0.0 * pad1], axis=1)
    return cos_t, sin_t


def _kr_placement():
    src = jnp.arange(QK_ROPE)
    cols = jnp.arange(MLA_HEADS * HEAD_PAD)
    hit = (cols[None, :] % HEAD_PAD) == (QK_NOPE + src[:, None])
    return hit.astype(BF16)


def _pack_states(h0):
    lead = h0.shape[:-3]
    s = jnp.moveaxis(h0, -1, -3).reshape(lead + (SSD_STATE, SSD_INNER))
    lane_group = jnp.arange(SSD_INNER) // LANE
    parts = [jnp.where(lane_group == g, s, 0.0) for g in range(SSD_GROUPS)]
    return jnp.concatenate(parts, axis=-2)


def _unpack_states(st):
    b = st.shape[0]
    s = st.reshape(b, 2, SSD_GROUPS, SSD_STATE, 2, SSD_HEAD_DIM).transpose(0, 1, 2, 4, 5, 3)
    return s.reshape(b, 2, SSD_HEADS, SSD_HEAD_DIM, SSD_STATE)


def kernel(x_prompt, x_sample, c, cache_ckv, cache_krope, state_ssd, c_ctx, w_ada, b_ada, g_mix, w_in, g_q, w_uq, g_kv, w_ukv, ssd_conv_w, ssd_conv_b, ssd_dt_bias, ssd_a_log, ssd_d, ssd_norm_g, cm_conv_w, cm_conv_b, cm_ln_g, cm_ln_b, w_out, g_ffn, w_gate, w_up, w_down, g_final):
    L = DEPTH
    w_in_pad = jnp.pad(w_in, ((0, 0), (0, 0), (0, IN_PAD - IN_WIDTH))).astype(BF16)
    win = _place_call(w_in_pad, _src_in(), "place_w_in")
    wuq = _place_call(w_uq, _src_uq(), "place_w_uq")
    wukv = _place_call(w_ukv, _src_ukv(), "place_w_ukv")
    wout = w_out.astype(BF16)
    wg = w_gate.astype(BF16)
    wu = w_up.astype(BF16)
    wd = w_down.astype(BF16)
    rope_tabs = _rope_tables()

    cond8 = jnp.concatenate([c_ctx[None, :], c, jnp.zeros((8 - 1 - DEC_BATCH, D_MODEL), F32)], axis=0)
    mod = _ada_call(cond8, w_ada, b_ada)
    mod = mod.reshape(L, 8, 1, N_MOD * D_MODEL)

    kc_all, vc_all = _ctxkv_call(cache_ckv, cache_krope, wukv, _kr_placement())

    ssd_cw = jnp.concatenate([ssd_conv_w, jnp.zeros((L, SUBLANE - SSD_CONV, SSD_XBC), F32)], axis=1)
    cm_w = jnp.concatenate([cm_conv_w, jnp.zeros((L, 4 * SUBLANE - CM_WIDTH, CM_CH), F32)], axis=1)
    lane_pad = lambda v: jnp.concatenate(
        [v.reshape(L, 1, 2 * SSD_HEADS), jnp.zeros((L, 1, LANE - 2 * SSD_HEADS), F32)], axis=-1)
    dtb = lane_pad(ssd_dt_bias)
    alog = lane_pad(ssd_a_log)
    d_rep = jnp.repeat(ssd_d, SSD_HEAD_DIM, axis=-1)
    d0 = d_rep[:, 0:1, :]
    d1 = d_rep[:, 1:2, :]
    h0_lat = _pack_states(jnp.swapaxes(state_ssd, 0, 1))
    gmix, gq, gkv, gffn = _vec(g_mix), _vec(g_q), _vec(g_kv), _vec(g_ffn)
    scb, sng = _vec(ssd_conv_b), _vec(ssd_norm_g)
    ccb, clg, clb = _vec(cm_conv_b), _vec(cm_ln_g), _vec(cm_ln_b)
    gfin = g_final.reshape(1, D_MODEL)

    streams = [
        dict(x=x_prompt.reshape(N_CTX, D_MODEL), nb=BATCH, t=SEQ, tmod=N_CTX, mod_row0=0, tq=SEQ, ngroups=2,
             rope=None),
        dict(x=x_sample.reshape(N_LAT, D_MODEL), nb=DEC_BATCH, t=DEC_SEQ, tmod=DEC_SEQ, mod_row0=1, tq=TQ_LAT,
             ngroups=1, rope=rope_tabs),
    ]
    ckvs, krs, hss = [], [], []
    for l in range(L):
        for si, s in enumerate(streams):
            lat = si == 1
            nb, t = s["nb"], s["t"]
            q, k, v, ckv, kr, z, xbc, dt, cm = _inproj_call(
                l, s["x"], mod, gmix, win, gq, wuq, gkv, wukv, s["rope"],
                t=t, tmod=s["tmod"], mod_row0=s["mod_row0"])
            attn = _attn_call(l, q, k, v, kc_all if lat else None, vc_all if lat else None,
                              nb=nb, t=t, tq=s["tq"], ngroups=s["ngroups"])
            ssm, hf = _ssd_call(l, z, xbc, dt, ssd_cw, scb, dtb, alog, d0, d1, sng,
                                h0_lat if lat else None, nb=nb, t=t)
            conv = _cm_call(l, cm, cm_w, ccb, clg, clb, nb=nb, t=t)
            s["x"] = _ffn_call(l, s["x"], attn, ssm, conv, mod, wout, gffn, wg, wu, wd, gfin,
                               tmod=s["tmod"], mod_row0=s["mod_row0"], final=(l == L - 1))
            if not lat:
                ckvs.append(ckv.reshape(BATCH, SEQ, KV_RANK))
                krs.append(kr[:, QK_NOPE:QK_HEAD].reshape(BATCH, SEQ, QK_ROPE))
                hss.append(_unpack_states(hf))

    y_prompt = streams[0]["x"].reshape(BATCH, SEQ, D_MODEL)
    y_sample = streams[1]["x"].reshape(DEC_BATCH, DEC_SEQ, D_MODEL)
    return (y_prompt, y_sample, jnp.stack(ckvs, axis=1), jnp.stack(krs, axis=1), jnp.stack(hss, axis=1))
```

```python
import functools

import numpy as np

import jax
import jax.numpy as jnp
from jax import lax
from jax.experimental import pallas as pl
from jax.experimental.pallas import tpu as pltpu

F32 = jnp.float32
BF16 = jnp.bfloat16

D_MODEL = 1024
BATCH = 16
SEQ = 256
DEPTH = 4
DEC_BATCH = 4
DEC_SEQ = 2048
PAST_LEN = 256
GRID_W = 64

MLA_HEADS = 8
QK_NOPE = 64
QK_ROPE = 32
QK_HEAD = QK_NOPE + QK_ROPE
V_HEAD = 64
Q_RANK = 256
KV_RANK = 128
ROPE_THETA = 10000.0
LOG2_E = 1.4426950408889634

SSD_HEADS = 4
SSD_HEAD_DIM = 64
SSD_INNER = SSD_HEADS * SSD_HEAD_DIM
SSD_GROUPS = 2
SSD_STATE = 64
SSD_CONV = 5
SSD_CHUNK = 128
SSD_XBC = SSD_INNER + 2 * SSD_GROUPS * SSD_STATE

CM_CH = 256
CM_WIDTH = 31

D_FF = 2816
N_MOD = 6
IN_WIDTH = (Q_RANK + KV_RANK + QK_ROPE) + (SSD_INNER + SSD_XBC + 2 * SSD_HEADS) + 2 * CM_CH

N_CTX = BATCH * SEQ
N_LAT = DEC_BATCH * DEC_SEQ

SUBLANE = 8
LANE = 128
MXU_N = 256
HEAD_PAD = 128
HEADS_PER_STEP = MXU_N // V_HEAD
VMEM_LIMIT = 48 * 1024 * 1024

TM_IN = 512
TM_FFN = 512
TQ_LAT = 512
FF_CHUNK = 256
CONV_ROWS = 128
CM_ROWS = 64
SSD_CONV_ROWS = 32
ATTN_ROW_PARTS = 2

C_QLAT = 0
C_CKV = C_QLAT + Q_RANK
C_DT = C_CKV + KV_RANK
C_KR = C_DT + LANE
C_Z = C_KR + 2 * HEAD_PAD
C_XBC = C_Z + SSD_INNER
C_CM = C_XBC + SSD_XBC
IN_ARR = C_CM + 2 * CM_CH
IN_PAD = -(-IN_WIDTH // LANE) * LANE


def _params(sem):
    return pltpu.CompilerParams(dimension_semantics=sem, vmem_limit_bytes=VMEM_LIMIT)


def _dot(a, b):
    return jnp.dot(a, b, preferred_element_type=F32)


def _dot_nt(a, b):
    return lax.dot_general(a, b, (((1,), (1,)), ((), ())), preferred_element_type=F32)


def _rms(x, eps=1e-6):
    return x * lax.rsqrt(jnp.mean(x * x, axis=-1, keepdims=True) + eps)


def _silu(x):
    return x * jax.nn.sigmoid(x)


def _layer_spec(shape, l, single_buffer=False):
    nd = len(shape)
    kw = dict(pipeline_mode=pl.Buffered(1)) if single_buffer else {}
    return pl.BlockSpec((None,) + tuple(shape), lambda *_: (l,) + (0,) * nd, **kw)


def _vec(v):
    return v.reshape(DEPTH, 1, -1)


def _place_kernel(w_ref, p_ref, o_ref):
    o_ref[0] = _dot(w_ref[0].astype(BF16), p_ref[...]).astype(BF16)


def _place_call(w, src, name):
    L, rows, cin = w.shape
    cout = src.shape[0]
    place = (jnp.arange(cin, dtype=jnp.int32)[:, None] == jnp.asarray(src, jnp.int32)[None, :]).astype(BF16)
    tr = min(rows, 256)
    return pl.pallas_call(
        _place_kernel,
        out_shape=jax.ShapeDtypeStruct((L, rows, cout), BF16),
        grid=(L, rows // tr),
        in_specs=[
            pl.BlockSpec((1, tr, cin), lambda l, i: (l, i, 0)),
            pl.BlockSpec((cin, cout), lambda l, i: (0, 0)),
        ],
        out_specs=pl.BlockSpec((1, tr, cout), lambda l, i: (l, i, 0)),
        compiler_params=_params(("parallel", "parallel")),
        name=name,
    )(w, place)


def _swap32():
    return np.arange(QK_ROPE).reshape(2, 2, QK_ROPE // 4)[:, ::-1, :].reshape(-1)


def _src_in():
    kr0 = Q_RANK + KV_RANK
    off_ssd = kr0 + QK_ROPE
    off_dt = off_ssd + SSD_INNER + SSD_XBC
    off_cm = off_dt + 2 * SSD_HEADS
    src = -np.ones((IN_ARR,), np.int64)
    src[0:C_DT] = np.arange(0, kr0)
    src[C_DT:C_DT + 2 * SSD_HEADS] = np.arange(off_dt, off_cm)
    src[C_KR + QK_NOPE:C_KR + QK_HEAD] = kr0 + np.arange(QK_ROPE)
    src[C_KR + HEAD_PAD + QK_NOPE:C_KR + HEAD_PAD + QK_HEAD] = kr0 + _swap32()
    src[C_Z:C_CM] = np.arange(off_ssd, off_dt)
    src[C_CM:IN_ARR] = np.arange(off_cm, IN_WIDTH)
    return src


def _src_uq():
    half = MLA_HEADS * HEAD_PAD
    src = -np.ones((2 * half,), np.int64)
    for h in range(MLA_HEADS):
        src[h * HEAD_PAD:h * HEAD_PAD + QK_HEAD] = h * QK_HEAD + np.arange(QK_HEAD)
        lo = half + h * HEAD_PAD + QK_NOPE
        src[lo:lo + QK_ROPE] = h * QK_HEAD + QK_NOPE + _swap32()
    return src


def _src_ukv():
    half = MLA_HEADS * HEAD_PAD
    src = -np.ones((half + MLA_HEADS * V_HEAD,), np.int64)
    for h in range(MLA_HEADS):
        base = h * (QK_NOPE + V_HEAD)
        src[h * HEAD_PAD:h * HEAD_PAD + QK_NOPE] = base + np.arange(QK_NOPE)
        src[half + h * V_HEAD:half + (h + 1) * V_HEAD] = base + QK_NOPE + np.arange(V_HEAD)
    return src


def _ada_kernel(cond_ref, w_ref, b_ref, o_ref):
    c = cond_ref[...]
    a = _silu(c).astype(BF16)
    o_ref[0] = _dot(a, w_ref[0].astype(BF16)) + b_ref[0]


def _ada_call(cond8, w_ada, b_ada):
    tn = 1536
    nt = (N_MOD * D_MODEL) // tn
    return pl.pallas_call(
        _ada_kernel,
        out_shape=jax.ShapeDtypeStruct((DEPTH, 8, N_MOD * D_MODEL), F32),
        grid=(DEPTH, nt),
        in_specs=[
            pl.BlockSpec((8, D_MODEL), lambda l, j: (0, 0)),
            pl.BlockSpec((1, D_MODEL, tn), lambda l, j: (l, 0, j)),
            pl.BlockSpec((1, 1, tn), lambda l, j: (l, 0, j)),
        ],
        out_specs=pl.BlockSpec((1, 8, tn), lambda l, j: (l, 0, j)),
        compiler_params=_params(("parallel", "parallel")),
        name="ada_mod",
    )(cond8, w_ada, b_ada.reshape(DEPTH, 1, N_MOD * D_MODEL))


def _mod_spec(l, tm, tmod, mod_row0):
    return pl.BlockSpec((None, 1, 1, N_MOD * D_MODEL), lambda i: (l, mod_row0 + (i * tm) // tmod, 0, 0))


def _inproj_kernel(*refs, rope):
    if rope:
        (x_ref, mod_ref, gmix_ref, win_ref, gq_ref, wuq_ref, gkv_ref, wukv_ref, cos_ref, sin_ref,
         q_out, k_out, v_out, ckv_out, kr_out, z_out, xbc_out, dt_out, cm_out) = refs
        cos = cos_ref[...]
        sin = sin_ref[...]
    else:
        (x_ref, mod_ref, gmix_ref, win_ref, gq_ref, wuq_ref, gkv_ref, wukv_ref,
         q_out, k_out, v_out, ckv_out, kr_out, z_out, xbc_out, dt_out, cm_out) = refs
    x = x_ref[...]
    sh1 = mod_ref[0, :, 0:D_MODEL]
    sc1 = mod_ref[0, :, D_MODEL:2 * D_MODEL]
    h = (_rms(x) * gmix_ref[...] * (1.0 + sc1) + sh1).astype(BF16)

    q_lat = _dot(h, win_ref[:, C_QLAT:C_QLAT + Q_RANK])
    qn = (_rms(q_lat) * gq_ref[...]).astype(BF16)
    qq = _dot(qn, wuq_ref[...])
    scale = QK_HEAD ** -0.5 * LOG2_E
    half = MLA_HEADS * HEAD_PAD
    for hd in range(MLA_HEADS):
        lo = hd * HEAD_PAD
        qh = qq[:, lo:lo + HEAD_PAD]
        if rope:
            qh = qh * cos + qq[:, half + lo:half + lo + HEAD_PAD] * sin
        q_out[:, lo:lo + HEAD_PAD] = (qh * scale).astype(BF16)

    ncol = C_KR - C_CKV + (2 if rope else 1) * HEAD_PAD
    mid = _dot(h, win_ref[:, C_CKV:C_CKV + ncol])
    ckv = _rms(mid[:, 0:KV_RANK]) * gkv_ref[...]
    ckv_out[...] = ckv
    dt_out[...] = mid[:, C_DT - C_CKV:C_DT - C_CKV + LANE]
    kr = mid[:, C_KR - C_CKV:C_KR - C_CKV + HEAD_PAD]
    kr_out[...] = kr
    kr_rot = kr
    if rope:
        kr_rot = kr * cos + mid[:, C_KR - C_CKV + HEAD_PAD:C_KR - C_CKV + 2 * HEAD_PAD] * sin
    kv = _dot(ckv.astype(BF16), wukv_ref[...])
    for hd in range(MLA_HEADS):
        lo = hd * HEAD_PAD
        k_out[:, lo:lo + HEAD_PAD] = (kv[:, lo:lo + HEAD_PAD] + kr_rot).astype(BF16)
    v_out[...] = kv[:, half:half + MLA_HEADS * V_HEAD].astype(BF16)

    z_out[...] = _dot(h, win_ref[:, C_Z:C_Z + SSD_INNER])
    xbc_out[...] = _dot(h, win_ref[:, C_XBC:C_XBC + SSD_XBC])
    cm_out[...] = _dot(h, win_ref[:, C_CM:C_CM + 2 * CM_CH])


def _inproj_call(l, x, mod, gmix, win, gq, wuq, gkv, wukv, rope_tabs, *, t, tmod, mod_row0):
    n = x.shape[0]
    tm = TM_IN
    rope = rope_tabs is not None
    half = MLA_HEADS * HEAD_PAD

    def tok(i):
        return (i, 0)

    outs = [
        ((n, half), BF16),
        ((n, half), BF16),
        ((n, MLA_HEADS * V_HEAD), BF16),
        ((n, KV_RANK), F32),
        ((n, HEAD_PAD), F32),
        ((n, SSD_INNER), F32),
        ((n, SSD_XBC), F32),
        ((n, LANE), F32),
        ((n, 2 * CM_CH), F32),
    ]
    in_specs = [
        pl.BlockSpec((tm, D_MODEL), tok),
        _mod_spec(l, tm, tmod, mod_row0),
        _layer_spec((1, D_MODEL), l),
        _layer_spec((D_MODEL, IN_ARR), l, single_buffer=True),
        _layer_spec((1, Q_RANK), l),
        _layer_spec((Q_RANK, 2 * half if rope else half), l, single_buffer=True),
        _layer_spec((1, KV_RANK), l),
        _layer_spec((KV_RANK, MLA_HEADS * (HEAD_PAD + V_HEAD)), l, single_buffer=True),
    ]
    args = [x, mod, gmix, win, gq, wuq, gkv, wukv]
    if rope:
        pos = lambda i: (i % (t // tm), 0)
        in_specs += [pl.BlockSpec((tm, HEAD_PAD), pos), pl.BlockSpec((tm, HEAD_PAD), pos)]
        args += list(rope_tabs)
    return pl.pallas_call(
        functools.partial(_inproj_kernel, rope=rope),
        out_shape=[jax.ShapeDtypeStruct(s, d) for s, d in outs],
        grid=(n // tm,),
        in_specs=in_specs,
        out_specs=[pl.BlockSpec((tm, s[1]), tok) for s, _ in outs],
        compiler_params=_params(("parallel",)),
        name="in_proj_lat" if rope else "in_proj_ctx",
    )(*args)


def _ctxkv_kernel(ckv_ref, kr_ref, wukv_ref, place_ref, k_out, v_out):
    kv = _dot(ckv_ref[0, 0].astype(BF16), wukv_ref[0])
    krp = _dot(kr_ref[0, 0].astype(BF16), place_ref[...])
    half = MLA_HEADS * HEAD_PAD
    k_out[0, 0] = (kv[:, 0:half] + krp).astype(BF16)
    v_out[0, 0] = kv[:, half:half + MLA_HEADS * V_HEAD].astype(BF16)


def _ctxkv_call(cache_ckv, cache_krope, wukv_all, place):
    return pl.pallas_call(
        _ctxkv_kernel,
        out_shape=[
            jax.ShapeDtypeStruct((DEPTH, DEC_BATCH, PAST_LEN, MLA_HEADS * HEAD_PAD), BF16),
            jax.ShapeDtypeStruct((DEPTH, DEC_BATCH, PAST_LEN, MLA_HEADS * V_HEAD), BF16),
        ],
        grid=(DEPTH, DEC_BATCH),
        in_specs=[
            pl.BlockSpec((1, 1, PAST_LEN, KV_RANK), lambda l, b: (b, l, 0, 0)),
            pl.BlockSpec((1, 1, PAST_LEN, QK_ROPE), lambda l, b: (b, l, 0, 0)),
            pl.BlockSpec((1, KV_RANK, MLA_HEADS * (HEAD_PAD + V_HEAD)), lambda l, b: (l, 0, 0)),
            pl.BlockSpec((QK_ROPE, MLA_HEADS * HEAD_PAD), lambda l, b: (0, 0)),
        ],
        out_specs=[
            pl.BlockSpec((1, 1, PAST_LEN, MLA_HEADS * HEAD_PAD), lambda l, b: (l, b, 0, 0)),
            pl.BlockSpec((1, 1, PAST_LEN, MLA_HEADS * V_HEAD), lambda l, b: (l, b, 0, 0)),
        ],
        compiler_params=_params(("parallel", "parallel")),
        name="ctx_kv",
    )(cache_ckv, cache_krope, wukv_all, place)


def _attn_kernel(*refs, has_ctx, ngroups, nparts):
    if has_ctx:
        q_ref, k_ref, v_ref, kc_ref, vc_ref, o_ref = refs
    else:
        q_ref, k_ref, v_ref, o_ref = refs
    gw = HEADS_PER_STEP * V_HEAD
    tq = q_ref.shape[0]
    rp = tq // nparts
    units = [(grp, part, hh) for grp in range(ngroups) for part in range(nparts) for hh in range(HEADS_PER_STEP)]

    def scores(u):
        grp, part, hh = u
        lo = (grp * HEADS_PER_STEP + hh) * HEAD_PAD
        qh = q_ref[part * rp:(part + 1) * rp, lo:lo + HEAD_PAD]
        s = _dot_nt(qh, k_ref[:, lo:lo + HEAD_PAD])
        sc = _dot_nt(qh, kc_ref[0, :, lo:lo + HEAD_PAD]) if has_ctx else None
        return s, sc

    def probs(s, sc):
        m = jnp.max(s, axis=-1, keepdims=True)
        if has_ctx:
            m = jnp.maximum(m, jnp.max(sc, axis=-1, keepdims=True))
        p = jnp.exp2(s - m)
        den = jnp.sum(p, axis=-1, keepdims=True)
        pc = None
        if has_ctx:
            pc = jnp.exp2(sc - m)
            den = den + jnp.sum(pc, axis=-1, keepdims=True)
            pc = pc.astype(BF16)
        return p.astype(BF16), pc, den

    def values(u, p, pc, den, acc):
        grp, part, hh = u
        o = _dot(p, v_ref[:, grp * gw:(grp + 1) * gw])
        if has_ctx:
            o = o + _dot(pc, vc_ref[0, :, grp * gw:(grp + 1) * gw])
        if hh == 0:
            return o, jnp.broadcast_to(den, o.shape)
        mine = lax.broadcasted_iota(jnp.int32, o.shape, 1) >= hh * V_HEAD
        return jnp.where(mine, o, acc[0]), jnp.where(mine, den, acc[1])

    sc_next = scores(units[0])
    acc = None
    for i, u in enumerate(units):
        sc_cur = sc_next
        if i + 1 < len(units):
            sc_next = scores(units[i + 1])
        acc = values(u, *probs(*sc_cur), acc)
        grp, part, hh = u
        if hh == HEADS_PER_STEP - 1:
            o_ref[part * rp:(part + 1) * rp, grp * gw:(grp + 1) * gw] = (acc[0] / acc[1]).astype(BF16)


def _attn_call(l, q, k, v, kc, vc, *, nb, t, tq, ngroups):
    has_ctx = kc is not None
    nq = t // tq
    nh = ngroups * HEADS_PER_STEP
    nsteps = MLA_HEADS // nh
    in_specs = [
        pl.BlockSpec((tq, nh * HEAD_PAD), lambda b, g, qi: (b * nq + qi, g)),
        pl.BlockSpec((t, nh * HEAD_PAD), lambda b, g, qi: (b, g)),
        pl.BlockSpec((t, nh * V_HEAD), lambda b, g, qi: (b, g)),
    ]
    args = [q, k, v]
    if has_ctx:
        in_specs += [
            pl.BlockSpec((None, 1, PAST_LEN, nh * HEAD_PAD), lambda b, g, qi: (l, b, 0, g)),
            pl.BlockSpec((None, 1, PAST_LEN, nh * V_HEAD), lambda b, g, qi: (l, b, 0, g)),
        ]
        args += [kc, vc]
    return pl.pallas_call(
        functools.partial(_attn_kernel, has_ctx=has_ctx, ngroups=ngroups, nparts=ATTN_ROW_PARTS if has_ctx else 1),
        out_shape=jax.ShapeDtypeStruct((nb * t, MLA_HEADS * V_HEAD), BF16),
        grid=(nb, nsteps, nq),
        in_specs=in_specs,
        out_specs=pl.BlockSpec((tq, nh * V_HEAD), lambda b, g, qi: (b * nq + qi, g)),
        compiler_params=_params(("parallel", "parallel", "arbitrary")),
        name="attn_lat" if has_ctx else "attn_ctx",
    )(*args)


def _split3(x):
    hi = x.astype(BF16)
    r = x - hi.astype(F32)
    mid = r.astype(BF16)
    lo = (r - mid.astype(F32)).astype(BF16)
    return hi, mid, lo


def _ssd_kernel(*refs, t, has_h0):
    if has_h0:
        (z_ref, xbc_ref, dt_ref, cw_ref, cb_ref, dtb_ref, alog_ref, d0_ref, d1_ref, gn_ref, h0_ref,
         y_out, hf_out, xpad, xc, dtp_s, acs_s, acst_s, ysc, ssc, dec_s, eacs_s, hin_s, st) = refs
    else:
        (z_ref, xbc_ref, dt_ref, cw_ref, cb_ref, dtb_ref, alog_ref, d0_ref, d1_ref, gn_ref,
         y_out, hf_out, xpad, xc, dtp_s, acs_s, acst_s, ysc, ssc, dec_s, eacs_s, hin_s, st) = refs
    L = SSD_CHUNK
    W = SSD_INNER
    nc = t // L
    halo = SUBLANE

    xpad[0:halo, :] = jnp.zeros((halo, SSD_XBC), F32)
    xpad[halo + t:halo + t + halo, :] = jnp.zeros((halo, SSD_XBC), F32)
    xpad[halo:halo + t, :] = xbc_ref[...]
    cw = cw_ref[...]
    cb = cb_ref[...]
    rc = SSD_CONV_ROWS
    nwin = rc + 2 * halo

    def conv(blk, carry):
        base = pl.multiple_of(blk * rc, rc)
        win = xpad[pl.ds(base, nwin), :]
        acc = cb + cw[SSD_CONV // 2:SSD_CONV // 2 + 1, :] * win[halo:halo + rc, :]
        for kk in range(SSD_CONV):
            if kk != SSD_CONV // 2:
                first = halo - SSD_CONV // 2 + kk
                acc = acc + cw[kk:kk + 1, :] * pltpu.roll(win, nwin - first, axis=0)[0:rc, :]
        xc[pl.ds(base, rc), :] = _silu(acc)
        return carry

    lax.fori_loop(0, t // rc, conv, 0, unroll=2)
    r = CONV_ROWS

    a_all = -jnp.exp(alog_ref[...])
    dtb = dtb_ref[...]
    row_i = lax.broadcasted_iota(jnp.int32, (L, L), 0)
    col_i = lax.broadcasted_iota(jnp.int32, (L, L), 1)
    tri = [(row_i >= col_i), (row_i <= col_i)]
    tri_bf = [jnp.where(m, 1.0, 0.0).astype(BF16) for m in tri]
    grp_rows = [row_i < SSD_STATE, row_i >= SSD_STATE]
    lane_w = lax.broadcasted_iota(jnp.int32, (L, W), 1)
    own_block = (lax.broadcasted_iota(jnp.int32, (L, W), 0) // SSD_STATE) == (lane_w // LANE)
    sp_rows = lax.broadcasted_iota(jnp.int32, (3 * LANE, 2 * W), 0) % LANE
    sp_cols = lax.broadcasted_iota(jnp.int32, (3 * LANE, 2 * W), 1) // SSD_HEAD_DIM
    spread = jnp.where(sp_rows == sp_cols, 1.0, 0.0).astype(BF16)

    def chunk_rows(c):
        return pl.ds(pl.multiple_of(c * L, L), L)

    lane_lo = col_i < SSD_HEAD_DIM

    def columns(m, d):
        return [jnp.broadcast_to(m[:, j:j + 1], (L, LANE)) for j in range(d * SSD_HEADS, (d + 1) * SSD_HEADS)]

    def expand(cols):
        return jnp.concatenate([jnp.where(lane_lo, cols[0], cols[1]), jnp.where(lane_lo, cols[2], cols[3])],
                               axis=1)

    def pre(c, carry):
        rows = chunk_rows(c)
        dtp = jax.nn.softplus(dt_ref[rows, :] + dtb)
        dtp_s[rows, :] = dtp
        x3 = jnp.concatenate(_split3(dtp * a_all), axis=1)
        rf = _dot(tri_bf[0], x3)
        rb = _dot(tri_bf[1], x3)
        acs_f = rf[:, 0:LANE] + rf[:, LANE:2 * LANE] + rf[:, 2 * LANE:3 * LANE]
        acs_b = rb[:, 0:LANE] + rb[:, LANE:2 * LANE] + rb[:, 2 * LANE:3 * LANE]
        acs = jnp.where(col_i < SSD_HEADS, acs_f, acs_b)
        acs_s[rows, :] = acs
        acst_s[pl.ds(pl.multiple_of(c * SUBLANE, SUBLANE), SUBLANE), :] = acs.T[0:SUBLANE, :]
        return carry

    lax.fori_loop(0, nc, pre, 0, unroll=2)

    def main(c, carry):
        rows = chunk_rows(c)
        xs = xc[rows, 0:W]
        cm = xc[rows, W + LANE:W + 2 * LANE].astype(BF16)
        bt = xc[rows, W:W + LANE].T
        btm2 = jnp.concatenate([jnp.where(grp_rows[0], bt, 0.0), jnp.where(grp_rows[1], bt, 0.0)],
                               axis=1).astype(BF16)
        g2 = _dot(cm, btm2)
        bt_bf = bt.astype(BF16)
        dtp = dtp_s[rows, :]
        acs = acs_s[rows, :]
        acst = acst_s[pl.ds(pl.multiple_of(c * SUBLANE, SUBLANE), SUBLANE), :]
        dt_wide = _dot(jnp.concatenate(_split3(dtp), axis=1), spread)
        ysum = None
        for d in range(2):
            acs_cols = columns(acs, d)
            acs_b = expand(acs_cols)
            eacs_s[c, d] = jnp.exp(acs_b)
            xdt = xs * dt_wide[:, d * W:(d + 1) * W]
            xdt_bf = xdt.astype(BF16)
            yd = None
            for h in range(SSD_HEADS):
                j = d * SSD_HEADS + h
                g = h // (SSD_HEADS // SSD_GROUPS)
                diff = acs_cols[h] - acst[j:j + 1, :]
                seg = jnp.exp(jnp.where(tri[d], diff, -jnp.inf))
                res = _dot((g2[:, g * L:(g + 1) * L] * seg).astype(BF16), xdt_bf)
                yd = res if h == 0 else jnp.where(lane_w >= h * SSD_HEAD_DIM, res, yd)
            ysum = yd if d == 0 else ysum + yd
            tot = acs_b[L - 1:L, :] if d == 0 else acs_b[0:1, :]
            s_new = _dot(bt_bf, (xdt * jnp.exp(tot - acs_b)).astype(BF16))
            ssc[c, d] = jnp.where(own_block, s_new, 0.0)
            dec_s[c, d] = jnp.broadcast_to(jnp.exp(tot), (SUBLANE, W))
        ysc[rows, :] = ysum
        return carry

    lax.fori_loop(0, nc, main, 0, unroll=2)

    if has_h0:
        st[...] = h0_ref[0]
    else:
        st[...] = jnp.zeros(st.shape, F32)

    def rec(i, carry):
        for d, c in ((0, i), (1, nc - 1 - i)):
            h_in = st[d]
            hin_s[c, d] = h_in.astype(BF16)
            st[d] = dec_s[c, d][0:1, :] * h_in + ssc[c, d]
        return carry

    lax.fori_loop(0, nc, rec, 0)
    for d in range(2):
        for g in range(SSD_GROUPS):
            hf_out[0, d, g] = st[d, g * SSD_STATE:(g + 1) * SSD_STATE, g * LANE:(g + 1) * LANE]

    d0 = d0_ref[...]
    d1 = d1_ref[...]
    gn = gn_ref[...]

    def fin(c, carry):
        rows = chunk_rows(c)
        xs = xc[rows, 0:W]
        cm = xc[rows, W + LANE:W + 2 * LANE].astype(BF16)
        y = ysc[rows, :] + d0 * xs + d1 * xs
        for d in range(2):
            y = y + _dot(cm, hin_s[c, d]) * eacs_s[c, d]
        y = y * _silu(z_ref[rows, :])
        y_out[rows, :] = (_rms(y) * gn).astype(BF16)
        return carry

    lax.fori_loop(0, nc, fin, 0, unroll=2)


def _ssd_call(l, z, xbc, dt, cw, cb, dtb, alog, d0, d1, gn, h0, *, nb, t):
    has_h0 = h0 is not None
    nc = t // SSD_CHUNK

    def tok(b):
        return (b, 0)

    in_specs = [
        pl.BlockSpec((t, SSD_INNER), tok),
        pl.BlockSpec((t, SSD_XBC), tok),
        pl.BlockSpec((t, LANE), tok),
        _layer_spec((SUBLANE, SSD_XBC), l),
        _layer_spec((1, SSD_XBC), l),
        _layer_spec((1, LANE), l),
        _layer_spec((1, LANE), l),
        _layer_spec((1, SSD_INNER), l),
        _layer_spec((1, SSD_INNER), l),
        _layer_spec((1, SSD_INNER), l),
    ]
    args = [z, xbc, dt, cw, cb, dtb, alog, d0, d1, gn]
    if has_h0:
        in_specs.append(pl.BlockSpec((None, 1, 2, LANE, SSD_INNER), lambda b: (l, b, 0, 0, 0)))
        args.append(h0)
    return pl.pallas_call(
        functools.partial(_ssd_kernel, t=t, has_h0=has_h0),
        out_shape=[
            jax.ShapeDtypeStruct((nb * t, SSD_INNER), BF16),
            jax.ShapeDtypeStruct((nb, 2, SSD_GROUPS, SSD_STATE, LANE), F32),
        ],
        grid=(nb,),
        in_specs=in_specs,
        out_specs=[
            pl.BlockSpec((t, SSD_INNER), tok),
            pl.BlockSpec((1, 2, SSD_GROUPS, SSD_STATE, LANE), lambda b: (b, 0, 0, 0, 0)),
        ],
        scratch_shapes=[
            pltpu.VMEM((t + 2 * SUBLANE, SSD_XBC), F32),
            pltpu.VMEM((t, SSD_XBC), F32),
            pltpu.VMEM((t, LANE), F32),
            pltpu.VMEM((t, LANE), F32),
            pltpu.VMEM((nc * SUBLANE, LANE), F32),
            pltpu.VMEM((t, SSD_INNER), F32),
            pltpu.VMEM((nc, 2, LANE, SSD_INNER), F32),
            pltpu.VMEM((nc, 2, SUBLANE, SSD_INNER), F32),
            pltpu.VMEM((nc, 2, LANE, SSD_INNER), F32),
            pltpu.VMEM((nc, 2, LANE, SSD_INNER), BF16),
            pltpu.VMEM((2, LANE, SSD_INNER), F32),
        ],
        compiler_params=_params(("parallel",)),
        name="ssd_lat" if has_h0 else "ssd_ctx",
    )(*args)


def _cm_kernel(cm_ref, w_ref, b_ref, lg_ref, lb_ref, o_ref, gpad, *, t):
    halo = 16
    gpad[0:halo, :] = jnp.zeros((halo, CM_CH), F32)
    gpad[halo + t:halo + t + halo, :] = jnp.zeros((halo, CM_CH), F32)
    r = CONV_ROWS
    for blk in range(t // r):
        rows = slice(blk * r, (blk + 1) * r)
        gpad[halo + blk * r:halo + (blk + 1) * r, :] = (
            cm_ref[rows, 0:CM_CH] * jax.nn.sigmoid(cm_ref[rows, CM_CH:2 * CM_CH]))
    w = w_ref[...]
    bias = b_ref[...]
    lg = lg_ref[...]
    lb = lb_ref[...]
    rb = CM_ROWS
    sub = SUBLANE
    assert halo - CM_WIDTH // 2 == 1 and CM_WIDTH <= 4 * sub

    def body(blk, carry):
        base = pl.multiple_of(blk * rb, rb)
        acc = bias
        for j in range(sub):
            pj = None
            for a in range(4):
                kk = sub * a + j
                if kk < CM_WIDTH:
                    term = w[kk:kk + 1, :] * gpad[pl.ds(base + sub * a, rb + sub), :]
                    pj = term if pj is None else pj + term
            acc = acc + pj[j + 1:j + 1 + rb, :]
        mu = jnp.mean(acc, axis=-1, keepdims=True)
        xc = acc - mu
        var = jnp.mean(xc * xc, axis=-1, keepdims=True)
        y = xc * lax.rsqrt(var + 1e-5) * lg + lb
        o_ref[pl.ds(base, rb), :] = _silu(y).astype(BF16)
        return carry

    lax.fori_loop(0, t // rb, body, 0, unroll=2)


def _cm_call(l, cm, w, b, lg, lb, *, nb, t):
    return pl.pallas_call(
        functools.partial(_cm_kernel, t=t),
        out_shape=jax.ShapeDtypeStruct((nb * t, CM_CH), BF16),
        grid=(nb,),
        in_specs=[
            pl.BlockSpec((t, 2 * CM_CH), lambda i: (i, 0)),
            _layer_spec((4 * SUBLANE, CM_CH), l),
            _layer_spec((1, CM_CH), l),
            _layer_spec((1, CM_CH), l),
            _layer_spec((1, CM_CH), l),
        ],
        out_specs=pl.BlockSpec((t, CM_CH), lambda i: (i, 0)),
        scratch_shapes=[pltpu.VMEM((t + 32, CM_CH), F32)],
        compiler_params=_params(("parallel",)),
        name="conv_module",
    )(cm, w, b, lg, lb)


def _ffn_kernel(x_ref, attn_ref, ssm_ref, conv_ref, mod_ref, wout_ref, gffn_ref, wg_ref, wu_ref, wd_ref,
                gfin_ref, o_ref, *, final):
    n_attn = MLA_HEADS * V_HEAD
    mixed = _dot(attn_ref[...], wout_ref[0:n_attn, :])
    mixed = mixed + _dot(ssm_ref[...], wout_ref[n_attn:n_attn + SSD_INNER, :])
    mixed = mixed + _dot(conv_ref[...], wout_ref[n_attn + SSD_INNER:n_attn + SSD_INNER + CM_CH, :])
    g1 = mod_ref[0, :, 2 * D_MODEL:3 * D_MODEL]
    sh2 = mod_ref[0, :, 3 * D_MODEL:4 * D_MODEL]
    sc2 = mod_ref[0, :, 4 * D_MODEL:5 * D_MODEL]
    g2 = mod_ref[0, :, 5 * D_MODEL:6 * D_MODEL]
    x = x_ref[...] + g1 * mixed
    h = (_rms(x) * gffn_ref[...] * (1.0 + sc2) + sh2).astype(BF16)
    ff = jnp.zeros(x.shape, F32)
    for j in range(D_FF // FF_CHUNK):
        cols = slice(j * FF_CHUNK, (j + 1) * FF_CHUNK)
        gate = _dot(h, wg_ref[:, cols])
        up = _dot(h, wu_ref[:, cols])
        ff = ff + _dot((_silu(gate) * up).astype(BF16), wd_ref[cols, :])
    x = x + g2 * ff
    if final:
        x = _rms(x) * gfin_ref[...]
    o_ref[...] = x


def _ffn_call(l, x, attn, ssm, conv, mod, wout, gffn, wg, wu, wd, gfin, *, tmod, mod_row0, final):
    n = x.shape[0]
    tm = TM_FFN

    def tok(i):
        return (i, 0)

    return pl.pallas_call(
        functools.partial(_ffn_kernel, final=final),
        out_shape=jax.ShapeDtypeStruct((n, D_MODEL), F32),
        grid=(n // tm,),
        in_specs=[
            pl.BlockSpec((tm, D_MODEL), tok),
            pl.BlockSpec((tm, MLA_HEADS * V_HEAD), tok),
            pl.BlockSpec((tm, SSD_INNER), tok),
            pl.BlockSpec((tm, CM_CH), tok),
            _mod_spec(l, tm, tmod, mod_row0),
            _layer_spec((D_MODEL, D_MODEL), l, single_buffer=True),
            _layer_spec((1, D_MODEL), l),
            _layer_spec((D_MODEL, D_FF), l, single_buffer=True),
            _layer_spec((D_MODEL, D_FF), l, single_buffer=True),
            _layer_spec((D_FF, D_MODEL), l, single_buffer=True),
            pl.BlockSpec((1, D_MODEL), lambda i: (0, 0)),
        ],
        out_specs=pl.BlockSpec((tm, D_MODEL), tok),
        compiler_params=_params(("parallel",)),
        name="out_ffn",
    )(x, attn, ssm, conv, mod, wout, gffn, wg, wu, wd, gfin)


def _rope_tables():
    rows = DEC_SEQ // GRID_W
    row = jnp.repeat(jnp.arange(rows), GRID_W).astype(F32)
    col = (jnp.arange(rows * GRID_W) % GRID_W).astype(F32)
    nf = QK_ROPE // 4
    inv = ROPE_THETA ** (-jnp.arange(nf, dtype=F32) / nf)
    ang = jnp.stack([row[:, None] * inv, col[:, None] * inv], axis=1)
    cos = jnp.cos(ang)
    sin = jnp.sin(ang)
    cos32 = jnp.stack([cos, cos], axis=2).reshape(DEC_SEQ, QK_ROPE)
    sin32 = jnp.stack([-sin, sin], axis=2).reshape(DEC_SEQ, QK_ROPE)
    ones = jnp.ones((DEC_SEQ, QK_NOPE), F32)
    pad1 = jnp.ones((DEC_SEQ, HEAD_PAD - QK_HEAD), F32)
    cos_t = jnp.concatenate([ones, cos32, pad1], axis=1)
    sin_t = jnp.concatenate([0.0 * ones, sin32, 0.0 * pad1], axis=1)
    return cos_t, sin_t


def _kr_placement():
    src = jnp.arange(QK_ROPE)
    cols = jnp.arange(MLA_HEADS * HEAD_PAD)
    hit = (cols[None, :] % HEAD_PAD) == (QK_NOPE + src[:, None])
    return hit.astype(BF16)


def _pack_states(h0):
    lead = h0.shape[:-3]
    s = jnp.moveaxis(h0, -1, -3).reshape(lead + (SSD_STATE, SSD_INNER))
    lane_group = jnp.arange(SSD_INNER) // LANE
    parts = [jnp.where(lane_group == g, s, 0.0) for g in range(SSD_GROUPS)]
    return jnp.concatenate(parts, axis=-2)


def _unpack_states(st):
    b = st.shape[0]
    s = st.reshape(b, 2, SSD_GROUPS, SSD_STATE, 2, SSD_HEAD_DIM).transpose(0, 1, 2, 4, 5, 3)
    return s.reshape(b, 2, SSD_HEADS, SSD_HEAD_DIM, SSD_STATE)


def kernel(x_prompt, x_sample, c, cache_ckv, cache_krope, state_ssd, c_ctx, w_ada, b_ada, g_mix, w_in, g_q, w_uq, g_kv, w_ukv, ssd_conv_w, ssd_conv_b, ssd_dt_bias, ssd_a_log, ssd_d, ssd_norm_g, cm_conv_w, cm_conv_b, cm_ln_g, cm_ln_b, w_out, g_ffn, w_gate, w_up, w_down, g_final):
    L = DEPTH
    w_in_pad = jnp.pad(w_in, ((0, 0), (0, 0), (0, IN_PAD - IN_WIDTH))).astype(BF16)
    win = _place_call(w_in_pad, _src_in(), "place_w_in")
    wuq = _place_call(w_uq, _src_uq(), "place_w_uq")
    wukv = _place_call(w_ukv, _src_ukv(), "place_w_ukv")
    wout = w_out.astype(BF16)
    wg = w_gate.astype(BF16)
    wu = w_up.astype(BF16)
    wd = w_down.astype(BF16)
    rope_tabs = _rope_tables()

    cond8 = jnp.concatenate([c_ctx[None, :], c, jnp.zeros((8 - 1 - DEC_BATCH, D_MODEL), F32)], axis=0)
    mod = _ada_call(cond8, w_ada, b_ada)
    mod = mod.reshape(L, 8, 1, N_MOD * D_MODEL)

    kc_all, vc_all = _ctxkv_call(cache_ckv, cache_krope, wukv, _kr_placement())

    ssd_cw = jnp.concatenate([ssd_conv_w, jnp.zeros((L, SUBLANE - SSD_CONV, SSD_XBC), F32)], axis=1)
    cm_w = jnp.concatenate([cm_conv_w, jnp.zeros((L, 4 * SUBLANE - CM_WIDTH, CM_CH), F32)], axis=1)
    lane_pad = lambda v: jnp.concatenate(
        [v.reshape(L, 1, 2 * SSD_HEADS), jnp.zeros((L, 1, LANE - 2 * SSD_HEADS), F32)], axis=-1)
    dtb = lane_pad(ssd_dt_bias)
    alog = lane_pad(ssd_a_log)
    d_rep = jnp.repeat(ssd_d, SSD_HEAD_DIM, axis=-1)
    d0 = d_rep[:, 0:1, :]
    d1 = d_rep[:, 1:2, :]
    h0_lat = _pack_states(jnp.swapaxes(state_ssd, 0, 1))
    gmix, gq, gkv, gffn = _vec(g_mix), _vec(g_q), _vec(g_kv), _vec(g_ffn)
    scb, sng = _vec(ssd_conv_b), _vec(ssd_norm_g)
    ccb, clg, clb = _vec(cm_conv_b), _vec(cm_ln_g), _vec(cm_ln_b)
    gfin = g_final.reshape(1, D_MODEL)

    streams = [
        dict(x=x_prompt.reshape(N_CTX, D_MODEL), nb=BATCH, t=SEQ, tmod=N_CTX, mod_row0=0, tq=SEQ, ngroups=2,
             rope=None),
        dict(x=x_sample.reshape(N_LAT, D_MODEL), nb=DEC_BATCH, t=DEC_SEQ, tmod=DEC_SEQ, mod_row0=1, tq=TQ_LAT,
             ngroups=1, rope=rope_tabs),
    ]
    ckvs, krs, hss = [], [], []
    for l in range(L):
        for si, s in enumerate(streams):
            lat = si == 1
            nb, t = s["nb"], s["t"]
            q, k, v, ckv, kr, z, xbc, dt, cm = _inproj_call(
                l, s["x"], mod, gmix, win, gq, wuq, gkv, wukv, s["rope"],
                t=t, tmod=s["tmod"], mod_row0=s["mod_row0"])
            attn = _attn_call(l, q, k, v, kc_all if lat else None, vc_all if lat else None,
                              nb=nb, t=t, tq=s["tq"], ngroups=s["ngroups"])
            ssm, hf = _ssd_call(l, z, xbc, dt, ssd_cw, scb, dtb, alog, d0, d1, sng,
                                h0_lat if lat else None, nb=nb, t=t)
            conv = _cm_call(l, cm, cm_w, ccb, clg, clb, nb=nb, t=t)
            s["x"] = _ffn_call(l, s["x"], attn, ssm, conv, mod, wout, gffn, wg, wu, wd, gfin,
                               tmod=s["tmod"], mod_row0=s["mod_row0"], final=(l == L - 1))
            if not lat:
                ckvs.append(ckv.reshape(BATCH, SEQ, KV_RANK))
                krs.append(kr[:, QK_NOPE:QK_HEAD].reshape(BATCH, SEQ, QK_ROPE))
                hss.append(_unpack_states(hf))

    y_prompt = streams[0]["x"].reshape(BATCH, SEQ, D_MODEL)
    y_sample = streams[1]["x"].reshape(DEC_BATCH, DEC_SEQ, D_MODEL)
    return (y_prompt, y_sample, jnp.stack(ckvs, axis=1), jnp.stack(krs, axis=1), jnp.stack(hss, axis=1))
```

```python
import functools

import numpy as np

import jax
import jax.numpy as jnp
from jax import lax
from jax.experimental import pallas as pl
from jax.experimental.pallas import tpu as pltpu

F32 = jnp.float32
BF16 = jnp.bfloat16

D_MODEL = 1024
BATCH = 16
SEQ = 256
DEPTH = 4
DEC_BATCH = 4
DEC_SEQ = 2048
PAST_LEN = 256
GRID_W = 64

MLA_HEADS = 8
QK_NOPE = 64
QK_ROPE = 32
QK_HEAD = QK_NOPE + QK_ROPE
V_HEAD = 64
Q_RANK = 256
KV_RANK = 128
ROPE_THETA = 10000.0
LOG2_E = 1.4426950408889634

SSD_HEADS = 4
SSD_HEAD_DIM = 64
SSD_INNER = SSD_HEADS * SSD_HEAD_DIM
SSD_GROUPS = 2
SSD_STATE = 64
SSD_CONV = 5
SSD_CHUNK = 128
SSD_XBC = SSD_INNER + 2 * SSD_GROUPS * SSD_STATE

CM_CH = 256
CM_WIDTH = 31

D_FF = 2816
N_MOD = 6
IN_WIDTH = (Q_RANK + KV_RANK + QK_ROPE) + (SSD_INNER + SSD_XBC + 2 * SSD_HEADS) + 2 * CM_CH

N_CTX = BATCH * SEQ
N_LAT = DEC_BATCH * DEC_SEQ

SUBLANE = 8
LANE = 128
MXU_N = 256
HEAD_PAD = 128
HEADS_PER_STEP = MXU_N // V_HEAD
VMEM_LIMIT = 48 * 1024 * 1024

TM_IN = 512
TM_FFN = 512
TQ_LAT = 512
FF_CHUNK = 256
CONV_ROWS = 128
CM_ROWS = 64
SSD_CONV_ROWS = 32
ATTN_ROW_PARTS = 2

C_QLAT = 0
C_CKV = C_QLAT + Q_RANK
C_DT = C_CKV + KV_RANK
C_KR = C_DT + LANE
C_Z = C_KR + 2 * HEAD_PAD
C_XBC = C_Z + SSD_INNER
C_CM = C_XBC + SSD_XBC
IN_ARR = C_CM + 2 * CM_CH
IN_PAD = -(-IN_WIDTH // LANE) * LANE


def _params(sem):
    return pltpu.CompilerParams(dimension_semantics=sem, vmem_limit_bytes=VMEM_LIMIT)


def _dot(a, b):
    return jnp.dot(a, b, preferred_element_type=F32)


def _dot_nt(a, b):
    return lax.dot_general(a, b, (((1,), (1,)), ((), ())), preferred_element_type=F32)


def _rms(x, eps=1e-6):
    return x * lax.rsqrt(jnp.mean(x * x, axis=-1, keepdims=True) + eps)


def _silu(x):
    return x * jax.nn.sigmoid(x)


def _layer_spec(shape, l, single_buffer=False):
    nd = len(shape)
    kw = dict(pipeline_mode=pl.Buffered(1)) if single_buffer else {}
    return pl.BlockSpec((None,) + tuple(shape), lambda *_: (l,) + (0,) * nd, **kw)


def _vec(v):
    return v.reshape(DEPTH, 1, -1)


def _place_kernel(w_ref, p_ref, o_ref):
    o_ref[0] = _dot(w_ref[0].astype(BF16), p_ref[...]).astype(BF16)


def _place_call(w, src, name):
    L, rows, cin = w.shape
    cout = src.shape[0]
    place = (jnp.arange(cin, dtype=jnp.int32)[:, None] == jnp.asarray(src, jnp.int32)[None, :]).astype(BF16)
    tr = min(rows, 256)
    return pl.pallas_call(
        _place_kernel,
        out_shape=jax.ShapeDtypeStruct((L, rows, cout), BF16),
        grid=(L, rows // tr),
        in_specs=[
            pl.BlockSpec((1, tr, cin), lambda l, i: (l, i, 0)),
            pl.BlockSpec((cin, cout), lambda l, i: (0, 0)),
        ],
        out_specs=pl.BlockSpec((1, tr, cout), lambda l, i: (l, i, 0)),
        compiler_params=_params(("parallel", "parallel")),
        name=name,
    )(w, place)


def _swap32():
    return np.arange(QK_ROPE).reshape(2, 2, QK_ROPE // 4)[:, ::-1, :].reshape(-1)


def _src_in():
    kr0 = Q_RANK + KV_RANK
    off_ssd = kr0 + QK_ROPE
    off_dt = off_ssd + SSD_INNER + SSD_XBC
    off_cm = off_dt + 2 * SSD_HEADS
    src = -np.ones((IN_ARR,), np.int64)
    src[0:C_DT] = np.arange(0, kr0)
    src[C_DT:C_DT + 2 * SSD_HEADS] = np.arange(off_dt, off_cm)
    src[C_KR + QK_NOPE:C_KR + QK_HEAD] = kr0 + np.arange(QK_ROPE)
    src[C_KR + HEAD_PAD + QK_NOPE:C_KR + HEAD_PAD + QK_HEAD] = kr0 + _swap32()
    src[C_Z:C_CM] = np.arange(off_ssd, off_dt)
    src[C_CM:IN_ARR] = np.arange(off_cm, IN_WIDTH)
    return src


def _src_uq():
    half = MLA_HEADS * HEAD_PAD
    src = -np.ones((2 * half,), np.int64)
    for h in range(MLA_HEADS):
        src[h * HEAD_PAD:h * HEAD_PAD + QK_HEAD] = h * QK_HEAD + np.arange(QK_HEAD)
        lo = half + h * HEAD_PAD + QK_NOPE
        src[lo:lo + QK_ROPE] = h * QK_HEAD + QK_NOPE + _swap32()
    return src


def _src_ukv():
    half = MLA_HEADS * HEAD_PAD
    src = -np.ones((half + MLA_HEADS * V_HEAD,), np.int64)
    for h in range(MLA_HEADS):
        base = h * (QK_NOPE + V_HEAD)
        src[h * HEAD_PAD:h * HEAD_PAD + QK_NOPE] = base + np.arange(QK_NOPE)
        src[half + h * V_HEAD:half + (h + 1) * V_HEAD] = base + QK_NOPE + np.arange(V_HEAD)
    return src


def _ada_kernel(cond_ref, w_ref, b_ref, o_ref):
    c = cond_ref[...]
    a = _silu(c).astype(BF16)
    o_ref[0] = _dot(a, w_ref[0].astype(BF16)) + b_ref[0]


def _ada_call(cond8, w_ada, b_ada):
    tn = 1536
    nt = (N_MOD * D_MODEL) // tn
    return pl.pallas_call(
        _ada_kernel,
        out_shape=jax.ShapeDtypeStruct((DEPTH, 8, N_MOD * D_MODEL), F32),
        grid=(DEPTH, nt),
        in_specs=[
            pl.BlockSpec((8, D_MODEL), lambda l, j: (0, 0)),
            pl.BlockSpec((1, D_MODEL, tn), lambda l, j: (l, 0, j)),
            pl.BlockSpec((1, 1, tn), lambda l, j: (l, 0, j)),
        ],
        out_specs=pl.BlockSpec((1, 8, tn), lambda l, j: (l, 0, j)),
        compiler_params=_params(("parallel", "parallel")),
        name="ada_mod",
    )(cond8, w_ada, b_ada.reshape(DEPTH, 1, N_MOD * D_MODEL))


def _mod_spec(l, tm, tmod, mod_row0):
    return pl.BlockSpec((None, 1, 1, N_MOD * D_MODEL), lambda i: (l, mod_row0 + (i * tm) // tmod, 0, 0))


def _inproj_kernel(*refs, rope):
    if rope:
        (x_ref, mod_ref, gmix_ref, win_ref, gq_ref, wuq_ref, gkv_ref, wukv_ref, cos_ref, sin_ref,
         q_out, k_out, v_out, ckv_out, kr_out, z_out, xbc_out, dt_out, cm_out) = refs
        cos = cos_ref[...]
        sin = sin_ref[...]
    else:
        (x_ref, mod_ref, gmix_ref, win_ref, gq_ref, wuq_ref, gkv_ref, wukv_ref,
         q_out, k_out, v_out, ckv_out, kr_out, z_out, xbc_out, dt_out, cm_out) = refs
    x = x_ref[...]
    sh1 = mod_ref[0, :, 0:D_MODEL]
    sc1 = mod_ref[0, :, D_MODEL:2 * D_MODEL]
    h = (_rms(x) * gmix_ref[...] * (1.0 + sc1) + sh1).astype(BF16)

    scale = QK_HEAD ** -0.5 * LOG2_E
    half = MLA_HEADS * HEAD_PAD
    q_lat = _dot(h, win_ref[:, C_QLAT:C_QLAT + Q_RANK])
    ncol = C_KR - C_CKV + (2 if rope else 1) * HEAD_PAD
    mid = _dot(h, win_ref[:, C_CKV:C_CKV + ncol])
    z_out[...] = _dot(h, win_ref[:, C_Z:C_Z + SSD_INNER])

    qn = (_rms(q_lat) * gq_ref[...]).astype(BF16)
    qq = _dot(qn, wuq_ref[...])
    ckv = _rms(mid[:, 0:KV_RANK]) * gkv_ref[...]
    ckv_out[...] = ckv
    kv = _dot(ckv.astype(BF16), wukv_ref[...])
    xbc_out[...] = _dot(h, win_ref[:, C_XBC:C_XBC + SSD_XBC])

    for hd in range(MLA_HEADS):
        lo = hd * HEAD_PAD
        qh = qq[:, lo:lo + HEAD_PAD]
        if rope:
            qh = qh * cos + qq[:, half + lo:half + lo + HEAD_PAD] * sin
        q_out[:, lo:lo + HEAD_PAD] = (qh * scale).astype(BF16)
    cm_out[...] = _dot(h, win_ref[:, C_CM:C_CM + 2 * CM_CH])

    dt_out[...] = mid[:, C_DT - C_CKV:C_DT - C_CKV + LANE]
    kr = mid[:, C_KR - C_CKV:C_KR - C_CKV + HEAD_PAD]
    kr_out[...] = kr
    kr_rot = kr
    if rope:
        kr_rot = kr * cos + mid[:, C_KR - C_CKV + HEAD_PAD:C_KR - C_CKV + 2 * HEAD_PAD] * sin
    for hd in range(MLA_HEADS):
        lo = hd * HEAD_PAD
        k_out[:, lo:lo + HEAD_PAD] = (kv[:, lo:lo + HEAD_PAD] + kr_rot).astype(BF16)
    v_out[...] = kv[:, half:half + MLA_HEADS * V_HEAD].astype(BF16)


def _inproj_call(l, x, mod, gmix, win, gq, wuq, gkv, wukv, rope_tabs, *, t, tmod, mod_row0):
    n = x.shape[0]
    tm = TM_IN
    rope = rope_tabs is not None
    half = MLA_HEADS * HEAD_PAD

    def tok(i):
        return (i, 0)

    outs = [
        ((n, half), BF16),
        ((n, half), BF16),
        ((n, MLA_HEADS * V_HEAD), BF16),
        ((n, KV_RANK), F32),
        ((n, HEAD_PAD), F32),
        ((n, SSD_INNER), F32),
        ((n, SSD_XBC), F32),
        ((n, LANE), F32),
        ((n, 2 * CM_CH), F32),
    ]
    in_specs = [
        pl.BlockSpec((tm, D_MODEL), tok),
        _mod_spec(l, tm, tmod, mod_row0),
        _layer_spec((1, D_MODEL), l),
        _layer_spec((D_MODEL, IN_ARR), l, single_buffer=True),
        _layer_spec((1, Q_RANK), l),
        _layer_spec((Q_RANK, 2 * half if rope else half), l, single_buffer=True),
        _layer_spec((1, KV_RANK), l),
        _layer_spec((KV_RANK, MLA_HEADS * (HEAD_PAD + V_HEAD)), l, single_buffer=True),
    ]
    args = [x, mod, gmix, win, gq, wuq, gkv, wukv]
    if rope:
        pos = lambda i: (i % (t // tm), 0)
        in_specs += [pl.BlockSpec((tm, HEAD_PAD), pos), pl.BlockSpec((tm, HEAD_PAD), pos)]
        args += list(rope_tabs)
    return pl.pallas_call(
        functools.partial(_inproj_kernel, rope=rope),
        out_shape=[jax.ShapeDtypeStruct(s, d) for s, d in outs],
        grid=(n // tm,),
        in_specs=in_specs,
        out_specs=[pl.BlockSpec((tm, s[1]), tok) for s, _ in outs],
        compiler_params=_params(("parallel",)),
        name="in_proj_lat" if rope else "in_proj_ctx",
    )(*args)


def _ctxkv_kernel(ckv_ref, kr_ref, wukv_ref, place_ref, k_out, v_out):
    kv = _dot(ckv_ref[0, 0].astype(BF16), wukv_ref[0])
    krp = _dot(kr_ref[0, 0].astype(BF16), place_ref[...])
    half = MLA_HEADS * HEAD_PAD
    k_out[0, 0] = (kv[:, 0:half] + krp).astype(BF16)
    v_out[0, 0] = kv[:, half:half + MLA_HEADS * V_HEAD].astype(BF16)


def _ctxkv_call(cache_ckv, cache_krope, wukv_all, place):
    return pl.pallas_call(
        _ctxkv_kernel,
        out_shape=[
            jax.ShapeDtypeStruct((DEPTH, DEC_BATCH, PAST_LEN, MLA_HEADS * HEAD_PAD), BF16),
            jax.ShapeDtypeStruct((DEPTH, DEC_BATCH, PAST_LEN, MLA_HEADS * V_HEAD), BF16),
        ],
        grid=(DEPTH, DEC_BATCH),
        in_specs=[
            pl.BlockSpec((1, 1, PAST_LEN, KV_RANK), lambda l, b: (b, l, 0, 0)),
            pl.BlockSpec((1, 1, PAST_LEN, QK_ROPE), lambda l, b: (b, l, 0, 0)),
            pl.BlockSpec((1, KV_RANK, MLA_HEADS * (HEAD_PAD + V_HEAD)), lambda l, b: (l, 0, 0)),
            pl.BlockSpec((QK_ROPE, MLA_HEADS * HEAD_PAD), lambda l, b: (0, 0)),
        ],
        out_specs=[
            pl.BlockSpec((1, 1, PAST_LEN, MLA_HEADS * HEAD_PAD), lambda l, b: (l, b, 0, 0)),
            pl.BlockSpec((1, 1, PAST_LEN, MLA_HEADS * V_HEAD), lambda l, b: (l, b, 0, 0)),
        ],
        compiler_params=_params(("parallel", "parallel")),
        name="ctx_kv",
    )(cache_ckv, cache_krope, wukv_all, place)


def _attn_kernel(*refs, has_ctx, ngroups, nparts):
    if has_ctx:
        q_ref, k_ref, v_ref, kc_ref, vc_ref, o_ref = refs
    else:
        q_ref, k_ref, v_ref, o_ref = refs
    gw = HEADS_PER_STEP * V_HEAD
    tq = q_ref.shape[0]
    rp = tq // nparts
    units = [(grp, part, hh) for grp in range(ngroups) for part in range(nparts) for hh in range(HEADS_PER_STEP)]

    def scores(u):
        grp, part, hh = u
        lo = (grp * HEADS_PER_STEP + hh) * HEAD_PAD
        qh = q_ref[part * rp:(part + 1) * rp, lo:lo + HEAD_PAD]
        s = _dot_nt(qh, k_ref[:, lo:lo + HEAD_PAD])
        sc = _dot_nt(qh, kc_ref[0, :, lo:lo + HEAD_PAD]) if has_ctx else None
        return s, sc

    def probs(s, sc):
        m = jnp.max(s, axis=-1, keepdims=True)
        if has_ctx:
            m = jnp.maximum(m, jnp.max(sc, axis=-1, keepdims=True))
        p = jnp.exp2(s - m)
        den = jnp.sum(p, axis=-1, keepdims=True)
        pc = None
        if has_ctx:
            pc = jnp.exp2(sc - m)
            den = den + jnp.sum(pc, axis=-1, keepdims=True)
            pc = pc.astype(BF16)
        return p.astype(BF16), pc, den

    def values(u, p, pc, den, acc):
        grp, part, hh = u
        o = _dot(p, v_ref[:, grp * gw:(grp + 1) * gw])
        if has_ctx:
            o = o + _dot(pc, vc_ref[0, :, grp * gw:(grp + 1) * gw])
        if hh == 0:
            return o, jnp.broadcast_to(den, o.shape)
        mine = lax.broadcasted_iota(jnp.int32, o.shape, 1) >= hh * V_HEAD
        return jnp.where(mine, o, acc[0]), jnp.where(mine, den, acc[1])

    sc_next = scores(units[0])
    acc = None
    for i, u in enumerate(units):
        sc_cur = sc_next
        if i + 1 < len(units):
            sc_next = scores(units[i + 1])
        acc = values(u, *probs(*sc_cur), acc)
        grp, part, hh = u
        if hh == HEADS_PER_STEP - 1:
            o_ref[part * rp:(part + 1) * rp, grp * gw:(grp + 1) * gw] = (acc[0] / acc[1]).astype(BF16)


def _attn_call(l, q, k, v, kc, vc, *, nb, t, tq, ngroups):
    has_ctx = kc is not None
    nq = t // tq
    nh = ngroups * HEADS_PER_STEP
    nsteps = MLA_HEADS // nh
    in_specs = [
        pl.BlockSpec((tq, nh * HEAD_PAD), lambda b, g, qi: (b * nq + qi, g)),
        pl.BlockSpec((t, nh * HEAD_PAD), lambda b, g, qi: (b, g)),
        pl.BlockSpec((t, nh * V_HEAD), lambda b, g, qi: (b, g)),
    ]
    args = [q, k, v]
    if has_ctx:
        in_specs += [
            pl.BlockSpec((None, 1, PAST_LEN, nh * HEAD_PAD), lambda b, g, qi: (l, b, 0, g)),
            pl.BlockSpec((None, 1, PAST_LEN, nh * V_HEAD), lambda b, g, qi: (l, b, 0, g)),
        ]
        args += [kc, vc]
    return pl.pallas_call(
        functools.partial(_attn_kernel, has_ctx=has_ctx, ngroups=ngroups, nparts=ATTN_ROW_PARTS if has_ctx else 1),
        out_shape=jax.ShapeDtypeStruct((nb * t, MLA_HEADS * V_HEAD), BF16),
        grid=(nb, nsteps, nq),
        in_specs=in_specs,
        out_specs=pl.BlockSpec((tq, nh * V_HEAD), lambda b, g, qi: (b * nq + qi, g)),
        compiler_params=_params(("parallel", "parallel", "arbitrary")),
        name="attn_lat" if has_ctx else "attn_ctx",
    )(*args)


def _split3(x):
    hi = x.astype(BF16)
    r = x - hi.astype(F32)
    mid = r.astype(BF16)
    lo = (r - mid.astype(F32)).astype(BF16)
    return hi, mid, lo


def _ssd_kernel(*refs, t, has_h0):
    if has_h0:
        (z_ref, xbc_ref, dt_ref, cw_ref, cb_ref, dtb_ref, alog_ref, d0_ref, d1_ref, gn_ref, h0_ref,
         y_out, hf_out, xpad, xc, dtp_s, acs_s, acst_s, ysc, ssc, dec_s, eacs_s, hin_s, st) = refs
    else:
        (z_ref, xbc_ref, dt_ref, cw_ref, cb_ref, dtb_ref, alog_ref, d0_ref, d1_ref, gn_ref,
         y_out, hf_out, xpad, xc, dtp_s, acs_s, acst_s, ysc, ssc, dec_s, eacs_s, hin_s, st) = refs
    L = SSD_CHUNK
    W = SSD_INNER
    nc = t // L
    halo = SUBLANE

    xpad[0:halo, :] = jnp.zeros((halo, SSD_XBC), F32)
    xpad[halo + t:halo + t + halo, :] = jnp.zeros((halo, SSD_XBC), F32)
    xpad[halo:halo + t, :] = xbc_ref[...]
    cw = cw_ref[...]
    cb = cb_ref[...]
    rc = SSD_CONV_ROWS
    nwin = rc + 2 * halo

    def conv(blk, carry):
        base = pl.multiple_of(blk * rc, rc)
        win = xpad[pl.ds(base, nwin), :]
        acc = cb + cw[SSD_CONV // 2:SSD_CONV // 2 + 1, :] * win[halo:halo + rc, :]
        for kk in range(SSD_CONV):
            if kk != SSD_CONV // 2:
                first = halo - SSD_CONV // 2 + kk
                acc = acc + cw[kk:kk + 1, :] * pltpu.roll(win, nwin - first, axis=0)[0:rc, :]
        xc[pl.ds(base, rc), :] = _silu(acc)
        return carry

    lax.fori_loop(0, t // rc, conv, 0, unroll=2)
    r = CONV_ROWS

    a_all = -jnp.exp(alog_ref[...])
    dtb = dtb_ref[...]
    row_i = lax.broadcasted_iota(jnp.int32, (L, L), 0)
    col_i = lax.broadcasted_iota(jnp.int32, (L, L), 1)
    tri = [(row_i >= col_i), (row_i <= col_i)]
    tri_bf = [jnp.where(m, 1.0, 0.0).astype(BF16) for m in tri]
    grp_rows = [row_i < SSD_STATE, row_i >= SSD_STATE]
    lane_w = lax.broadcasted_iota(jnp.int32, (L, W), 1)
    own_block = (lax.broadcasted_iota(jnp.int32, (L, W), 0) // SSD_STATE) == (lane_w // LANE)
    sp_rows = lax.broadcasted_iota(jnp.int32, (3 * LANE, 2 * W), 0) % LANE
    sp_cols = lax.broadcasted_iota(jnp.int32, (3 * LANE, 2 * W), 1) // SSD_HEAD_DIM
    spread = jnp.where(sp_rows == sp_cols, 1.0, 0.0).astype(BF16)

    def chunk_rows(c):
        return pl.ds(pl.multiple_of(c * L, L), L)

    lane_lo = col_i < SSD_HEAD_DIM

    def columns(m, d):
        return [jnp.broadcast_to(m[:, j:j + 1], (L, LANE)) for j in range(d * SSD_HEADS, (d + 1) * SSD_HEADS)]

    def expand(cols):
        return jnp.concatenate([jnp.where(lane_lo, cols[0], cols[1]), jnp.where(lane_lo, cols[2], cols[3])],
                               axis=1)

    def pre(c, carry):
        rows = chunk_rows(c)
        dtp = jax.nn.softplus(dt_ref[rows, :] + dtb)
        dtp_s[rows, :] = dtp
        x3 = jnp.concatenate(_split3(dtp * a_all), axis=1)
        rf = _dot(tri_bf[0], x3)
        rb = _dot(tri_bf[1], x3)
        acs_f = rf[:, 0:LANE] + rf[:, LANE:2 * LANE] + rf[:, 2 * LANE:3 * LANE]
        acs_b = rb[:, 0:LANE] + rb[:, LANE:2 * LANE] + rb[:, 2 * LANE:3 * LANE]
        acs = jnp.where(col_i < SSD_HEADS, acs_f, acs_b)
        acs_s[rows, :] = acs
        acst_s[pl.ds(pl.multiple_of(c * SUBLANE, SUBLANE), SUBLANE), :] = acs.T[0:SUBLANE, :]
        return carry

    lax.fori_loop(0, nc, pre, 0, unroll=2)

    def main(c, carry):
        rows = chunk_rows(c)
        xs = xc[rows, 0:W]
        cm = xc[rows, W + LANE:W + 2 * LANE].astype(BF16)
        bt = xc[rows, W:W + LANE].T
        btm2 = jnp.concatenate([jnp.where(grp_rows[0], bt, 0.0), jnp.where(grp_rows[1], bt, 0.0)],
                               axis=1).astype(BF16)
        g2 = _dot(cm, btm2)
        bt_bf = bt.astype(BF16)
        dtp = dtp_s[rows, :]
        acs = acs_s[rows, :]
        acst = acst_s[pl.ds(pl.multiple_of(c * SUBLANE, SUBLANE), SUBLANE), :]
        dt_wide = _dot(jnp.concatenate(_split3(dtp), axis=1), spread)
        ysum = None
        for d in range(2):
            acs_cols = columns(acs, d)
            acs_b = expand(acs_cols)
            eacs_s[c, d] = jnp.exp(acs_b)
            xdt = xs * dt_wide[:, d * W:(d + 1) * W]
            xdt_bf = xdt.astype(BF16)
            yd = None
            for h in range(SSD_HEADS):
                j = d * SSD_HEADS + h
                g = h // (SSD_HEADS // SSD_GROUPS)
                diff = acs_cols[h] - acst[j:j + 1, :]
                seg = jnp.exp(jnp.where(tri[d], diff, -jnp.inf))
                res = _dot((g2[:, g * L:(g + 1) * L] * seg).astype(BF16), xdt_bf)
                yd = res if h == 0 else jnp.where(lane_w >= h * SSD_HEAD_DIM, res, yd)
            ysum = yd if d == 0 else ysum + yd
            tot = acs_b[L - 1:L, :] if d == 0 else acs_b[0:1, :]
            s_new = _dot(bt_bf, (xdt * jnp.exp(tot - acs_b)).astype(BF16))
            ssc[c, d] = jnp.where(own_block, s_new, 0.0)
            dec_s[c, d] = jnp.broadcast_to(jnp.exp(tot), (SUBLANE, W))
        ysc[rows, :] = ysum
        return carry

    lax.fori_loop(0, nc, main, 0, unroll=2)

    if has_h0:
        st[...] = h0_ref[0]
    else:
        st[...] = jnp.zeros(st.shape, F32)

    def rec(i, carry):
        for d, c in ((0, i), (1, nc - 1 - i)):
            h_in = st[d]
            hin_s[c, d] = h_in.astype(BF16)
            st[d] = dec_s[c, d][0:1, :] * h_in + ssc[c, d]
        return carry

    lax.fori_loop(0, nc, rec, 0)
    for d in range(2):
        for g in range(SSD_GROUPS):
            hf_out[0, d, g] = st[d, g * SSD_STATE:(g + 1) * SSD_STATE, g * LANE:(g + 1) * LANE]

    d0 = d0_ref[...]
    d1 = d1_ref[...]
    gn = gn_ref[...]

    def fin(c, carry):
        rows = chunk_rows(c)
        xs = xc[rows, 0:W]
        cm = xc[rows, W + LANE:W + 2 * LANE].astype(BF16)
        y = ysc[rows, :] + d0 * xs + d1 * xs
        for d in range(2):
            y = y + _dot(cm, hin_s[c, d]) * eacs_s[c, d]
        y = y * _silu(z_ref[rows, :])
        y_out[rows, :] = (_rms(y) * gn).astype(BF16)
        return carry

    lax.fori_loop(0, nc, fin, 0, unroll=2)


def _ssd_call(l, z, xbc, dt, cw, cb, dtb, alog, d0, d1, gn, h0, *, nb, t):
    has_h0 = h0 is not None
    nc = t // SSD_CHUNK

    def tok(b):
        return (b, 0)

    in_specs = [
        pl.BlockSpec((t, SSD_INNER), tok),
        pl.BlockSpec((t, SSD_XBC), tok),
        pl.BlockSpec((t, LANE), tok),
        _layer_spec((SUBLANE, SSD_XBC), l),
        _layer_spec((1, SSD_XBC), l),
        _layer_spec((1, LANE), l),
        _layer_spec((1, LANE), l),
        _layer_spec((1, SSD_INNER), l),
        _layer_spec((1, SSD_INNER), l),
        _layer_spec((1, SSD_INNER), l),
    ]
    args = [z, xbc, dt, cw, cb, dtb, alog, d0, d1, gn]
    if has_h0:
        in_specs.append(pl.BlockSpec((None, 1, 2, LANE, SSD_INNER), lambda b: (l, b, 0, 0, 0)))
        args.append(h0)
    return pl.pallas_call(
        functools.partial(_ssd_kernel, t=t, has_h0=has_h0),
        out_shape=[
            jax.ShapeDtypeStruct((nb * t, SSD_INNER), BF16),
            jax.ShapeDtypeStruct((nb, 2, SSD_GROUPS, SSD_STATE, LANE), F32),
        ],
        grid=(nb,),
        in_specs=in_specs,
        out_specs=[
            pl.BlockSpec((t, SSD_INNER), tok),
            pl.BlockSpec((1, 2, SSD_GROUPS, SSD_STATE, LANE), lambda b: (b, 0, 0, 0, 0)),
        ],
        scratch_shapes=[
            pltpu.VMEM((t + 2 * SUBLANE, SSD_XBC), F32),
            pltpu.VMEM((t, SSD_XBC), F32),
            pltpu.VMEM((t, LANE), F32),
            pltpu.VMEM((t, LANE), F32),
            pltpu.VMEM((nc * SUBLANE, LANE), F32),
            pltpu.VMEM((t, SSD_INNER), F32),
            pltpu.VMEM((nc, 2, LANE, SSD_INNER), F32),
            pltpu.VMEM((nc, 2, SUBLANE, SSD_INNER), F32),
            pltpu.VMEM((nc, 2, LANE, SSD_INNER), F32),
            pltpu.VMEM((nc, 2, LANE, SSD_INNER), BF16),
            pltpu.VMEM((2, LANE, SSD_INNER), F32),
        ],
        compiler_params=_params(("parallel",)),
        name="ssd_lat" if has_h0 else "ssd_ctx",
    )(*args)


CM_HALO = 16


def _glu(v):
    return v[:, 0:CM_CH] * jax.nn.sigmoid(v[:, CM_CH:2 * CM_CH])


def _conv_tile(cur_ref, prev_rows, next_rows, w, bias, lg, lb, gpad, dst, *, seg):
    tm = cur_ref.shape[0]
    halo = CM_HALO
    sub = SUBLANE
    rb = CM_ROWS
    assert halo - CM_WIDTH // 2 == 1 and CM_WIDTH <= 4 * sub
    zeros = jnp.zeros((halo, CM_CH), F32)
    for sgi in range(tm // seg):
        for piece in range(seg // CONV_ROWS):
            r0 = piece * CONV_ROWS
            gpad[sgi, halo + r0:halo + r0 + CONV_ROWS, :] = _glu(
                cur_ref[sgi * seg + r0:sgi * seg + r0 + CONV_ROWS, :])
        gpad[sgi, 0:halo, :] = zeros if prev_rows is None else prev_rows
        gpad[sgi, halo + seg:halo + seg + halo, :] = zeros if next_rows is None else next_rows

    def block(sgi, blk):
        base = blk * rb
        acc = bias
        for j in range(sub):
            pj = None
            for a in range(4):
                kk = sub * a + j
                if kk < CM_WIDTH:
                    term = w[kk:kk + 1, :] * gpad[sgi, base + sub * a:base + sub * a + rb + sub, :]
                    pj = term if pj is None else pj + term
            acc = acc + pj[j + 1:j + 1 + rb, :]
        mu = jnp.mean(acc, axis=-1, keepdims=True)
        xc = acc - mu
        var = jnp.mean(xc * xc, axis=-1, keepdims=True)
        y = xc * lax.rsqrt(var + 1e-5) * lg + lb
        dst[sgi * seg + base:sgi * seg + base + rb, :] = _silu(y).astype(BF16)

    return [functools.partial(block, sgi, blk) for sgi in range(tm // seg) for blk in range(seg // rb)]


def _ffn_conv_kernel(*refs, final, seg, tiles_per_seq):
    halos = tiles_per_seq > 1
    if halos:
        (x_ref, attn_ref, ssm_ref, cma_ref, cmb_ref, cmn_ref, mod_ref, wout_ref, gffn_ref, wg_ref, wu_ref,
         wd_ref, gfin_ref, cw_ref, cb_ref, lg_ref, lb_ref, o_ref, gpad, conv_s) = refs
    else:
        (x_ref, attn_ref, ssm_ref, cma_ref, cmb_ref, mod_ref, wout_ref, gffn_ref, wg_ref, wu_ref,
         wd_ref, gfin_ref, cw_ref, cb_ref, lg_ref, lb_ref, o_ref, gpad, conv_s) = refs
    i = pl.program_id(0)
    tm = x_ref.shape[0]
    cw = cw_ref[...]
    cbias = cb_ref[...]
    lg = lg_ref[...]
    lb = lb_ref[...]
    conv_args = dict(w=cw, bias=cbias, lg=lg, lb=lb, gpad=gpad, seg=seg)

    @pl.when(i == 0)
    def _():
        nxt = _glu(cmb_ref[0:CM_HALO, :]) if halos else None
        for emit in _conv_tile(cma_ref, None, nxt, dst=conv_s, **conv_args):
            emit()

    n_attn = MLA_HEADS * V_HEAD
    mixed = _dot(attn_ref[...], wout_ref[0:n_attn, :])
    mixed = mixed + _dot(ssm_ref[...], wout_ref[n_attn:n_attn + SSD_INNER, :])
    mixed = mixed + _dot(conv_s[...], wout_ref[n_attn + SSD_INNER:n_attn + SSD_INNER + CM_CH, :])
    g1 = mod_ref[0, :, 2 * D_MODEL:3 * D_MODEL]
    sh2 = mod_ref[0, :, 3 * D_MODEL:4 * D_MODEL]
    sc2 = mod_ref[0, :, 4 * D_MODEL:5 * D_MODEL]
    g2 = mod_ref[0, :, 5 * D_MODEL:6 * D_MODEL]
    x = x_ref[...] + g1 * mixed
    h = (_rms(x) * gffn_ref[...] * (1.0 + sc2) + sh2).astype(BF16)

    if halos:
        j = i + 1
        first = (j % tiles_per_seq) == 0
        last = (j % tiles_per_seq) == tiles_per_seq - 1
        prv = jnp.where(first, 0.0, _glu(cma_ref[tm - CM_HALO:tm, :]))
        nxt = jnp.where(last, 0.0, _glu(cmn_ref[...]))
    else:
        prv = nxt = None
    conv_blocks = _conv_tile(cmb_ref, prv, nxt, dst=conv_s, **conv_args)

    nchunk = D_FF // FF_CHUNK
    ff = jnp.zeros(x.shape, F32)
    for jc in range(nchunk):
        cols = slice(jc * FF_CHUNK, (jc + 1) * FF_CHUNK)
        gate = _dot(h, wg_ref[:, cols])
        up = _dot(h, wu_ref[:, cols])
        ff = ff + _dot((_silu(gate) * up).astype(BF16), wd_ref[cols, :])
        for emit in conv_blocks[jc * len(conv_blocks) // nchunk:(jc + 1) * len(conv_blocks) // nchunk]:
            emit()
    x = x + g2 * ff
    if final:
        x = _rms(x) * gfin_ref[...]
    o_ref[...] = x


def _ffn_conv_call(l, x, attn, ssm, cm, mod, wout, gffn, wg, wu, wd, gfin, cw, cb, lg, lb, *,
                   t, tmod, mod_row0, final):
    n = x.shape[0]
    tm = TM_FFN
    nt = n // tm
    seg = min(t, tm)
    tiles_per_seq = t // seg
    halos = tiles_per_seq > 1
    hb = tm // CM_HALO

    def tok(i):
        return (i, 0)

    in_specs = [
        pl.BlockSpec((tm, D_MODEL), tok),
        pl.BlockSpec((tm, MLA_HEADS * V_HEAD), tok),
        pl.BlockSpec((tm, SSD_INNER), tok),
        pl.BlockSpec((tm, 2 * CM_CH), tok),
        pl.BlockSpec((tm, 2 * CM_CH), lambda i: (jnp.minimum(i + 1, nt - 1), 0)),
    ]
    args = [x, attn, ssm, cm, cm]
    if halos:
        in_specs.append(pl.BlockSpec((CM_HALO, 2 * CM_CH), lambda i: (jnp.minimum((i + 2) * hb, nt * hb - 1), 0)))
        args.append(cm)
    in_specs += [
        _mod_spec(l, tm, tmod, mod_row0),
        _layer_spec((D_MODEL, D_MODEL), l, single_buffer=True),
        _layer_spec((1, D_MODEL), l),
        _layer_spec((D_MODEL, D_FF), l, single_buffer=True),
        _layer_spec((D_MODEL, D_FF), l, single_buffer=True),
        _layer_spec((D_FF, D_MODEL), l, single_buffer=True),
        pl.BlockSpec((1, D_MODEL), lambda i: (0, 0)),
        _layer_spec((4 * SUBLANE, CM_CH), l),
        _layer_spec((1, CM_CH), l),
        _layer_spec((1, CM_CH), l),
        _layer_spec((1, CM_CH), l),
    ]
    args += [mod, wout, gffn, wg, wu, wd, gfin, cw, cb, lg, lb]
    return pl.pallas_call(
        functools.partial(_ffn_conv_kernel, final=final, seg=seg, tiles_per_seq=tiles_per_seq),
        out_shape=jax.ShapeDtypeStruct((n, D_MODEL), F32),
        grid=(nt,),
        in_specs=in_specs,
        out_specs=pl.BlockSpec((tm, D_MODEL), tok),
        scratch_shapes=[
            pltpu.VMEM((tm // seg, seg + 2 * CM_HALO, CM_CH), F32),
            pltpu.VMEM((tm, CM_CH), BF16),
        ],
        compiler_params=_params(("arbitrary",)),
        name="ffn_conv",
    )(*args)


def _cm_kernel(cm_ref, w_ref, b_ref, lg_ref, lb_ref, o_ref, gpad, *, t):
    halo = 16
    gpad[0:halo, :] = jnp.zeros((halo, CM_CH), F32)
    gpad[halo + t:halo + t + halo, :] = jnp.zeros((halo, CM_CH), F32)
    r = CONV_ROWS
    for blk in range(t // r):
        rows = slice(blk * r, (blk + 1) * r)
        gpad[halo + blk * r:halo + (blk + 1) * r, :] = (
            cm_ref[rows, 0:CM_CH] * jax.nn.sigmoid(cm_ref[rows, CM_CH:2 * CM_CH]))
    w = w_ref[...]
    bias = b_ref[...]
    lg = lg_ref[...]
    lb = lb_ref[...]
    rb = CM_ROWS
    sub = SUBLANE
    assert halo - CM_WIDTH // 2 == 1 and CM_WIDTH <= 4 * sub

    def body(blk, carry):
        base = pl.multiple_of(blk * rb, rb)
        acc = bias
        for j in range(sub):
            pj = None
            for a in range(4):
                kk = sub * a + j
                if kk < CM_WIDTH:
                    term = w[kk:kk + 1, :] * gpad[pl.ds(base + sub * a, rb + sub), :]
                    pj = term if pj is None else pj + term
            acc = acc + pj[j + 1:j + 1 + rb, :]
        mu = jnp.mean(acc, axis=-1, keepdims=True)
        xc = acc - mu
        var = jnp.mean(xc * xc, axis=-1, keepdims=True)
        y = xc * lax.rsqrt(var + 1e-5) * lg + lb
        o_ref[pl.ds(base, rb), :] = _silu(y).astype(BF16)
        return carry

    lax.fori_loop(0, t // rb, body, 0, unroll=2)


def _cm_call(l, cm, w, b, lg, lb, *, nb, t):
    return pl.pallas_call(
        functools.partial(_cm_kernel, t=t),
        out_shape=jax.ShapeDtypeStruct((nb * t, CM_CH), BF16),
        grid=(nb,),
        in_specs=[
            pl.BlockSpec((t, 2 * CM_CH), lambda i: (i, 0)),
            _layer_spec((4 * SUBLANE, CM_CH), l),
            _layer_spec((1, CM_CH), l),
            _layer_spec((1, CM_CH), l),
            _layer_spec((1, CM_CH), l),
        ],
        out_specs=pl.BlockSpec((t, CM_CH), lambda i: (i, 0)),
        scratch_shapes=[pltpu.VMEM((t + 32, CM_CH), F32)],
        compiler_params=_params(("parallel",)),
        name="conv_module",
    )(cm, w, b, lg, lb)


def _ffn_kernel(x_ref, attn_ref, ssm_ref, conv_ref, mod_ref, wout_ref, gffn_ref, wg_ref, wu_ref, wd_ref,
                gfin_ref, o_ref, *, final):
    n_attn = MLA_HEADS * V_HEAD
    mixed = _dot(attn_ref[...], wout_ref[0:n_attn, :])
    mixed = mixed + _dot(ssm_ref[...], wout_ref[n_attn:n_attn + SSD_INNER, :])
    mixed = mixed + _dot(conv_ref[...], wout_ref[n_attn + SSD_INNER:n_attn + SSD_INNER + CM_CH, :])
    g1 = mod_ref[0, :, 2 * D_MODEL:3 * D_MODEL]
    sh2 = mod_ref[0, :, 3 * D_MODEL:4 * D_MODEL]
    sc2 = mod_ref[0, :, 4 * D_MODEL:5 * D_MODEL]
    g2 = mod_ref[0, :, 5 * D_MODEL:6 * D_MODEL]
    x = x_ref[...] + g1 * mixed
    h = (_rms(x) * gffn_ref[...] * (1.0 + sc2) + sh2).astype(BF16)
    ff = jnp.zeros(x.shape, F32)
    for j in range(D_FF // FF_CHUNK):
        cols = slice(j * FF_CHUNK, (j + 1) * FF_CHUNK)
        gate = _dot(h, wg_ref[:, cols])
        up = _dot(h, wu_ref[:, cols])
        ff = ff + _dot((_silu(gate) * up).astype(BF16), wd_ref[cols, :])
    x = x + g2 * ff
    if final:
        x = _rms(x) * gfin_ref[...]
    o_ref[...] = x


def _ffn_call(l, x, attn, ssm, conv, mod, wout, gffn, wg, wu, wd, gfin, *, tmod, mod_row0, final):
    n = x.shape[0]
    tm = TM_FFN

    def tok(i):
        return (i, 0)

    return pl.pallas_call(
        functools.partial(_ffn_kernel, final=final),
        out_shape=jax.ShapeDtypeStruct((n, D_MODEL), F32),
        grid=(n // tm,),
        in_specs=[
            pl.BlockSpec((tm, D_MODEL), tok),
            pl.BlockSpec((tm, MLA_HEADS * V_HEAD), tok),
            pl.BlockSpec((tm, SSD_INNER), tok),
            pl.BlockSpec((tm, CM_CH), tok),
            _mod_spec(l, tm, tmod, mod_row0),
            _layer_spec((D_MODEL, D_MODEL), l, single_buffer=True),
            _layer_spec((1, D_MODEL), l),
            _layer_spec((D_MODEL, D_FF), l, single_buffer=True),
            _layer_spec((D_MODEL, D_FF), l, single_buffer=True),
            _layer_spec((D_FF, D_MODEL), l, single_buffer=True),
            pl.BlockSpec((1, D_MODEL), lambda i: (0, 0)),
        ],
        out_specs=pl.BlockSpec((tm, D_MODEL), tok),
        compiler_params=_params(("parallel",)),
        name="out_ffn",
    )(x, attn, ssm, conv, mod, wout, gffn, wg, wu, wd, gfin)


def _rope_tables():
    rows = DEC_SEQ // GRID_W
    row = jnp.repeat(jnp.arange(rows), GRID_W).astype(F32)
    col = (jnp.arange(rows * GRID_W) % GRID_W).astype(F32)
    nf = QK_ROPE // 4
    inv = ROPE_THETA ** (-jnp.arange(nf, dtype=F32) / nf)
    ang = jnp.stack([row[:, None] * inv, col[:, None] * inv], axis=1)
    cos = jnp.cos(ang)
    sin = jnp.sin(ang)
    cos32 = jnp.stack([cos, cos], axis=2).reshape(DEC_SEQ, QK_ROPE)
    sin32 = jnp.stack([-sin, sin], axis=2).reshape(DEC_SEQ, QK_ROPE)
    ones = jnp.ones((DEC_SEQ, QK_NOPE), F32)
    pad1 = jnp.ones((DEC_SEQ, HEAD_PAD - QK_HEAD), F32)
    cos_t = jnp.concatenate([ones, cos32, pad1], axis=1)
    sin_t = jnp.concatenate([0.0 * ones, sin32, 0.0 * pad1], axis=1)
    return cos_t, sin_t


def _kr_placement():
    src = jnp.arange(QK_ROPE)
    cols = jnp.arange(MLA_HEADS * HEAD_PAD)
    hit = (cols[None, :] % HEAD_PAD) == (QK_NOPE + src[:, None])
    return hit.astype(BF16)


def _pack_states(h0):
    lead = h0.shape[:-3]
    s = jnp.moveaxis(h0, -1, -3).reshape(lead + (SSD_STATE, SSD_INNER))
    lane_group = jnp.arange(SSD_INNER) // LANE
    parts = [jnp.where(lane_group == g, s, 0.0) for g in range(SSD_GROUPS)]
    return jnp.concatenate(parts, axis=-2)


def _unpack_states(st):
    b = st.shape[0]
    s = st.reshape(b, 2, SSD_GROUPS, SSD_STATE, 2, SSD_HEAD_DIM).transpose(0, 1, 2, 4, 5, 3)
    return s.reshape(b, 2, SSD_HEADS, SSD_HEAD_DIM, SSD_STATE)


def kernel(x_prompt, x_sample, c, cache_ckv, cache_krope, state_ssd, c_ctx, w_ada, b_ada, g_mix, w_in, g_q, w_uq, g_kv, w_ukv, ssd_conv_w, ssd_conv_b, ssd_dt_bias, ssd_a_log, ssd_d, ssd_norm_g, cm_conv_w, cm_conv_b, cm_ln_g, cm_ln_b, w_out, g_ffn, w_gate, w_up, w_down, g_final):
    L = DEPTH
    w_in_pad = jnp.pad(w_in, ((0, 0), (0, 0), (0, IN_PAD - IN_WIDTH))).astype(BF16)
    win = _place_call(w_in_pad, _src_in(), "place_w_in")
    wuq = _place_call(w_uq, _src_uq(), "place_w_uq")
    wukv = _place_call(w_ukv, _src_ukv(), "place_w_ukv")
    wout = w_out.astype(BF16)
    wg = w_gate.astype(BF16)
    wu = w_up.astype(BF16)
    wd = w_down.astype(BF16)
    rope_tabs = _rope_tables()

    cond8 = jnp.concatenate([c_ctx[None, :], c, jnp.zeros((8 - 1 - DEC_BATCH, D_MODEL), F32)], axis=0)
    mod = _ada_call(cond8, w_ada, b_ada)
    mod = mod.reshape(L, 8, 1, N_MOD * D_MODEL)

    kc_all, vc_all = _ctxkv_call(cache_ckv, cache_krope, wukv, _kr_placement())

    ssd_cw = jnp.concatenate([ssd_conv_w, jnp.zeros((L, SUBLANE - SSD_CONV, SSD_XBC), F32)], axis=1)
    cm_w = jnp.concatenate([cm_conv_w, jnp.zeros((L, 4 * SUBLANE - CM_WIDTH, CM_CH), F32)], axis=1)
    lane_pad = lambda v: jnp.concatenate(
        [v.reshape(L, 1, 2 * SSD_HEADS), jnp.zeros((L, 1, LANE - 2 * SSD_HEADS), F32)], axis=-1)
    dtb = lane_pad(ssd_dt_bias)
    alog = lane_pad(ssd_a_log)
    d_rep = jnp.repeat(ssd_d, SSD_HEAD_DIM, axis=-1)
    d0 = d_rep[:, 0:1, :]
    d1 = d_rep[:, 1:2, :]
    h0_lat = _pack_states(jnp.swapaxes(state_ssd, 0, 1))
    gmix, gq, gkv, gffn = _vec(g_mix), _vec(g_q), _vec(g_kv), _vec(g_ffn)
    scb, sng = _vec(ssd_conv_b), _vec(ssd_norm_g)
    ccb, clg, clb = _vec(cm_conv_b), _vec(cm_ln_g), _vec(cm_ln_b)
    gfin = g_final.reshape(1, D_MODEL)

    streams = [
        dict(x=x_prompt.reshape(N_CTX, D_MODEL), nb=BATCH, t=SEQ, tmod=N_CTX, mod_row0=0, tq=SEQ, ngroups=2,
             rope=None),
        dict(x=x_sample.reshape(N_LAT, D_MODEL), nb=DEC_BATCH, t=DEC_SEQ, tmod=DEC_SEQ, mod_row0=1, tq=TQ_LAT,
             ngroups=1, rope=rope_tabs),
    ]
    ckvs, krs, hss = [], [], []
    for l in range(L):
        for si, s in enumerate(streams):
            lat = si == 1
            nb, t = s["nb"], s["t"]
            q, k, v, ckv, kr, z, xbc, dt, cm = _inproj_call(
                l, s["x"], mod, gmix, win, gq, wuq, gkv, wukv, s["rope"],
                t=t, tmod=s["tmod"], mod_row0=s["mod_row0"])
            attn = _attn_call(l, q, k, v, kc_all if lat else None, vc_all if lat else None,
                              nb=nb, t=t, tq=s["tq"], ngroups=s["ngroups"])
            ssm, hf = _ssd_call(l, z, xbc, dt, ssd_cw, scb, dtb, alog, d0, d1, sng,
                                h0_lat if lat else None, nb=nb, t=t)
            s["x"] = _ffn_conv_call(l, s["x"], attn, ssm, cm, mod, wout, gffn, wg, wu, wd, gfin,
                                    cm_w, ccb, clg, clb, t=t, tmod=s["tmod"], mod_row0=s["mod_row0"],
                                    final=(l == L - 1))
            if not lat:
                ckvs.append(ckv.reshape(BATCH, SEQ, KV_RANK))
                krs.append(kr[:, QK_NOPE:QK_HEAD].reshape(BATCH, SEQ, QK_ROPE))
                hss.append(_unpack_states(hf))

    y_prompt = streams[0]["x"].reshape(BATCH, SEQ, D_MODEL)
    y_sample = streams[1]["x"].reshape(DEC_BATCH, DEC_SEQ, D_MODEL)
    return (y_prompt, y_sample, jnp.stack(ckvs, axis=1), jnp.stack(krs, axis=1), jnp.stack(hss, axis=1))
```

```python
import functools

import numpy as np

import jax
import jax.numpy as jnp
from jax import lax
from jax.experimental import pallas as pl
from jax.experimental.pallas import tpu as pltpu

F32 = jnp.float32
BF16 = jnp.bfloat16

D_MODEL = 1024
BATCH = 16
SEQ = 256
DEPTH = 4
DEC_BATCH = 4
DEC_SEQ = 2048
PAST_LEN = 256
GRID_W = 64

MLA_HEADS = 8
QK_NOPE = 64
QK_ROPE = 32
QK_HEAD = QK_NOPE + QK_ROPE
V_HEAD = 64
Q_RANK = 256
KV_RANK = 128
ROPE_THETA = 10000.0
LOG2_E = 1.4426950408889634

SSD_HEADS = 4
SSD_HEAD_DIM = 64
SSD_INNER = SSD_HEADS * SSD_HEAD_DIM
SSD_GROUPS = 2
SSD_STATE = 64
SSD_CONV = 5
SSD_CHUNK = 128
SSD_XBC = SSD_INNER + 2 * SSD_GROUPS * SSD_STATE

CM_CH = 256
CM_WIDTH = 31

D_FF = 2816
N_MOD = 6
IN_WIDTH = (Q_RANK + KV_RANK + QK_ROPE) + (SSD_INNER + SSD_XBC + 2 * SSD_HEADS) + 2 * CM_CH

N_CTX = BATCH * SEQ
N_LAT = DEC_BATCH * DEC_SEQ

SUBLANE = 8
LANE = 128
MXU_N = 256
HEAD_PAD = 128
HEADS_PER_STEP = MXU_N // V_HEAD
VMEM_LIMIT = 48 * 1024 * 1024

TM_IN = 512
TM_FFN = 512
TQ_LAT = 1024
FF_CHUNK = 256
CONV_ROWS = 128
CM_ROWS = 64
SSD_CONV_ROWS = 32
ATTN_ROW_PARTS = 4

C_QLAT = 0
C_CKV = C_QLAT + Q_RANK
C_DT = C_CKV + KV_RANK
C_KR = C_DT + LANE
C_Z = C_KR + 2 * HEAD_PAD
C_XBC = C_Z + SSD_INNER
C_CM = C_XBC + SSD_XBC
IN_ARR = C_CM + 2 * CM_CH
IN_PAD = -(-IN_WIDTH // LANE) * LANE


def _params(sem):
    return pltpu.CompilerParams(dimension_semantics=sem, vmem_limit_bytes=VMEM_LIMIT)


def _dot(a, b):
    return jnp.dot(a, b, preferred_element_type=F32)


def _dot_nt(a, b):
    return lax.dot_general(a, b, (((1,), (1,)), ((), ())), preferred_element_type=F32)


def _rms(x, eps=1e-6):
    return x * lax.rsqrt(jnp.mean(x * x, axis=-1, keepdims=True) + eps)


def _silu(x):
    return x * jax.nn.sigmoid(x)


def _layer_spec(shape, l, single_buffer=False):
    nd = len(shape)
    kw = dict(pipeline_mode=pl.Buffered(1)) if single_buffer else {}
    return pl.BlockSpec((None,) + tuple(shape), lambda *_: (l,) + (0,) * nd, **kw)


def _vec(v):
    return v.reshape(DEPTH, 1, -1)


def _place_kernel(w_ref, p_ref, o_ref, *, groups):
    for dlo, dhi, slo, shi in groups:
        if slo is None:
            o_ref[0, :, dlo:dhi] = w_ref[0, :, dlo:dhi].astype(BF16)
        else:
            o_ref[0, :, dlo:dhi] = _dot(w_ref[0, :, slo:shi].astype(BF16), p_ref[slo:shi, dlo:dhi]).astype(BF16)


def _place_groups(src, cin, bounds):
    groups = []
    for dlo, dhi in zip(bounds[:-1], bounds[1:]):
        s = src[dlo:dhi]
        if np.array_equal(s, np.arange(dlo, dhi)):
            groups.append((dlo, dhi, None, None))
            continue
        used = s[s >= 0]
        slo = int(used.min()) // LANE * LANE
        shi = min(-(-(int(used.max()) + 1) // LANE) * LANE, cin)
        groups.append((dlo, dhi, slo, shi))
    return tuple(groups)


def _place_call(w, src, name, bounds=None):
    L, rows, cin = w.shape
    cout = src.shape[0]
    place = (jnp.arange(cin, dtype=jnp.int32)[:, None] == jnp.asarray(src, jnp.int32)[None, :]).astype(BF16)
    tr = min(rows, 256)
    groups = _place_groups(src, cin, bounds if bounds is not None else (0, cout))
    return pl.pallas_call(
        functools.partial(_place_kernel, groups=groups),
        out_shape=jax.ShapeDtypeStruct((L, rows, cout), BF16),
        grid=(L, rows // tr),
        in_specs=[
            pl.BlockSpec((1, tr, cin), lambda l, i: (l, i, 0)),
            pl.BlockSpec((cin, cout), lambda l, i: (0, 0)),
        ],
        out_specs=pl.BlockSpec((1, tr, cout), lambda l, i: (l, i, 0)),
        compiler_params=_params(("parallel", "parallel")),
        name=name,
    )(w, place)


def _swap32():
    return np.arange(QK_ROPE).reshape(2, 2, QK_ROPE // 4)[:, ::-1, :].reshape(-1)


def _src_in():
    kr0 = Q_RANK + KV_RANK
    off_ssd = kr0 + QK_ROPE
    off_dt = off_ssd + SSD_INNER + SSD_XBC
    off_cm = off_dt + 2 * SSD_HEADS
    src = -np.ones((IN_ARR,), np.int64)
    src[0:C_DT] = np.arange(0, kr0)
    src[C_DT:C_DT + 2 * SSD_HEADS] = np.arange(off_dt, off_cm)
    src[C_KR + QK_NOPE:C_KR + QK_HEAD] = kr0 + np.arange(QK_ROPE)
    src[C_KR + HEAD_PAD + QK_NOPE:C_KR + HEAD_PAD + QK_HEAD] = kr0 + _swap32()
    src[C_Z:C_CM] = np.arange(off_ssd, off_dt)
    src[C_CM:IN_ARR] = np.arange(off_cm, IN_WIDTH)
    return src


def _src_uq():
    half = MLA_HEADS * HEAD_PAD
    src = -np.ones((2 * half,), np.int64)
    for h in range(MLA_HEADS):
        src[h * HEAD_PAD:h * HEAD_PAD + QK_HEAD] = h * QK_HEAD + np.arange(QK_HEAD)
        lo = half + h * HEAD_PAD + QK_NOPE
        src[lo:lo + QK_ROPE] = h * QK_HEAD + QK_NOPE + _swap32()
    return src


def _src_ukv():
    half = MLA_HEADS * HEAD_PAD
    src = -np.ones((half + MLA_HEADS * V_HEAD,), np.int64)
    for h in range(MLA_HEADS):
        base = h * (QK_NOPE + V_HEAD)
        src[h * HEAD_PAD:h * HEAD_PAD + QK_NOPE] = base + np.arange(QK_NOPE)
        src[half + h * V_HEAD:half + (h + 1) * V_HEAD] = base + QK_NOPE + np.arange(V_HEAD)
    return src


def _ada_kernel(cond_ref, w_ref, b_ref, o_ref):
    c = cond_ref[...]
    a = _silu(c).astype(BF16)
    o_ref[0] = _dot(a, w_ref[0].astype(BF16)) + b_ref[0]


def _ada_call(cond8, w_ada, b_ada):
    tn = 1536
    nt = (N_MOD * D_MODEL) // tn
    return pl.pallas_call(
        _ada_kernel,
        out_shape=jax.ShapeDtypeStruct((DEPTH, 8, N_MOD * D_MODEL), F32),
        grid=(DEPTH, nt),
        in_specs=[
            pl.BlockSpec((8, D_MODEL), lambda l, j: (0, 0)),
            pl.BlockSpec((1, D_MODEL, tn), lambda l, j: (l, 0, j)),
            pl.BlockSpec((1, 1, tn), lambda l, j: (l, 0, j)),
        ],
        out_specs=pl.BlockSpec((1, 8, tn), lambda l, j: (l, 0, j)),
        compiler_params=_params(("parallel", "parallel")),
        name="ada_mod",
    )(cond8, w_ada, b_ada.reshape(DEPTH, 1, N_MOD * D_MODEL))


def _mod_spec(l, tm, tmod, mod_row0):
    return pl.BlockSpec((None, 1, 1, N_MOD * D_MODEL), lambda i: (l, mod_row0 + (i * tm) // tmod, 0, 0))


def _inproj_kernel(*refs, rope):
    if rope:
        (x_ref, mod_ref, gmix_ref, win_ref, gq_ref, wuq_ref, gkv_ref, wukv_ref, cos_ref, sin_ref,
         q_out, k_out, v_out, ckv_out, kr_out, z_out, xbc_out, dt_out, cm_out) = refs
        cos = cos_ref[...]
        sin = sin_ref[...]
    else:
        (x_ref, mod_ref, gmix_ref, win_ref, gq_ref, wuq_ref, gkv_ref, wukv_ref,
         q_out, k_out, v_out, ckv_out, kr_out, z_out, xbc_out, dt_out, cm_out) = refs
    x = x_ref[...]
    sh1 = mod_ref[0, :, 0:D_MODEL]
    sc1 = mod_ref[0, :, D_MODEL:2 * D_MODEL]
    h = (_rms(x) * gmix_ref[...] * (1.0 + sc1) + sh1).astype(BF16)

    scale = QK_HEAD ** -0.5 * LOG2_E
    half = MLA_HEADS * HEAD_PAD
    q_lat = _dot(h, win_ref[:, C_QLAT:C_QLAT + Q_RANK])
    ncol = C_KR - C_CKV + (2 if rope else 1) * HEAD_PAD
    mid = _dot(h, win_ref[:, C_CKV:C_CKV + ncol])
    z_out[...] = _dot(h, win_ref[:, C_Z:C_Z + SSD_INNER])

    qn = (_rms(q_lat) * gq_ref[...]).astype(BF16)
    qq = _dot(qn, wuq_ref[...])
    ckv = _rms(mid[:, 0:KV_RANK]) * gkv_ref[...]
    ckv_out[...] = ckv
    kv = _dot(ckv.astype(BF16), wukv_ref[...])
    xbc_out[...] = _dot(h, win_ref[:, C_XBC:C_XBC + SSD_XBC])

    for hd in range(MLA_HEADS):
        lo = hd * HEAD_PAD
        qh = qq[:, lo:lo + HEAD_PAD]
        if rope:
            qh = qh * cos + qq[:, half + lo:half + lo + HEAD_PAD] * sin
        q_out[:, lo:lo + HEAD_PAD] = (qh * scale).astype(BF16)
    cm_out[...] = _dot(h, win_ref[:, C_CM:C_CM + 2 * CM_CH])

    dt_out[...] = mid[:, C_DT - C_CKV:C_DT - C_CKV + LANE]
    kr = mid[:, C_KR - C_CKV:C_KR - C_CKV + HEAD_PAD]
    kr_out[...] = kr
    kr_rot = kr
    if rope:
        kr_rot = kr * cos + mid[:, C_KR - C_CKV + HEAD_PAD:C_KR - C_CKV + 2 * HEAD_PAD] * sin
    for hd in range(MLA_HEADS):
        lo = hd * HEAD_PAD
        k_out[:, lo:lo + HEAD_PAD] = (kv[:, lo:lo + HEAD_PAD] + kr_rot).astype(BF16)
    v_out[...] = kv[:, half:half + MLA_HEADS * V_HEAD].astype(BF16)


def _inproj_call(l, x, mod, gmix, win, gq, wuq, gkv, wukv, rope_tabs, *, t, tmod, mod_row0):
    n = x.shape[0]
    tm = TM_IN
    rope = rope_tabs is not None
    half = MLA_HEADS * HEAD_PAD

    def tok(i):
        return (i, 0)

    outs = [
        ((n, half), BF16),
        ((n, half), BF16),
        ((n, MLA_HEADS * V_HEAD), BF16),
        ((n, KV_RANK), F32),
        ((n, HEAD_PAD), F32),
        ((n, SSD_INNER), F32),
        ((n, SSD_XBC), F32),
        ((n, LANE), F32),
        ((n, 2 * CM_CH), F32),
    ]
    in_specs = [
        pl.BlockSpec((tm, D_MODEL), tok),
        _mod_spec(l, tm, tmod, mod_row0),
        _layer_spec((1, D_MODEL), l),
        _layer_spec((D_MODEL, IN_ARR), l, single_buffer=True),
        _layer_spec((1, Q_RANK), l),
        _layer_spec((Q_RANK, 2 * half if rope else half), l, single_buffer=True),
        _layer_spec((1, KV_RANK), l),
        _layer_spec((KV_RANK, MLA_HEADS * (HEAD_PAD + V_HEAD)), l, single_buffer=True),
    ]
    args = [x, mod, gmix, win, gq, wuq, gkv, wukv]
    if rope:
        pos = lambda i: (i % (t // tm), 0)
        in_specs += [pl.BlockSpec((tm, HEAD_PAD), pos), pl.BlockSpec((tm, HEAD_PAD), pos)]
        args += list(rope_tabs)
    return pl.pallas_call(
        functools.partial(_inproj_kernel, rope=rope),
        out_shape=[jax.ShapeDtypeStruct(s, d) for s, d in outs],
        grid=(n // tm,),
        in_specs=in_specs,
        out_specs=[pl.BlockSpec((tm, s[1]), tok) for s, _ in outs],
        compiler_params=_params(("parallel",)),
        name="in_proj_lat" if rope else "in_proj_ctx",
    )(*args)


def _ctxkv_kernel(ckv_ref, kr_ref, wukv_ref, place_ref, k_out, v_out):
    kv = _dot(ckv_ref[0, 0].astype(BF16), wukv_ref[0])
    krp = _dot(kr_ref[0, 0].astype(BF16), place_ref[...])
    half = MLA_HEADS * HEAD_PAD
    k_out[0, 0] = (kv[:, 0:half] + krp).astype(BF16)
    v_out[0, 0] = kv[:, half:half + MLA_HEADS * V_HEAD].astype(BF16)


def _ctxkv_call(cache_ckv, cache_krope, wukv_all, place):
    return pl.pallas_call(
        _ctxkv_kernel,
        out_shape=[
            jax.ShapeDtypeStruct((DEPTH, DEC_BATCH, PAST_LEN, MLA_HEADS * HEAD_PAD), BF16),
            jax.ShapeDtypeStruct((DEPTH, DEC_BATCH, PAST_LEN, MLA_HEADS * V_HEAD), BF16),
        ],
        grid=(DEPTH, DEC_BATCH),
        in_specs=[
            pl.BlockSpec((1, 1, PAST_LEN, KV_RANK), lambda l, b: (b, l, 0, 0)),
            pl.BlockSpec((1, 1, PAST_LEN, QK_ROPE), lambda l, b: (b, l, 0, 0)),
            pl.BlockSpec((1, KV_RANK, MLA_HEADS * (HEAD_PAD + V_HEAD)), lambda l, b: (l, 0, 0)),
            pl.BlockSpec((QK_ROPE, MLA_HEADS * HEAD_PAD), lambda l, b: (0, 0)),
        ],
        out_specs=[
            pl.BlockSpec((1, 1, PAST_LEN, MLA_HEADS * HEAD_PAD), lambda l, b: (l, b, 0, 0)),
            pl.BlockSpec((1, 1, PAST_LEN, MLA_HEADS * V_HEAD), lambda l, b: (l, b, 0, 0)),
        ],
        compiler_params=_params(("parallel", "parallel")),
        name="ctx_kv",
    )(cache_ckv, cache_krope, wukv_all, place)


def _attn_kernel(*refs, has_ctx, ngroups, nparts):
    if has_ctx:
        q_ref, k_ref, v_ref, kc_ref, vc_ref, o_ref = refs
    else:
        q_ref, k_ref, v_ref, o_ref = refs
    gw = HEADS_PER_STEP * V_HEAD
    tq = q_ref.shape[0]
    rp = tq // nparts
    units = [(grp, part, hh) for grp in range(ngroups) for part in range(nparts) for hh in range(HEADS_PER_STEP)]

    def scores(u):
        grp, part, hh = u
        lo = (grp * HEADS_PER_STEP + hh) * HEAD_PAD
        qh = q_ref[part * rp:(part + 1) * rp, lo:lo + HEAD_PAD]
        s = _dot_nt(qh, k_ref[:, lo:lo + HEAD_PAD])
        sc = _dot_nt(qh, kc_ref[0, :, lo:lo + HEAD_PAD]) if has_ctx else None
        return s, sc

    def probs(s, sc):
        m = jnp.max(s, axis=-1, keepdims=True)
        if has_ctx:
            m = jnp.maximum(m, jnp.max(sc, axis=-1, keepdims=True))
        p = jnp.exp2(s - m)
        den = jnp.sum(p, axis=-1, keepdims=True)
        pc = None
        if has_ctx:
            pc = jnp.exp2(sc - m)
            den = den + jnp.sum(pc, axis=-1, keepdims=True)
            pc = pc.astype(BF16)
        return p.astype(BF16), pc, den

    def values(u, p, pc, den, acc):
        grp, part, hh = u
        o = _dot(p, v_ref[:, grp * gw:(grp + 1) * gw])
        if has_ctx:
            o = o + _dot(pc, vc_ref[0, :, grp * gw:(grp + 1) * gw])
        if hh == 0:
            return o, jnp.broadcast_to(den, o.shape)
        mine = lax.broadcasted_iota(jnp.int32, o.shape, 1) >= hh * V_HEAD
        return jnp.where(mine, o, acc[0]), jnp.where(mine, den, acc[1])

    sc_next = scores(units[0])
    acc = None
    for i, u in enumerate(units):
        sc_cur = sc_next
        if i + 1 < len(units):
            sc_next = scores(units[i + 1])
        acc = values(u, *probs(*sc_cur), acc)
        grp, part, hh = u
        if hh == HEADS_PER_STEP - 1:
            o_ref[part * rp:(part + 1) * rp, grp * gw:(grp + 1) * gw] = (acc[0] / acc[1]).astype(BF16)


def _attn_call(l, q, k, v, kc, vc, *, nb, t, tq, ngroups):
    has_ctx = kc is not None
    nq = t // tq
    nh = ngroups * HEADS_PER_STEP
    nsteps = MLA_HEADS // nh
    in_specs = [
        pl.BlockSpec((tq, nh * HEAD_PAD), lambda b, g, qi: (b * nq + qi, g)),
        pl.BlockSpec((t, nh * HEAD_PAD), lambda b, g, qi: (b, g)),
        pl.BlockSpec((t, nh * V_HEAD), lambda b, g, qi: (b, g)),
    ]
    args = [q, k, v]
    if has_ctx:
        in_specs += [
            pl.BlockSpec((None, 1, PAST_LEN, nh * HEAD_PAD), lambda b, g, qi: (l, b, 0, g)),
            pl.BlockSpec((None, 1, PAST_LEN, nh * V_HEAD), lambda b, g, qi: (l, b, 0, g)),
        ]
        args += [kc, vc]
    return pl.pallas_call(
        functools.partial(_attn_kernel, has_ctx=has_ctx, ngroups=ngroups, nparts=ATTN_ROW_PARTS if has_ctx else 1),
        out_shape=jax.ShapeDtypeStruct((nb * t, MLA_HEADS * V_HEAD), BF16),
        grid=(nb, nsteps, nq),
        in_specs=in_specs,
        out_specs=pl.BlockSpec((tq, nh * V_HEAD), lambda b, g, qi: (b * nq + qi, g)),
        compiler_params=_params(("parallel", "parallel", "arbitrary")),
        name="attn_lat" if has_ctx else "attn_ctx",
    )(*args)


def _split3(x):
    hi = x.astype(BF16)
    r = x - hi.astype(F32)
    mid = r.astype(BF16)
    lo = (r - mid.astype(F32)).astype(BF16)
    return hi, mid, lo


def _ssd_kernel(*refs, t, has_h0):
    if has_h0:
        (z_ref, xbc_ref, dt_ref, cw_ref, cb_ref, dtb_ref, alog_ref, d0_ref, d1_ref, gn_ref, h0_ref,
         y_out, hf_out, xpad, xc, dtp_s, acs_s, acst_s, ysc, ssc, dec_s, eacs_s, hin_s, st) = refs
    else:
        (z_ref, xbc_ref, dt_ref, cw_ref, cb_ref, dtb_ref, alog_ref, d0_ref, d1_ref, gn_ref,
         y_out, hf_out, xpad, xc, dtp_s, acs_s, acst_s, ysc, ssc, dec_s, eacs_s, hin_s, st) = refs
    L = SSD_CHUNK
    W = SSD_INNER
    nc = t // L
    halo = SUBLANE

    xpad[0:halo, :] = jnp.zeros((halo, SSD_XBC), F32)
    xpad[halo + t:halo + t + halo, :] = jnp.zeros((halo, SSD_XBC), F32)
    xpad[halo:halo + t, :] = xbc_ref[...]
    cw = cw_ref[...]
    cb = cb_ref[...]
    rc = SSD_CONV_ROWS
    nwin = rc + 2 * halo

    def conv(blk, carry):
        base = pl.multiple_of(blk * rc, rc)
        win = xpad[pl.ds(base, nwin), :]
        acc = cb + cw[SSD_CONV // 2:SSD_CONV // 2 + 1, :] * win[halo:halo + rc, :]
        for kk in range(SSD_CONV):
            if kk != SSD_CONV // 2:
                first = halo - SSD_CONV // 2 + kk
                acc = acc + cw[kk:kk + 1, :] * pltpu.roll(win, nwin - first, axis=0)[0:rc, :]
        xc[pl.ds(base, rc), :] = _silu(acc)
        return carry

    lax.fori_loop(0, t // rc, conv, 0, unroll=2)
    r = CONV_ROWS

    a_all = -jnp.exp(alog_ref[...])
    dtb = dtb_ref[...]
    row_i = lax.broadcasted_iota(jnp.int32, (L, L), 0)
    col_i = lax.broadcasted_iota(jnp.int32, (L, L), 1)
    tri = [(row_i >= col_i), (row_i <= col_i)]
    tri_bf = [jnp.where(m, 1.0, 0.0).astype(BF16) for m in tri]
    grp_rows = [row_i < SSD_STATE, row_i >= SSD_STATE]
    lane_w = lax.broadcasted_iota(jnp.int32, (L, W), 1)
    own_block = (lax.broadcasted_iota(jnp.int32, (L, W), 0) // SSD_STATE) == (lane_w // LANE)
    sp_rows = lax.broadcasted_iota(jnp.int32, (3 * LANE, 2 * W), 0) % LANE
    sp_cols = lax.broadcasted_iota(jnp.int32, (3 * LANE, 2 * W), 1) // SSD_HEAD_DIM
    spread = jnp.where(sp_rows == sp_cols, 1.0, 0.0).astype(BF16)

    def chunk_rows(c):
        return pl.ds(pl.multiple_of(c * L, L), L)

    lane_lo = col_i < SSD_HEAD_DIM

    def columns(m, d):
        return [jnp.broadcast_to(m[:, j:j + 1], (L, LANE)) for j in range(d * SSD_HEADS, (d + 1) * SSD_HEADS)]

    def expand(cols):
        return jnp.concatenate([jnp.where(lane_lo, cols[0], cols[1]), jnp.where(lane_lo, cols[2], cols[3])],
                               axis=1)

    def pre(c, carry):
        rows = chunk_rows(c)
        dtp = jax.nn.softplus(dt_ref[rows, :] + dtb)
        dtp_s[rows, :] = dtp
        x3 = jnp.concatenate(_split3(dtp * a_all), axis=1)
        rf = _dot(tri_bf[0], x3)
        rb = _dot(tri_bf[1], x3)
        acs_f = rf[:, 0:LANE] + rf[:, LANE:2 * LANE] + rf[:, 2 * LANE:3 * LANE]
        acs_b = rb[:, 0:LANE] + rb[:, LANE:2 * LANE] + rb[:, 2 * LANE:3 * LANE]
        acs = jnp.where(col_i < SSD_HEADS, acs_f, acs_b)
        acs_s[rows, :] = acs
        acst_s[pl.ds(pl.multiple_of(c * SUBLANE, SUBLANE), SUBLANE), :] = acs.T[0:SUBLANE, :]
        return carry

    lax.fori_loop(0, nc, pre, 0, unroll=2)

    def main(c, carry):
        rows = chunk_rows(c)
        xs = xc[rows, 0:W]
        cm = xc[rows, W + LANE:W + 2 * LANE].astype(BF16)
        bt = xc[rows, W:W + LANE].T
        btm2 = jnp.concatenate([jnp.where(grp_rows[0], bt, 0.0), jnp.where(grp_rows[1], bt, 0.0)],
                               axis=1).astype(BF16)
        g2 = _dot(cm, btm2)
        bt_bf = bt.astype(BF16)
        dtp = dtp_s[rows, :]
        acs = acs_s[rows, :]
        acst = acst_s[pl.ds(pl.multiple_of(c * SUBLANE, SUBLANE), SUBLANE), :]
        dt_wide = _dot(jnp.concatenate(_split3(dtp), axis=1), spread)
        ysum = None
        for d in range(2):
            acs_cols = columns(acs, d)
            acs_b = expand(acs_cols)
            eacs_s[c, d] = jnp.exp(acs_b)
            xdt = xs * dt_wide[:, d * W:(d + 1) * W]
            xdt_bf = xdt.astype(BF16)
            yd = None
            for h in range(SSD_HEADS):
                j = d * SSD_HEADS + h
                g = h // (SSD_HEADS // SSD_GROUPS)
                diff = acs_cols[h] - acst[j:j + 1, :]
                seg = jnp.exp(jnp.where(tri[d], diff, -jnp.inf))
                res = _dot((g2[:, g * L:(g + 1) * L] * seg).astype(BF16), xdt_bf)
                yd = res if h == 0 else jnp.where(lane_w >= h * SSD_HEAD_DIM, res, yd)
            ysum = yd if d == 0 else ysum + yd
            tot = acs_b[L - 1:L, :] if d == 0 else acs_b[0:1, :]
            s_new = _dot(bt_bf, (xdt * jnp.exp(tot - acs_b)).astype(BF16))
            ssc[c, d] = jnp.where(own_block, s_new, 0.0)
            dec_s[c, d] = jnp.broadcast_to(jnp.exp(tot), (SUBLANE, W))
        ysc[rows, :] = ysum
        return carry

    lax.fori_loop(0, nc, main, 0, unroll=2)

    if has_h0:
        st[...] = h0_ref[0]
    else:
        st[...] = jnp.zeros(st.shape, F32)

    def rec(i, carry):
        for d, c in ((0, i), (1, nc - 1 - i)):
            h_in = st[d]
            hin_s[c, d] = h_in.astype(BF16)
            st[d] = dec_s[c, d][0:1, :] * h_in + ssc[c, d]
        return carry

    lax.fori_loop(0, nc, rec, 0)
    for d in range(2):
        st_t = st[d].T
        for h in range(SSD_HEADS):
            rows = st_t[h * SSD_HEAD_DIM:(h + 1) * SSD_HEAD_DIM, :]
            if h // (SSD_HEADS // SSD_GROUPS) == 1:
                rows = pltpu.roll(rows, LANE - SSD_STATE, axis=1)
            hf_out[0, d, h] = rows[:, 0:SSD_STATE]

    d0 = d0_ref[...]
    d1 = d1_ref[...]
    gn = gn_ref[...]

    def fin(c, carry):
        rows = chunk_rows(c)
        xs = xc[rows, 0:W]
        cm = xc[rows, W + LANE:W + 2 * LANE].astype(BF16)
        y = ysc[rows, :] + d0 * xs + d1 * xs
        for d in range(2):
            y = y + _dot(cm, hin_s[c, d]) * eacs_s[c, d]
        y = y * _silu(z_ref[rows, :])
        y_out[rows, :] = (_rms(y) * gn).astype(BF16)
        return carry

    lax.fori_loop(0, nc, fin, 0, unroll=2)


def _ssd_call(l, z, xbc, dt, cw, cb, dtb, alog, d0, d1, gn, h0, *, nb, t):
    has_h0 = h0 is not None
    nc = t // SSD_CHUNK

    def tok(b):
        return (b, 0)

    in_specs = [
        pl.BlockSpec((t, SSD_INNER), tok),
        pl.BlockSpec((t, SSD_XBC), tok),
        pl.BlockSpec((t, LANE), tok),
        _layer_spec((SUBLANE, SSD_XBC), l),
        _layer_spec((1, SSD_XBC), l),
        _layer_spec((1, LANE), l),
        _layer_spec((1, LANE), l),
        _layer_spec((1, SSD_INNER), l),
        _layer_spec((1, SSD_INNER), l),
        _layer_spec((1, SSD_INNER), l),
    ]
    args = [z, xbc, dt, cw, cb, dtb, alog, d0, d1, gn]
    if has_h0:
        in_specs.append(pl.BlockSpec((None, 1, 2, LANE, SSD_INNER), lambda b: (l, b, 0, 0, 0)))
        args.append(h0)
    return pl.pallas_call(
        functools.partial(_ssd_kernel, t=t, has_h0=has_h0),
        out_shape=[
            jax.ShapeDtypeStruct((nb * t, SSD_INNER), BF16),
            jax.ShapeDtypeStruct((nb, 2, SSD_HEADS, SSD_HEAD_DIM, SSD_STATE), F32),
        ],
        grid=(nb,),
        in_specs=in_specs,
        out_specs=[
            pl.BlockSpec((t, SSD_INNER), tok),
            pl.BlockSpec((1, 2, SSD_HEADS, SSD_HEAD_DIM, SSD_STATE), lambda b: (b, 0, 0, 0, 0)),
        ],
        scratch_shapes=[
            pltpu.VMEM((t + 2 * SUBLANE, SSD_XBC), F32),
            pltpu.VMEM((t, SSD_XBC), F32),
            pltpu.VMEM((t, LANE), F32),
            pltpu.VMEM((t, LANE), F32),
            pltpu.VMEM((nc * SUBLANE, LANE), F32),
            pltpu.VMEM((t, SSD_INNER), F32),
            pltpu.VMEM((nc, 2, LANE, SSD_INNER), F32),
            pltpu.VMEM((nc, 2, SUBLANE, SSD_INNER), F32),
            pltpu.VMEM((nc, 2, LANE, SSD_INNER), F32),
            pltpu.VMEM((nc, 2, LANE, SSD_INNER), BF16),
            pltpu.VMEM((2, LANE, SSD_INNER), F32),
        ],
        compiler_params=_params(("parallel",)),
        name="ssd_lat" if has_h0 else "ssd_ctx",
    )(*args)


CM_HALO = 16


def _glu(v):
    return v[:, 0:CM_CH] * jax.nn.sigmoid(v[:, CM_CH:2 * CM_CH])


def _conv_tile(cur_ref, prev_rows, next_rows, w, bias, lg, lb, gpad, dst, *, seg):
    tm = cur_ref.shape[0]
    halo = CM_HALO
    sub = SUBLANE
    rb = CM_ROWS
    assert halo - CM_WIDTH // 2 == 1 and CM_WIDTH <= 4 * sub
    zeros = jnp.zeros((halo, CM_CH), F32)
    for sgi in range(tm // seg):
        for piece in range(seg // CONV_ROWS):
            r0 = piece * CONV_ROWS
            gpad[sgi, halo + r0:halo + r0 + CONV_ROWS, :] = _glu(
                cur_ref[sgi * seg + r0:sgi * seg + r0 + CONV_ROWS, :])
        gpad[sgi, 0:halo, :] = zeros if prev_rows is None else prev_rows
        gpad[sgi, halo + seg:halo + seg + halo, :] = zeros if next_rows is None else next_rows

    def block(sgi, blk):
        base = blk * rb
        acc = bias
        for j in range(sub):
            pj = None
            for a in range(4):
                kk = sub * a + j
                if kk < CM_WIDTH:
                    term = w[kk:kk + 1, :] * gpad[sgi, base + sub * a:base + sub * a + rb + sub, :]
                    pj = term if pj is None else pj + term
            acc = acc + pj[j + 1:j + 1 + rb, :]
        mu = jnp.mean(acc, axis=-1, keepdims=True)
        xc = acc - mu
        var = jnp.mean(xc * xc, axis=-1, keepdims=True)
        y = xc * lax.rsqrt(var + 1e-5) * lg + lb
        dst[sgi * seg + base:sgi * seg + base + rb, :] = _silu(y).astype(BF16)

    return [functools.partial(block, sgi, blk) for sgi in range(tm // seg) for blk in range(seg // rb)]


def _ffn_conv_kernel(*refs, final, seg, tiles_per_seq):
    halos = tiles_per_seq > 1
    if halos:
        (x_ref, attn_ref, ssm_ref, cma_ref, cmb_ref, cmn_ref, mod_ref, wout_ref, gffn_ref, wg_ref, wu_ref,
         wd_ref, gfin_ref, cw_ref, cb_ref, lg_ref, lb_ref, o_ref, gpad, conv_s) = refs
    else:
        (x_ref, attn_ref, ssm_ref, cma_ref, cmb_ref, mod_ref, wout_ref, gffn_ref, wg_ref, wu_ref,
         wd_ref, gfin_ref, cw_ref, cb_ref, lg_ref, lb_ref, o_ref, gpad, conv_s) = refs
    i = pl.program_id(0)
    tm = x_ref.shape[0]
    cw = cw_ref[...]
    cbias = cb_ref[...]
    lg = lg_ref[...]
    lb = lb_ref[...]
    conv_args = dict(w=cw, bias=cbias, lg=lg, lb=lb, gpad=gpad, seg=seg)

    @pl.when(i == 0)
    def _():
        nxt = _glu(cmb_ref[0:CM_HALO, :]) if halos else None
        for emit in _conv_tile(cma_ref, None, nxt, dst=conv_s, **conv_args):
            emit()

    n_attn = MLA_HEADS * V_HEAD
    mixed = _dot(attn_ref[...], wout_ref[0:n_attn, :])
    mixed = mixed + _dot(ssm_ref[...], wout_ref[n_attn:n_attn + SSD_INNER, :])
    mixed = mixed + _dot(conv_s[...], wout_ref[n_attn + SSD_INNER:n_attn + SSD_INNER + CM_CH, :])
    g1 = mod_ref[0, :, 2 * D_MODEL:3 * D_MODEL]
    sh2 = mod_ref[0, :, 3 * D_MODEL:4 * D_MODEL]
    sc2 = mod_ref[0, :, 4 * D_MODEL:5 * D_MODEL]
    g2 = mod_ref[0, :, 5 * D_MODEL:6 * D_MODEL]
    x = x_ref[...] + g1 * mixed
    h = (_rms(x) * gffn_ref[...] * (1.0 + sc2) + sh2).astype(BF16)

    if halos:
        j = i + 1
        first = (j % tiles_per_seq) == 0
        last = (j % tiles_per_seq) == tiles_per_seq - 1
        prv = jnp.where(first, 0.0, _glu(cma_ref[tm - CM_HALO:tm, :]))
        nxt = jnp.where(last, 0.0, _glu(cmn_ref[...]))
    else:
        prv = nxt = None
    conv_blocks = _conv_tile(cmb_ref, prv, nxt, dst=conv_s, **conv_args)

    nchunk = D_FF // FF_CHUNK
    ff = jnp.zeros(x.shape, F32)
    for jc in range(nchunk):
        cols = slice(jc * FF_CHUNK, (jc + 1) * FF_CHUNK)
        gate = _dot(h, wg_ref[:, cols])
        up = _dot(h, wu_ref[:, cols])
        ff = ff + _dot((_silu(gate) * up).astype(BF16), wd_ref[cols, :])
        for emit in conv_blocks[jc * len(conv_blocks) // nchunk:(jc + 1) * len(conv_blocks) // nchunk]:
            emit()
    x = x + g2 * ff
    if final:
        x = _rms(x) * gfin_ref[...]
    o_ref[...] = x


def _ffn_conv_call(l, x, attn, ssm, cm, mod, wout, gffn, wg, wu, wd, gfin, cw, cb, lg, lb, *,
                   t, tmod, mod_row0, final):
    n = x.shape[0]
    tm = TM_FFN
    nt = n // tm
    seg = min(t, tm)
    tiles_per_seq = t // seg
    halos = tiles_per_seq > 1
    hb = tm // CM_HALO

    def tok(i):
        return (i, 0)

    in_specs = [
        pl.BlockSpec((tm, D_MODEL), tok),
        pl.BlockSpec((tm, MLA_HEADS * V_HEAD), tok),
        pl.BlockSpec((tm, SSD_INNER), tok),
        pl.BlockSpec((tm, 2 * CM_CH), tok),
        pl.BlockSpec((tm, 2 * CM_CH), lambda i: (jnp.minimum(i + 1, nt - 1), 0)),
    ]
    args = [x, attn, ssm, cm, cm]
    if halos:
        in_specs.append(pl.BlockSpec((CM_HALO, 2 * CM_CH), lambda i: (jnp.minimum((i + 2) * hb, nt * hb - 1), 0)))
        args.append(cm)
    in_specs += [
        _mod_spec(l, tm, tmod, mod_row0),
        _layer_spec((D_MODEL, D_MODEL), l, single_buffer=True),
        _layer_spec((1, D_MODEL), l),
        _layer_spec((D_MODEL, D_FF), l, single_buffer=True),
        _layer_spec((D_MODEL, D_FF), l, single_buffer=True),
        _layer_spec((D_FF, D_MODEL), l, single_buffer=True),
        pl.BlockSpec((1, D_MODEL), lambda i: (0, 0)),
        _layer_spec((4 * SUBLANE, CM_CH), l),
        _layer_spec((1, CM_CH), l),
        _layer_spec((1, CM_CH), l),
        _layer_spec((1, CM_CH), l),
    ]
    args += [mod, wout, gffn, wg, wu, wd, gfin, cw, cb, lg, lb]
    return pl.pallas_call(
        functools.partial(_ffn_conv_kernel, final=final, seg=seg, tiles_per_seq=tiles_per_seq),
        out_shape=jax.ShapeDtypeStruct((n, D_MODEL), F32),
        grid=(nt,),
        in_specs=in_specs,
        out_specs=pl.BlockSpec((tm, D_MODEL), tok),
        scratch_shapes=[
            pltpu.VMEM((tm // seg, seg + 2 * CM_HALO, CM_CH), F32),
            pltpu.VMEM((tm, CM_CH), BF16),
        ],
        compiler_params=_params(("arbitrary",)),
        name="ffn_conv",
    )(*args)


def _rope_tables():
    rows = DEC_SEQ // GRID_W
    row = jnp.repeat(jnp.arange(rows), GRID_W).astype(F32)
    col = (jnp.arange(rows * GRID_W) % GRID_W).astype(F32)
    nf = QK_ROPE // 4
    inv = ROPE_THETA ** (-jnp.arange(nf, dtype=F32) / nf)
    ang = jnp.stack([row[:, None] * inv, col[:, None] * inv], axis=1)
    cos = jnp.cos(ang)
    sin = jnp.sin(ang)
    cos32 = jnp.stack([cos, cos], axis=2).reshape(DEC_SEQ, QK_ROPE)
    sin32 = jnp.stack([-sin, sin], axis=2).reshape(DEC_SEQ, QK_ROPE)
    ones = jnp.ones((DEC_SEQ, QK_NOPE), F32)
    pad1 = jnp.ones((DEC_SEQ, HEAD_PAD - QK_HEAD), F32)
    cos_t = jnp.concatenate([ones, cos32, pad1], axis=1)
    sin_t = jnp.concatenate([0.0 * ones, sin32, 0.0 * pad1], axis=1)
    return cos_t, sin_t


def _kr_placement():
    src = jnp.arange(QK_ROPE)
    cols = jnp.arange(MLA_HEADS * HEAD_PAD)
    hit = (cols[None, :] % HEAD_PAD) == (QK_NOPE + src[:, None])
    return hit.astype(BF16)


def _pack_states(h0):
    lead = h0.shape[:-3]
    s = jnp.moveaxis(h0, -1, -3).reshape(lead + (SSD_STATE, SSD_INNER))
    lane_group = jnp.arange(SSD_INNER) // LANE
    parts = [jnp.where(lane_group == g, s, 0.0) for g in range(SSD_GROUPS)]
    return jnp.concatenate(parts, axis=-2)


def kernel(x_prompt, x_sample, c, cache_ckv, cache_krope, state_ssd, c_ctx, w_ada, b_ada, g_mix, w_in, g_q, w_uq, g_kv, w_ukv, ssd_conv_w, ssd_conv_b, ssd_dt_bias, ssd_a_log, ssd_d, ssd_norm_g, cm_conv_w, cm_conv_b, cm_ln_g, cm_ln_b, w_out, g_ffn, w_gate, w_up, w_down, g_final):
    L = DEPTH
    w_in_pad = jnp.pad(w_in, ((0, 0), (0, 0), (0, IN_PAD - IN_WIDTH))).astype(BF16)
    win = _place_call(w_in_pad, _src_in(), "place_w_in", bounds=(0, C_DT, C_KR, C_Z, C_CM, IN_ARR))
    wuq = _place_call(w_uq, _src_uq(), "place_w_uq")
    wukv = _place_call(w_ukv, _src_ukv(), "place_w_ukv")
    wout = w_out.astype(BF16)
    wg = w_gate.astype(BF16)
    wu = w_up.astype(BF16)
    wd = w_down.astype(BF16)
    rope_tabs = _rope_tables()

    cond8 = jnp.concatenate([c_ctx[None, :], c, jnp.zeros((8 - 1 - DEC_BATCH, D_MODEL), F32)], axis=0)
    mod = _ada_call(cond8, w_ada, b_ada)
    mod = mod.reshape(L, 8, 1, N_MOD * D_MODEL)

    kc_all, vc_all = _ctxkv_call(cache_ckv, cache_krope, wukv, _kr_placement())

    ssd_cw = jnp.concatenate([ssd_conv_w, jnp.zeros((L, SUBLANE - SSD_CONV, SSD_XBC), F32)], axis=1)
    cm_w = jnp.concatenate([cm_conv_w, jnp.zeros((L, 4 * SUBLANE - CM_WIDTH, CM_CH), F32)], axis=1)
    lane_pad = lambda v: jnp.concatenate(
        [v.reshape(L, 1, 2 * SSD_HEADS), jnp.zeros((L, 1, LANE - 2 * SSD_HEADS), F32)], axis=-1)
    dtb = lane_pad(ssd_dt_bias)
    alog = lane_pad(ssd_a_log)
    d_rep = jnp.repeat(ssd_d, SSD_HEAD_DIM, axis=-1)
    d0 = d_rep[:, 0:1, :]
    d1 = d_rep[:, 1:2, :]
    h0_lat = _pack_states(jnp.swapaxes(state_ssd, 0, 1))
    gmix, gq, gkv, gffn = _vec(g_mix), _vec(g_q), _vec(g_kv), _vec(g_ffn)
    scb, sng = _vec(ssd_conv_b), _vec(ssd_norm_g)
    ccb, clg, clb = _vec(cm_conv_b), _vec(cm_ln_g), _vec(cm_ln_b)
    gfin = g_final.reshape(1, D_MODEL)

    streams = [
        dict(x=x_prompt.reshape(N_CTX, D_MODEL), nb=BATCH, t=SEQ, tmod=N_CTX, mod_row0=0, tq=SEQ, ngroups=2,
             rope=None),
        dict(x=x_sample.reshape(N_LAT, D_MODEL), nb=DEC_BATCH, t=DEC_SEQ, tmod=DEC_SEQ, mod_row0=1, tq=TQ_LAT,
             ngroups=1, rope=rope_tabs),
    ]
    ckvs, krs, hss = [], [], []
    for l in range(L):
        for si, s in enumerate(streams):
            lat = si == 1
            nb, t = s["nb"], s["t"]
            q, k, v, ckv, kr, z, xbc, dt, cm = _inproj_call(
                l, s["x"], mod, gmix, win, gq, wuq, gkv, wukv, s["rope"],
                t=t, tmod=s["tmod"], mod_row0=s["mod_row0"])
            attn = _attn_call(l, q, k, v, kc_all if lat else None, vc_all if lat else None,
                              nb=nb, t=t, tq=s["tq"], ngroups=s["ngroups"])
            ssm, hf = _ssd_call(l, z, xbc, dt, ssd_cw, scb, dtb, alog, d0, d1, sng,
                                h0_lat if lat else None, nb=nb, t=t)
            s["x"] = _ffn_conv_call(l, s["x"], attn, ssm, cm, mod, wout, gffn, wg, wu, wd, gfin,
                                    cm_w, ccb, clg, clb, t=t, tmod=s["tmod"], mod_row0=s["mod_row0"],
                                    final=(l == L - 1))
            if not lat:
                ckvs.append(ckv.reshape(BATCH, SEQ, KV_RANK))
                krs.append(kr[:, QK_NOPE:QK_HEAD].reshape(BATCH, SEQ, QK_ROPE))
                hss.append(hf)

    y_prompt = streams[0]["x"].reshape(BATCH, SEQ, D_MODEL)
    y_sample = streams[1]["x"].reshape(DEC_BATCH, DEC_SEQ, D_MODEL)
    return (y_prompt, y_sample, jnp.stack(ckvs, axis=1), jnp.stack(krs, axis=1), jnp.stack(hss, axis=1))
```

```python
import functools

import numpy as np

import jax
import jax.numpy as jnp
from jax import lax
from jax.experimental import pallas as pl
from jax.experimental.pallas import tpu as pltpu

F32 = jnp.float32
BF16 = jnp.bfloat16

D_MODEL = 1024
BATCH = 16
SEQ = 256
DEPTH = 4
DEC_BATCH = 4
DEC_SEQ = 2048
PAST_LEN = 256
GRID_W = 64

MLA_HEADS = 8
QK_NOPE = 64
QK_ROPE = 32
QK_HEAD = QK_NOPE + QK_ROPE
V_HEAD = 64
Q_RANK = 256
KV_RANK = 128
ROPE_THETA = 10000.0
LOG2_E = 1.4426950408889634

SSD_HEADS = 4
SSD_HEAD_DIM = 64
SSD_INNER = SSD_HEADS * SSD_HEAD_DIM
SSD_GROUPS = 2
SSD_STATE = 64
SSD_CONV = 5
SSD_CHUNK = 128
SSD_XBC = SSD_INNER + 2 * SSD_GROUPS * SSD_STATE

CM_CH = 256
CM_WIDTH = 31

D_FF = 2816
N_MOD = 6
IN_WIDTH = (Q_RANK + KV_RANK + QK_ROPE) + (SSD_INNER + SSD_XBC + 2 * SSD_HEADS) + 2 * CM_CH

N_CTX = BATCH * SEQ
N_LAT = DEC_BATCH * DEC_SEQ

SUBLANE = 8
LANE = 128
MXU_N = 256
HEAD_PAD = 128
HEADS_PER_STEP = MXU_N // V_HEAD
VMEM_LIMIT = 48 * 1024 * 1024

ADA_COLS = 3072
TM_IN = 512
TM_FFN = 512
TQ_LAT = 1024
FF_CHUNK = 256
CONV_ROWS = 128
CM_ROWS = 64
SSD_CONV_ROWS = 32
ATTN_ROW_PARTS = 4

C_QLAT = 0
C_CKV = C_QLAT + Q_RANK
C_DT = C_CKV + KV_RANK
C_KR = C_DT + LANE
C_Z = C_KR + 2 * HEAD_PAD
C_XBC = C_Z + SSD_INNER
C_CM = C_XBC + SSD_XBC
IN_ARR = C_CM + 2 * CM_CH
IN_PAD = -(-IN_WIDTH // LANE) * LANE


def _params(sem):
    return pltpu.CompilerParams(dimension_semantics=sem, vmem_limit_bytes=VMEM_LIMIT)


def _dot(a, b):
    return jnp.dot(a, b, preferred_element_type=F32)


def _dot_nt(a, b):
    return lax.dot_general(a, b, (((1,), (1,)), ((), ())), preferred_element_type=F32)


def _rms(x, eps=1e-6):
    return x * lax.rsqrt(jnp.mean(x * x, axis=-1, keepdims=True) + eps)


def _silu(x):
    return x * jax.nn.sigmoid(x)


def _layer_spec(shape, l, single_buffer=False):
    nd = len(shape)
    kw = dict(pipeline_mode=pl.Buffered(1)) if single_buffer else {}
    return pl.BlockSpec((None,) + tuple(shape), lambda *_: (l,) + (0,) * nd, **kw)


def _vec(v):
    return v.reshape(DEPTH, 1, -1)


def _place_kernel(w_ref, p_ref, o_ref, *, groups):
    for dlo, dhi, slo, shi in groups:
        if slo is None:
            o_ref[0, :, dlo:dhi] = w_ref[0, :, dlo:dhi].astype(BF16)
        else:
            o_ref[0, :, dlo:dhi] = _dot(w_ref[0, :, slo:shi].astype(BF16), p_ref[slo:shi, dlo:dhi]).astype(BF16)


def _place_groups(src, cin, bounds):
    groups = []
    for dlo, dhi in zip(bounds[:-1], bounds[1:]):
        s = src[dlo:dhi]
        if np.array_equal(s, np.arange(dlo, dhi)):
            groups.append((dlo, dhi, None, None))
            continue
        used = s[s >= 0]
        slo = int(used.min()) // LANE * LANE
        shi = min(-(-(int(used.max()) + 1) // LANE) * LANE, cin)
        groups.append((dlo, dhi, slo, shi))
    return tuple(groups)


def _place_call(w, src, name, bounds=None):
    L, rows, cin = w.shape
    cout = src.shape[0]
    place = (jnp.arange(cin, dtype=jnp.int32)[:, None] == jnp.asarray(src, jnp.int32)[None, :]).astype(BF16)
    tr = min(rows, 256)
    groups = _place_groups(src, cin, bounds if bounds is not None else (0, cout))
    return pl.pallas_call(
        functools.partial(_place_kernel, groups=groups),
        out_shape=jax.ShapeDtypeStruct((L, rows, cout), BF16),
        grid=(L, rows // tr),
        in_specs=[
            pl.BlockSpec((1, tr, cin), lambda l, i: (l, i, 0)),
            pl.BlockSpec((cin, cout), lambda l, i: (0, 0)),
        ],
        out_specs=pl.BlockSpec((1, tr, cout), lambda l, i: (l, i, 0)),
        compiler_params=_params(("parallel", "parallel")),
        name=name,
    )(w, place)


def _swap32():
    return np.arange(QK_ROPE).reshape(2, 2, QK_ROPE // 4)[:, ::-1, :].reshape(-1)


def _src_in():
    kr0 = Q_RANK + KV_RANK
    off_ssd = kr0 + QK_ROPE
    off_dt = off_ssd + SSD_INNER + SSD_XBC
    off_cm = off_dt + 2 * SSD_HEADS
    src = -np.ones((IN_ARR,), np.int64)
    src[0:C_DT] = np.arange(0, kr0)
    src[C_DT:C_DT + 2 * SSD_HEADS] = np.arange(off_dt, off_cm)
    src[C_KR + QK_NOPE:C_KR + QK_HEAD] = kr0 + np.arange(QK_ROPE)
    src[C_KR + HEAD_PAD + QK_NOPE:C_KR + HEAD_PAD + QK_HEAD] = kr0 + _swap32()
    src[C_Z:C_CM] = np.arange(off_ssd, off_dt)
    src[C_CM:IN_ARR] = np.arange(off_cm, IN_WIDTH)
    return src


def _src_uq():
    half = MLA_HEADS * HEAD_PAD
    src = -np.ones((2 * half,), np.int64)
    for h in range(MLA_HEADS):
        src[h * HEAD_PAD:h * HEAD_PAD + QK_HEAD] = h * QK_HEAD + np.arange(QK_HEAD)
        lo = half + h * HEAD_PAD + QK_NOPE
        src[lo:lo + QK_ROPE] = h * QK_HEAD + QK_NOPE + _swap32()
    return src


def _src_ukv():
    half = MLA_HEADS * HEAD_PAD
    src = -np.ones((half + MLA_HEADS * V_HEAD,), np.int64)
    for h in range(MLA_HEADS):
        base = h * (QK_NOPE + V_HEAD)
        src[h * HEAD_PAD:h * HEAD_PAD + QK_NOPE] = base + np.arange(QK_NOPE)
        src[half + h * V_HEAD:half + (h + 1) * V_HEAD] = base + QK_NOPE + np.arange(V_HEAD)
    return src


def _ada_kernel(cond_ref, w_ref, b_ref, o_ref):
    c = cond_ref[...]
    a = _silu(c).astype(BF16)
    o_ref[0] = _dot(a, w_ref[0].astype(BF16)) + b_ref[0]


def _ada_call(cond8, w_ada, b_ada):
    tn = ADA_COLS
    nt = (N_MOD * D_MODEL) // tn
    return pl.pallas_call(
        _ada_kernel,
        out_shape=jax.ShapeDtypeStruct((DEPTH, 8, N_MOD * D_MODEL), F32),
        grid=(DEPTH, nt),
        in_specs=[
            pl.BlockSpec((8, D_MODEL), lambda l, j: (0, 0)),
            pl.BlockSpec((1, D_MODEL, tn), lambda l, j: (l, 0, j)),
            pl.BlockSpec((1, 1, tn), lambda l, j: (l, 0, j)),
        ],
        out_specs=pl.BlockSpec((1, 8, tn), lambda l, j: (l, 0, j)),
        compiler_params=_params(("parallel", "parallel")),
        name="ada_mod",
    )(cond8, w_ada, b_ada.reshape(DEPTH, 1, N_MOD * D_MODEL))


def _mod_spec(l, tm, tmod, mod_row0):
    return pl.BlockSpec((None, 1, 1, N_MOD * D_MODEL), lambda i: (l, mod_row0 + (i * tm) // tmod, 0, 0))


def _inproj_kernel(*refs, rope):
    if rope:
        (x_ref, mod_ref, gmix_ref, win_ref, gq_ref, wuq_ref, gkv_ref, wukv_ref, cos_ref, sin_ref,
         q_out, k_out, v_out, ckv_out, kr_out, z_out, xbc_out, dt_out, cm_out) = refs
        cos = cos_ref[...]
        sin = sin_ref[...]
    else:
        (x_ref, mod_ref, gmix_ref, win_ref, gq_ref, wuq_ref, gkv_ref, wukv_ref,
         q_out, k_out, v_out, ckv_out, kr_out, z_out, xbc_out, dt_out, cm_out) = refs
    x = x_ref[...]
    sh1 = mod_ref[0, :, 0:D_MODEL]
    sc1 = mod_ref[0, :, D_MODEL:2 * D_MODEL]
    h = (_rms(x) * gmix_ref[...] * (1.0 + sc1) + sh1).astype(BF16)

    scale = QK_HEAD ** -0.5 * LOG2_E
    half = MLA_HEADS * HEAD_PAD
    q_lat = _dot(h, win_ref[:, C_QLAT:C_QLAT + Q_RANK])
    ncol = C_KR - C_CKV + (2 if rope else 1) * HEAD_PAD
    mid = _dot(h, win_ref[:, C_CKV:C_CKV + ncol])
    z_out[...] = _dot(h, win_ref[:, C_Z:C_Z + SSD_INNER])

    qn = (_rms(q_lat) * gq_ref[...]).astype(BF16)
    qq = _dot(qn, wuq_ref[...])
    ckv = _rms(mid[:, 0:KV_RANK]) * gkv_ref[...]
    ckv_out[...] = ckv
    kv = _dot(ckv.astype(BF16), wukv_ref[...])
    xbc_out[...] = _dot(h, win_ref[:, C_XBC:C_XBC + SSD_XBC])

    for hd in range(MLA_HEADS):
        lo = hd * HEAD_PAD
        qh = qq[:, lo:lo + HEAD_PAD]
        if rope:
            qh = qh * cos + qq[:, half + lo:half + lo + HEAD_PAD] * sin
        q_out[:, lo:lo + HEAD_PAD] = (qh * scale).astype(BF16)
    cm_out[...] = _dot(h, win_ref[:, C_CM:C_CM + 2 * CM_CH])

    dt_out[...] = mid[:, C_DT - C_CKV:C_DT - C_CKV + LANE]
    kr = mid[:, C_KR - C_CKV:C_KR - C_CKV + HEAD_PAD]
    kr_out[...] = kr
    kr_rot = kr
    if rope:
        kr_rot = kr * cos + mid[:, C_KR - C_CKV + HEAD_PAD:C_KR - C_CKV + 2 * HEAD_PAD] * sin
    for hd in range(MLA_HEADS):
        lo = hd * HEAD_PAD
        k_out[:, lo:lo + HEAD_PAD] = (kv[:, lo:lo + HEAD_PAD] + kr_rot).astype(BF16)
    v_out[...] = kv[:, half:half + MLA_HEADS * V_HEAD].astype(BF16)


def _inproj_call(l, x, mod, gmix, win, gq, wuq, gkv, wukv, rope_tabs, *, t, tmod, mod_row0):
    n = x.shape[0]
    tm = TM_IN
    rope = rope_tabs is not None
    half = MLA_HEADS * HEAD_PAD

    def tok(i):
        return (i, 0)

    outs = [
        ((n, half), BF16),
        ((n, half), BF16),
        ((n, MLA_HEADS * V_HEAD), BF16),
        ((n, KV_RANK), F32),
        ((n, HEAD_PAD), F32),
        ((n, SSD_INNER), F32),
        ((n, SSD_XBC), F32),
        ((n, LANE), F32),
        ((n, 2 * CM_CH), F32),
    ]
    in_specs = [
        pl.BlockSpec((tm, D_MODEL), tok),
        _mod_spec(l, tm, tmod, mod_row0),
        _layer_spec((1, D_MODEL), l),
        _layer_spec((D_MODEL, IN_ARR), l, single_buffer=True),
        _layer_spec((1, Q_RANK), l),
        _layer_spec((Q_RANK, 2 * half if rope else half), l, single_buffer=True),
        _layer_spec((1, KV_RANK), l),
        _layer_spec((KV_RANK, MLA_HEADS * (HEAD_PAD + V_HEAD)), l, single_buffer=True),
    ]
    args = [x, mod, gmix, win, gq, wuq, gkv, wukv]
    if rope:
        pos = lambda i: (i % (t // tm), 0)
        in_specs += [pl.BlockSpec((tm, HEAD_PAD), pos), pl.BlockSpec((tm, HEAD_PAD), pos)]
        args += list(rope_tabs)
    return pl.pallas_call(
        functools.partial(_inproj_kernel, rope=rope),
        out_shape=[jax.ShapeDtypeStruct(s, d) for s, d in outs],
        grid=(n // tm,),
        in_specs=in_specs,
        out_specs=[pl.BlockSpec((tm, s[1]), tok) for s, _ in outs],
        compiler_params=_params(("parallel",)),
        name="in_proj_lat" if rope else "in_proj_ctx",
    )(*args)


def _ctxkv_kernel(ckv_ref, kr_ref, wukv_ref, place_ref, k_out, v_out):
    kv = _dot(ckv_ref[0, 0].astype(BF16), wukv_ref[0])
    krp = _dot(kr_ref[0, 0].astype(BF16), place_ref[...])
    half = MLA_HEADS * HEAD_PAD
    k_out[0, 0] = (kv[:, 0:half] + krp).astype(BF16)
    v_out[0, 0] = kv[:, half:half + MLA_HEADS * V_HEAD].astype(BF16)


def _ctxkv_call(cache_ckv, cache_krope, wukv_all, place):
    return pl.pallas_call(
        _ctxkv_kernel,
        out_shape=[
            jax.ShapeDtypeStruct((DEPTH, DEC_BATCH, PAST_LEN, MLA_HEADS * HEAD_PAD), BF16),
            jax.ShapeDtypeStruct((DEPTH, DEC_BATCH, PAST_LEN, MLA_HEADS * V_HEAD), BF16),
        ],
        grid=(DEPTH, DEC_BATCH),
        in_specs=[
            pl.BlockSpec((1, 1, PAST_LEN, KV_RANK), lambda l, b: (b, l, 0, 0)),
            pl.BlockSpec((1, 1, PAST_LEN, QK_ROPE), lambda l, b: (b, l, 0, 0)),
            pl.BlockSpec((1, KV_RANK, MLA_HEADS * (HEAD_PAD + V_HEAD)), lambda l, b: (l, 0, 0)),
            pl.BlockSpec((QK_ROPE, MLA_HEADS * HEAD_PAD), lambda l, b: (0, 0)),
        ],
        out_specs=[
            pl.BlockSpec((1, 1, PAST_LEN, MLA_HEADS * HEAD_PAD), lambda l, b: (l, b, 0, 0)),
            pl.BlockSpec((1, 1, PAST_LEN, MLA_HEADS * V_HEAD), lambda l, b: (l, b, 0, 0)),
        ],
        compiler_params=_params(("parallel", "parallel")),
        name="ctx_kv",
    )(cache_ckv, cache_krope, wukv_all, place)


def _attn_kernel(*refs, has_ctx, ngroups, nparts):
    if has_ctx:
        q_ref, k_ref, v_ref, kc_ref, vc_ref, o_ref = refs
    else:
        q_ref, k_ref, v_ref, o_ref = refs
    gw = HEADS_PER_STEP * V_HEAD
    tq = q_ref.shape[0]
    rp = tq // nparts
    units = [(grp, part, hh) for grp in range(ngroups) for part in range(nparts) for hh in range(HEADS_PER_STEP)]

    def scores(u):
        grp, part, hh = u
        lo = (grp * HEADS_PER_STEP + hh) * HEAD_PAD
        qh = q_ref[part * rp:(part + 1) * rp, lo:lo + HEAD_PAD]
        s = _dot_nt(qh, k_ref[:, lo:lo + HEAD_PAD])
        sc = _dot_nt(qh, kc_ref[0, :, lo:lo + HEAD_PAD]) if has_ctx else None
        return s, sc

    def probs(s, sc):
        m = jnp.max(s, axis=-1, keepdims=True)
        if has_ctx:
            m = jnp.maximum(m, jnp.max(sc, axis=-1, keepdims=True))
        p = jnp.exp2(s - m)
        den = jnp.sum(p, axis=-1, keepdims=True)
        pc = None
        if has_ctx:
            pc = jnp.exp2(sc - m)
            den = den + jnp.sum(pc, axis=-1, keepdims=True)
            pc = pc.astype(BF16)
        return p.astype(BF16), pc, den

    def values(u, p, pc, den, acc):
        grp, part, hh = u
        o = _dot(p, v_ref[:, grp * gw:(grp + 1) * gw])
        if has_ctx:
            o = o + _dot(pc, vc_ref[0, :, grp * gw:(grp + 1) * gw])
        if hh == 0:
            return o, jnp.broadcast_to(den, o.shape)
        mine = lax.broadcasted_iota(jnp.int32, o.shape, 1) >= hh * V_HEAD
        return jnp.where(mine, o, acc[0]), jnp.where(mine, den, acc[1])

    sc_next = scores(units[0])
    acc = None
    for i, u in enumerate(units):
        sc_cur = sc_next
        if i + 1 < len(units):
            sc_next = scores(units[i + 1])
        acc = values(u, *probs(*sc_cur), acc)
        grp, part, hh = u
        if hh == HEADS_PER_STEP - 1:
            o_ref[part * rp:(part + 1) * rp, grp * gw:(grp + 1) * gw] = (acc[0] / acc[1]).astype(BF16)


def _attn_call(l, q, k, v, kc, vc, *, nb, t, tq, ngroups):
    has_ctx = kc is not None
    nq = t // tq
    nh = ngroups * HEADS_PER_STEP
    nsteps = MLA_HEADS // nh
    in_specs = [
        pl.BlockSpec((tq, nh * HEAD_PAD), lambda b, g, qi: (b * nq + qi, g)),
        pl.BlockSpec((t, nh * HEAD_PAD), lambda b, g, qi: (b, g)),
        pl.BlockSpec((t, nh * V_HEAD), lambda b, g, qi: (b, g)),
    ]
    args = [q, k, v]
    if has_ctx:
        in_specs += [
            pl.BlockSpec((None, 1, PAST_LEN, nh * HEAD_PAD), lambda b, g, qi: (l, b, 0, g)),
            pl.BlockSpec((None, 1, PAST_LEN, nh * V_HEAD), lambda b, g, qi: (l, b, 0, g)),
        ]
        args += [kc, vc]
    return pl.pallas_call(
        functools.partial(_attn_kernel, has_ctx=has_ctx, ngroups=ngroups, nparts=ATTN_ROW_PARTS if has_ctx else 1),
        out_shape=jax.ShapeDtypeStruct((nb * t, MLA_HEADS * V_HEAD), BF16),
        grid=(nb, nsteps, nq),
        in_specs=in_specs,
        out_specs=pl.BlockSpec((tq, nh * V_HEAD), lambda b, g, qi: (b * nq + qi, g)),
        compiler_params=_params(("parallel", "parallel", "arbitrary")),
        name="attn_lat" if has_ctx else "attn_ctx",
    )(*args)


def _split3(x):
    hi = x.astype(BF16)
    r = x - hi.astype(F32)
    mid = r.astype(BF16)
    lo = (r - mid.astype(F32)).astype(BF16)
    return hi, mid, lo


def _ssd_kernel(*refs, t, has_h0):
    if has_h0:
        (z_ref, xbc_ref, dt_ref, cw_ref, cb_ref, dtb_ref, alog_ref, d0_ref, d1_ref, gn_ref, h0_ref,
         y_out, hf_out, xpad, xc, dtp_s, acs_s, acst_s, ysc, ssc, dec_s, eacs_s, hin_s, st) = refs
    else:
        (z_ref, xbc_ref, dt_ref, cw_ref, cb_ref, dtb_ref, alog_ref, d0_ref, d1_ref, gn_ref,
         y_out, hf_out, xpad, xc, dtp_s, acs_s, acst_s, ysc, ssc, dec_s, eacs_s, hin_s, st) = refs
    L = SSD_CHUNK
    W = SSD_INNER
    nc = t // L
    halo = SUBLANE

    xpad[0:halo, :] = jnp.zeros((halo, SSD_XBC), F32)
    xpad[halo + t:halo + t + halo, :] = jnp.zeros((halo, SSD_XBC), F32)
    xpad[halo:halo + t, :] = xbc_ref[...]
    cw = cw_ref[...]
    cb = cb_ref[...]
    rc = SSD_CONV_ROWS
    nwin = rc + 2 * halo

    def conv(blk, carry):
        base = pl.multiple_of(blk * rc, rc)
        win = xpad[pl.ds(base, nwin), :]
        acc = cb + cw[SSD_CONV // 2:SSD_CONV // 2 + 1, :] * win[halo:halo + rc, :]
        for kk in range(SSD_CONV):
            if kk != SSD_CONV // 2:
                first = halo - SSD_CONV // 2 + kk
                acc = acc + cw[kk:kk + 1, :] * pltpu.roll(win, nwin - first, axis=0)[0:rc, :]
        xc[pl.ds(base, rc), :] = _silu(acc)
        return carry

    lax.fori_loop(0, t // rc, conv, 0, unroll=2)
    r = CONV_ROWS

    a_all = -jnp.exp(alog_ref[...])
    dtb = dtb_ref[...]
    row_i = lax.broadcasted_iota(jnp.int32, (L, L), 0)
    col_i = lax.broadcasted_iota(jnp.int32, (L, L), 1)
    tri = [(row_i >= col_i), (row_i <= col_i)]
    tri_bf = [jnp.where(m, 1.0, 0.0).astype(BF16) for m in tri]
    grp_rows = [row_i < SSD_STATE, row_i >= SSD_STATE]
    lane_w = lax.broadcasted_iota(jnp.int32, (L, W), 1)
    own_block = (lax.broadcasted_iota(jnp.int32, (L, W), 0) // SSD_STATE) == (lane_w // LANE)
    sp_rows = lax.broadcasted_iota(jnp.int32, (3 * LANE, 2 * W), 0) % LANE
    sp_cols = lax.broadcasted_iota(jnp.int32, (3 * LANE, 2 * W), 1) // SSD_HEAD_DIM
    spread = jnp.where(sp_rows == sp_cols, 1.0, 0.0).astype(BF16)

    def chunk_rows(c):
        return pl.ds(pl.multiple_of(c * L, L), L)

    lane_lo = col_i < SSD_HEAD_DIM

    def columns(m, d):
        return [jnp.broadcast_to(m[:, j:j + 1], (L, LANE)) for j in range(d * SSD_HEADS, (d + 1) * SSD_HEADS)]

    def expand(cols):
        return jnp.concatenate([jnp.where(lane_lo, cols[0], cols[1]), jnp.where(lane_lo, cols[2], cols[3])],
                               axis=1)

    def pre(c, carry):
        rows = chunk_rows(c)
        dtp = jax.nn.softplus(dt_ref[rows, :] + dtb)
        dtp_s[rows, :] = dtp
        x3 = jnp.concatenate(_split3(dtp * a_all), axis=1)
        rf = _dot(tri_bf[0], x3)
        rb = _dot(tri_bf[1], x3)
        acs_f = rf[:, 0:LANE] + rf[:, LANE:2 * LANE] + rf[:, 2 * LANE:3 * LANE]
        acs_b = rb[:, 0:LANE] + rb[:, LANE:2 * LANE] + rb[:, 2 * LANE:3 * LANE]
        acs = jnp.where(col_i < SSD_HEADS, acs_f, acs_b)
        acs_s[rows, :] = acs
        acst_s[pl.ds(pl.multiple_of(c * SUBLANE, SUBLANE), SUBLANE), :] = acs.T[0:SUBLANE, :]
        return carry

    lax.fori_loop(0, nc, pre, 0, unroll=min(nc, 4))

    def main(c, carry):
        rows = chunk_rows(c)
        xs = xc[rows, 0:W]
        cm = xc[rows, W + LANE:W + 2 * LANE].astype(BF16)
        bt = xc[rows, W:W + LANE].T
        btm2 = jnp.concatenate([jnp.where(grp_rows[0], bt, 0.0), jnp.where(grp_rows[1], bt, 0.0)],
                               axis=1).astype(BF16)
        g2 = _dot(cm, btm2)
        bt_bf = bt.astype(BF16)
        dtp = dtp_s[rows, :]
        acs = acs_s[rows, :]
        acst = acst_s[pl.ds(pl.multiple_of(c * SUBLANE, SUBLANE), SUBLANE), :]
        dt_wide = _dot(jnp.concatenate(_split3(dtp), axis=1), spread)
        ysum = None
        for d in range(2):
            acs_cols = columns(acs, d)
            acs_b = expand(acs_cols)
            eacs_s[c, d] = jnp.exp(acs_b)
            xdt = xs * dt_wide[:, d * W:(d + 1) * W]
            xdt_bf = xdt.astype(BF16)
            yd = None
            for h in range(SSD_HEADS):
                j = d * SSD_HEADS + h
                g = h // (SSD_HEADS // SSD_GROUPS)
                diff = acs_cols[h] - acst[j:j + 1, :]
                seg = jnp.exp(jnp.where(tri[d], diff, -jnp.inf))
                res = _dot((g2[:, g * L:(g + 1) * L] * seg).astype(BF16), xdt_bf)
                yd = res if h == 0 else jnp.where(lane_w >= h * SSD_HEAD_DIM, res, yd)
            ysum = yd if d == 0 else ysum + yd
            tot = acs_b[L - 1:L, :] if d == 0 else acs_b[0:1, :]
            s_new = _dot(bt_bf, (xdt * jnp.exp(tot - acs_b)).astype(BF16))
            ssc[c, d] = jnp.where(own_block, s_new, 0.0)
            dec_s[c, d] = jnp.broadcast_to(jnp.exp(tot), (SUBLANE, W))
        ysc[rows, :] = ysum
        return carry

    lax.fori_loop(0, nc, main, 0, unroll=2)

    if has_h0:
        st[...] = h0_ref[0]
    else:
        st[...] = jnp.zeros(st.shape, F32)

    def rec(i, carry):
        for d, c in ((0, i), (1, nc - 1 - i)):
            h_in = st[d]
            hin_s[c, d] = h_in.astype(BF16)
            st[d] = dec_s[c, d][0:1, :] * h_in + ssc[c, d]
        return carry

    lax.fori_loop(0, nc, rec, 0)
    for d in range(2):
        st_t = st[d].T
        for h in range(SSD_HEADS):
            rows = st_t[h * SSD_HEAD_DIM:(h + 1) * SSD_HEAD_DIM, :]
            if h // (SSD_HEADS // SSD_GROUPS) == 1:
                rows = pltpu.roll(rows, LANE - SSD_STATE, axis=1)
            hf_out[0, d, h] = rows[:, 0:SSD_STATE]

    d0 = d0_ref[...]
    d1 = d1_ref[...]
    gn = gn_ref[...]

    def fin(c, carry):
        rows = chunk_rows(c)
        xs = xc[rows, 0:W]
        cm = xc[rows, W + LANE:W + 2 * LANE].astype(BF16)
        y = ysc[rows, :] + d0 * xs + d1 * xs
        for d in range(2):
            y = y + _dot(cm, hin_s[c, d]) * eacs_s[c, d]
        y = y * _silu(z_ref[rows, :])
        y_out[rows, :] = (_rms(y) * gn).astype(BF16)
        return carry

    lax.fori_loop(0, nc, fin, 0, unroll=min(nc, 4))


def _ssd_call(l, z, xbc, dt, cw, cb, dtb, alog, d0, d1, gn, h0, *, nb, t):
    has_h0 = h0 is not None
    nc = t // SSD_CHUNK

    def tok(b):
        return (b, 0)

    in_specs = [
        pl.BlockSpec((t, SSD_INNER), tok),
        pl.BlockSpec((t, SSD_XBC), tok),
        pl.BlockSpec((t, LANE), tok),
        _layer_spec((SUBLANE, SSD_XBC), l),
        _layer_spec((1, SSD_XBC), l),
        _layer_spec((1, LANE), l),
        _layer_spec((1, LANE), l),
        _layer_spec((1, SSD_INNER), l),
        _layer_spec((1, SSD_INNER), l),
        _layer_spec((1, SSD_INNER), l),
    ]
    args = [z, xbc, dt, cw, cb, dtb, alog, d0, d1, gn]
    if has_h0:
        in_specs.append(pl.BlockSpec((None, 1, 2, LANE, SSD_INNER), lambda b: (l, b, 0, 0, 0)))
        args.append(h0)
    return pl.pallas_call(
        functools.partial(_ssd_kernel, t=t, has_h0=has_h0),
        out_shape=[
            jax.ShapeDtypeStruct((nb * t, SSD_INNER), BF16),
            jax.ShapeDtypeStruct((nb, 2, SSD_HEADS, SSD_HEAD_DIM, SSD_STATE), F32),
        ],
        grid=(nb,),
        in_specs=in_specs,
        out_specs=[
            pl.BlockSpec((t, SSD_INNER), tok),
            pl.BlockSpec((1, 2, SSD_HEADS, SSD_HEAD_DIM, SSD_STATE), lambda b: (b, 0, 0, 0, 0)),
        ],
        scratch_shapes=[
            pltpu.VMEM((t + 2 * SUBLANE, SSD_XBC), F32),
            pltpu.VMEM((t, SSD_XBC), F32),
            pltpu.VMEM((t, LANE), F32),
            pltpu.VMEM((t, LANE), F32),
            pltpu.VMEM((nc * SUBLANE, LANE), F32),
            pltpu.VMEM((t, SSD_INNER), F32),
            pltpu.VMEM((nc, 2, LANE, SSD_INNER), F32),
            pltpu.VMEM((nc, 2, SUBLANE, SSD_INNER), F32),
            pltpu.VMEM((nc, 2, LANE, SSD_INNER), F32),
            pltpu.VMEM((nc, 2, LANE, SSD_INNER), BF16),
            pltpu.VMEM((2, LANE, SSD_INNER), F32),
        ],
        compiler_params=_params(("parallel",)),
        name="ssd_lat" if has_h0 else "ssd_ctx",
    )(*args)


CM_HALO = 16


def _glu(v):
    return v[:, 0:CM_CH] * jax.nn.sigmoid(v[:, CM_CH:2 * CM_CH])


def _conv_tile(cur_ref, prev_rows, next_rows, w, bias, lg, lb, gpad, dst, *, seg):
    tm = cur_ref.shape[0]
    halo = CM_HALO
    sub = SUBLANE
    rb = CM_ROWS
    assert halo - CM_WIDTH // 2 == 1 and CM_WIDTH <= 4 * sub
    zeros = jnp.zeros((halo, CM_CH), F32)
    for sgi in range(tm // seg):
        for piece in range(seg // CONV_ROWS):
            r0 = piece * CONV_ROWS
            gpad[sgi, halo + r0:halo + r0 + CONV_ROWS, :] = _glu(
                cur_ref[sgi * seg + r0:sgi * seg + r0 + CONV_ROWS, :])
        gpad[sgi, 0:halo, :] = zeros if prev_rows is None else prev_rows
        gpad[sgi, halo + seg:halo + seg + halo, :] = zeros if next_rows is None else next_rows

    def block(sgi, blk):
        base = blk * rb
        acc = bias
        for j in range(sub):
            pj = None
            for a in range(4):
                kk = sub * a + j
                if kk < CM_WIDTH:
                    term = w[kk:kk + 1, :] * gpad[sgi, base + sub * a:base + sub * a + rb + sub, :]
                    pj = term if pj is None else pj + term
            acc = acc + pj[j + 1:j + 1 + rb, :]
        mu = jnp.mean(acc, axis=-1, keepdims=True)
        xc = acc - mu
        var = jnp.mean(xc * xc, axis=-1, keepdims=True)
        y = xc * lax.rsqrt(var + 1e-5) * lg + lb
        dst[sgi * seg + base:sgi * seg + base + rb, :] = _silu(y).astype(BF16)

    for sgi in range(tm // seg):
        for blk in range(seg // rb):
            block(sgi, blk)


def _ffn_conv_kernel(*refs, final, seg, tiles_per_seq):
    halos = tiles_per_seq > 1
    if halos:
        (x_ref, attn_ref, ssm_ref, cma_ref, cmb_ref, cmn_ref, mod_ref, wout_ref, gffn_ref, wg_ref, wu_ref,
         wd_ref, gfin_ref, cw_ref, cb_ref, lg_ref, lb_ref, o_ref, gpad, conv_s) = refs
    else:
        (x_ref, attn_ref, ssm_ref, cma_ref, cmb_ref, mod_ref, wout_ref, gffn_ref, wg_ref, wu_ref,
         wd_ref, gfin_ref, cw_ref, cb_ref, lg_ref, lb_ref, o_ref, gpad, conv_s) = refs
    i = pl.program_id(0)
    tm = x_ref.shape[0]
    conv_args = dict(w=cw_ref[...], bias=cb_ref[...], lg=lg_ref[...], lb=lb_ref[...], gpad=gpad, dst=conv_s,
                     seg=seg)

    @pl.when(i == 0)
    def _():
        nxt = _glu(cmb_ref[0:CM_HALO, :]) if halos else None
        _conv_tile(cma_ref, None, nxt, **conv_args)

    n_attn = MLA_HEADS * V_HEAD
    mixed = _dot(attn_ref[...], wout_ref[0:n_attn, :])
    mixed = mixed + _dot(ssm_ref[...], wout_ref[n_attn:n_attn + SSD_INNER, :])
    mixed = mixed + _dot(conv_s[...], wout_ref[n_attn + SSD_INNER:n_attn + SSD_INNER + CM_CH, :])
    g1 = mod_ref[0, :, 2 * D_MODEL:3 * D_MODEL]
    sh2 = mod_ref[0, :, 3 * D_MODEL:4 * D_MODEL]
    sc2 = mod_ref[0, :, 4 * D_MODEL:5 * D_MODEL]
    g2 = mod_ref[0, :, 5 * D_MODEL:6 * D_MODEL]
    x = x_ref[...] + g1 * mixed
    h = (_rms(x) * gffn_ref[...] * (1.0 + sc2) + sh2).astype(BF16)

    ff = jnp.zeros(x.shape, F32)
    for jc in range(D_FF // FF_CHUNK):
        cols = slice(jc * FF_CHUNK, (jc + 1) * FF_CHUNK)
        gate = _dot(h, wg_ref[:, cols])
        up = _dot(h, wu_ref[:, cols])
        ff = ff + _dot((_silu(gate) * up).astype(BF16), wd_ref[cols, :])
    x = x + g2 * ff
    if final:
        x = _rms(x) * gfin_ref[...]
    o_ref[...] = x

    if halos:
        j = i + 1
        first = (j % tiles_per_seq) == 0
        last = (j % tiles_per_seq) == tiles_per_seq - 1
        prv = jnp.where(first, 0.0, _glu(cma_ref[tm - CM_HALO:tm, :]))
        nxt = jnp.where(last, 0.0, _glu(cmn_ref[...]))
    else:
        prv = nxt = None
    _conv_tile(cmb_ref, prv, nxt, **conv_args)


def _ffn_conv_call(l, x, attn, ssm, cm, mod, wout, gffn, wg, wu, wd, gfin, cw, cb, lg, lb, *,
                   t, tmod, mod_row0, final):
    n = x.shape[0]
    tm = TM_FFN
    nt = n // tm
    seg = min(t, tm)
    tiles_per_seq = t // seg
    halos = tiles_per_seq > 1
    hb = tm // CM_HALO

    def tok(i):
        return (i, 0)

    in_specs = [
        pl.BlockSpec((tm, D_MODEL), tok),
        pl.BlockSpec((tm, MLA_HEADS * V_HEAD), tok),
        pl.BlockSpec((tm, SSD_INNER), tok),
        pl.BlockSpec((tm, 2 * CM_CH), tok),
        pl.BlockSpec((tm, 2 * CM_CH), lambda i: (jnp.minimum(i + 1, nt - 1), 0)),
    ]
    args = [x, attn, ssm, cm, cm]
    if halos:
        in_specs.append(pl.BlockSpec((CM_HALO, 2 * CM_CH), lambda i: (jnp.minimum((i + 2) * hb, nt * hb - 1), 0)))
        args.append(cm)
    in_specs += [
        _mod_spec(l, tm, tmod, mod_row0),
        _layer_spec((D_MODEL, D_MODEL), l, single_buffer=True),
        _layer_spec((1, D_MODEL), l),
        _layer_spec((D_MODEL, D_FF), l, single_buffer=True),
        _layer_spec((D_MODEL, D_FF), l, single_buffer=True),
        _layer_spec((D_FF, D_MODEL), l, single_buffer=True),
        pl.BlockSpec((1, D_MODEL), lambda i: (0, 0)),
        _layer_spec((4 * SUBLANE, CM_CH), l),
        _layer_spec((1, CM_CH), l),
        _layer_spec((1, CM_CH), l),
        _layer_spec((1, CM_CH), l),
    ]
    args += [mod, wout, gffn, wg, wu, wd, gfin, cw, cb, lg, lb]
    return pl.pallas_call(
        functools.partial(_ffn_conv_kernel, final=final, seg=seg, tiles_per_seq=tiles_per_seq),
        out_shape=jax.ShapeDtypeStruct((n, D_MODEL), F32),
        grid=(nt,),
        in_specs=in_specs,
        out_specs=pl.BlockSpec((tm, D_MODEL), tok),
        scratch_shapes=[
            pltpu.VMEM((tm // seg, seg + 2 * CM_HALO, CM_CH), F32),
            pltpu.VMEM((tm, CM_CH), BF16),
        ],
        compiler_params=_params(("arbitrary",)),
        name="ffn_conv",
    )(*args)


def _rope_tables():
    rows = DEC_SEQ // GRID_W
    row = jnp.repeat(jnp.arange(rows), GRID_W).astype(F32)
    col = (jnp.arange(rows * GRID_W) % GRID_W).astype(F32)
    nf = QK_ROPE // 4
    inv = ROPE_THETA ** (-jnp.arange(nf, dtype=F32) / nf)
    ang = jnp.stack([row[:, None] * inv, col[:, None] * inv], axis=1)
    cos = jnp.cos(ang)
    sin = jnp.sin(ang)
    cos32 = jnp.stack([cos, cos], axis=2).reshape(DEC_SEQ, QK_ROPE)
    sin32 = jnp.stack([-sin, sin], axis=2).reshape(DEC_SEQ, QK_ROPE)
    ones = jnp.ones((DEC_SEQ, QK_NOPE), F32)
    pad1 = jnp.ones((DEC_SEQ, HEAD_PAD - QK_HEAD), F32)
    cos_t = jnp.concatenate([ones, cos32, pad1], axis=1)
    sin_t = jnp.concatenate([0.0 * ones, sin32, 0.0 * pad1], axis=1)
    return cos_t, sin_t


def _kr_placement():
    src = jnp.arange(QK_ROPE)
    cols = jnp.arange(MLA_HEADS * HEAD_PAD)
    hit = (cols[None, :] % HEAD_PAD) == (QK_NOPE + src[:, None])
    return hit.astype(BF16)


def _pack_states(h0):
    lead = h0.shape[:-3]
    s = jnp.moveaxis(h0, -1, -3).reshape(lead + (SSD_STATE, SSD_INNER))
    lane_group = jnp.arange(SSD_INNER) // LANE
    parts = [jnp.where(lane_group == g, s, 0.0) for g in range(SSD_GROUPS)]
    return jnp.concatenate(parts, axis=-2)


def kernel(x_prompt, x_sample, c, cache_ckv, cache_krope, state_ssd, c_ctx, w_ada, b_ada, g_mix, w_in, g_q, w_uq, g_kv, w_ukv, ssd_conv_w, ssd_conv_b, ssd_dt_bias, ssd_a_log, ssd_d, ssd_norm_g, cm_conv_w, cm_conv_b, cm_ln_g, cm_ln_b, w_out, g_ffn, w_gate, w_up, w_down, g_final):
    L = DEPTH
    w_in_pad = jnp.pad(w_in, ((0, 0), (0, 0), (0, IN_PAD - IN_WIDTH))).astype(BF16)
    win = _place_call(w_in_pad, _src_in(), "place_w_in", bounds=(0, C_DT, C_KR, C_Z, C_CM, IN_ARR))
    wuq = _place_call(w_uq, _src_uq(), "place_w_uq")
    wukv = _place_call(w_ukv, _src_ukv(), "place_w_ukv")
    wout = w_out.astype(BF16)
    wg = w_gate.astype(BF16)
    wu = w_up.astype(BF16)
    wd = w_down.astype(BF16)
    rope_tabs = _rope_tables()

    cond8 = jnp.concatenate([c_ctx[None, :], c, jnp.zeros((8 - 1 - DEC_BATCH, D_MODEL), F32)], axis=0)
    mod = _ada_call(cond8, w_ada, b_ada)
    mod = mod.reshape(L, 8, 1, N_MOD * D_MODEL)

    kc_all, vc_all = _ctxkv_call(cache_ckv, cache_krope, wukv, _kr_placement())

    ssd_cw = jnp.concatenate([ssd_conv_w, jnp.zeros((L, SUBLANE - SSD_CONV, SSD_XBC), F32)], axis=1)
    cm_w = jnp.concatenate([cm_conv_w, jnp.zeros((L, 4 * SUBLANE - CM_WIDTH, CM_CH), F32)], axis=1)
    lane_pad = lambda v: jnp.concatenate(
        [v.reshape(L, 1, 2 * SSD_HEADS), jnp.zeros((L, 1, LANE - 2 * SSD_HEADS), F32)], axis=-1)
    dtb = lane_pad(ssd_dt_bias)
    alog = lane_pad(ssd_a_log)
    d_rep = jnp.repeat(ssd_d, SSD_HEAD_DIM, axis=-1)
    d0 = d_rep[:, 0:1, :]
    d1 = d_rep[:, 1:2, :]
    h0_lat = _pack_states(jnp.swapaxes(state_ssd, 0, 1))
    gmix, gq, gkv, gffn = _vec(g_mix), _vec(g_q), _vec(g_kv), _vec(g_ffn)
    scb, sng = _vec(ssd_conv_b), _vec(ssd_norm_g)
    ccb, clg, clb = _vec(cm_conv_b), _vec(cm_ln_g), _vec(cm_ln_b)
    gfin = g_final.reshape(1, D_MODEL)

    streams = [
        dict(x=x_prompt.reshape(N_CTX, D_MODEL), nb=BATCH, t=SEQ, tmod=N_CTX, mod_row0=0, tq=SEQ, ngroups=2,
             rope=None),
        dict(x=x_sample.reshape(N_LAT, D_MODEL), nb=DEC_BATCH, t=DEC_SEQ, tmod=DEC_SEQ, mod_row0=1, tq=TQ_LAT,
             ngroups=1, rope=rope_tabs),
    ]
    ckvs, krs, hss = [], [], []
    for l in range(L):
        for si, s in enumerate(streams):
            lat = si == 1
            nb, t = s["nb"], s["t"]
            q, k, v, ckv, kr, z, xbc, dt, cm = _inproj_call(
                l, s["x"], mod, gmix, win, gq, wuq, gkv, wukv, s["rope"],
                t=t, tmod=s["tmod"], mod_row0=s["mod_row0"])
            attn = _attn_call(l, q, k, v, kc_all if lat else None, vc_all if lat else None,
                              nb=nb, t=t, tq=s["tq"], ngroups=s["ngroups"])
            ssm, hf = _ssd_call(l, z, xbc, dt, ssd_cw, scb, dtb, alog, d0, d1, sng,
                                h0_lat if lat else None, nb=nb, t=t)
            s["x"] = _ffn_conv_call(l, s["x"], attn, ssm, cm, mod, wout, gffn, wg, wu, wd, gfin,
                                    cm_w, ccb, clg, clb, t=t, tmod=s["tmod"], mod_row0=s["mod_row0"],
                                    final=(l == L - 1))
            if not lat:
                ckvs.append(ckv.reshape(BATCH, SEQ, KV_RANK))
                krs.append(kr[:, QK_NOPE:QK_HEAD].reshape(BATCH, SEQ, QK_ROPE))
                hss.append(hf)

    y_prompt = streams[0]["x"].reshape(BATCH, SEQ, D_MODEL)
    y_sample = streams[1]["x"].reshape(DEC_BATCH, DEC_SEQ, D_MODEL)
    return (y_prompt, y_sample, jnp.stack(ckvs, axis=1), jnp.stack(krs, axis=1), jnp.stack(hss, axis=1))
```

```python
import functools

import numpy as np

import jax
import jax.numpy as jnp
from jax import lax
from jax.experimental import pallas as pl
from jax.experimental.pallas import tpu as pltpu

F32 = jnp.float32
BF16 = jnp.bfloat16

D_MODEL = 1024
BATCH = 16
SEQ = 256
DEPTH = 4
DEC_BATCH = 4
DEC_SEQ = 2048
PAST_LEN = 256
GRID_W = 64

MLA_HEADS = 8
QK_NOPE = 64
QK_ROPE = 32
QK_HEAD = QK_NOPE + QK_ROPE
V_HEAD = 64
Q_RANK = 256
KV_RANK = 128
ROPE_THETA = 10000.0
LOG2_E = 1.4426950408889634

SSD_HEADS = 4
SSD_HEAD_DIM = 64
SSD_INNER = SSD_HEADS * SSD_HEAD_DIM
SSD_GROUPS = 2
SSD_STATE = 64
SSD_CONV = 5
SSD_CHUNK = 128
SSD_XBC = SSD_INNER + 2 * SSD_GROUPS * SSD_STATE

CM_CH = 256
CM_WIDTH = 31

D_FF = 2816
N_MOD = 6
IN_WIDTH = (Q_RANK + KV_RANK + QK_ROPE) + (SSD_INNER + SSD_XBC + 2 * SSD_HEADS) + 2 * CM_CH

N_CTX = BATCH * SEQ
N_LAT = DEC_BATCH * DEC_SEQ

SUBLANE = 8
LANE = 128
MXU_N = 256
HEAD_PAD = 128
HEADS_PER_STEP = MXU_N // V_HEAD
VMEM_LIMIT = 48 * 1024 * 1024

ADA_COLS = 3072
TM_IN = 1024
TM_FFN = 512
TQ_LAT = 1024
FF_CHUNK = 256
CONV_ROWS = 128
CM_ROWS = 64
SSD_CONV_ROWS = 32
ATTN_ROW_PARTS = 4

C_QLAT = 0
C_CKV = C_QLAT + Q_RANK
C_DT = C_CKV + KV_RANK
C_KR = C_DT + LANE
C_Z = C_KR + 2 * HEAD_PAD
C_XBC = C_Z + SSD_INNER
C_CM = C_XBC + SSD_XBC
IN_ARR = C_CM + 2 * CM_CH
IN_PAD = -(-IN_WIDTH // LANE) * LANE


def _params(sem):
    return pltpu.CompilerParams(dimension_semantics=sem, vmem_limit_bytes=VMEM_LIMIT)


def _dot(a, b):
    return jnp.dot(a, b, preferred_element_type=F32)


def _dot_nt(a, b):
    return lax.dot_general(a, b, (((1,), (1,)), ((), ())), preferred_element_type=F32)


def _rms(x, eps=1e-6):
    return x * lax.rsqrt(jnp.mean(x * x, axis=-1, keepdims=True) + eps)


def _silu(x):
    return x * jax.nn.sigmoid(x)


def _layer_spec(shape, l, single_buffer=False):
    nd = len(shape)
    kw = dict(pipeline_mode=pl.Buffered(1)) if single_buffer else {}
    return pl.BlockSpec((None,) + tuple(shape), lambda *_: (l,) + (0,) * nd, **kw)


def _vec(v):
    return v.reshape(DEPTH, 1, -1)


def _place_kernel(w_ref, p_ref, o_ref, *, groups):
    for dlo, dhi, slo, shi in groups:
        if slo is None:
            o_ref[0, :, dlo:dhi] = w_ref[0, :, dlo:dhi].astype(BF16)
        else:
            o_ref[0, :, dlo:dhi] = _dot(w_ref[0, :, slo:shi].astype(BF16), p_ref[slo:shi, dlo:dhi]).astype(BF16)


def _place_groups(src, cin, bounds):
    groups = []
    for dlo, dhi in zip(bounds[:-1], bounds[1:]):
        s = src[dlo:dhi]
        if np.array_equal(s, np.arange(dlo, dhi)):
            groups.append((dlo, dhi, None, None))
            continue
        used = s[s >= 0]
        slo = int(used.min()) // LANE * LANE
        shi = min(-(-(int(used.max()) + 1) // LANE) * LANE, cin)
        groups.append((dlo, dhi, slo, shi))
    return tuple(groups)


def _place_call(w, src, name, bounds=None):
    L, rows, cin = w.shape
    cout = src.shape[0]
    place = (jnp.arange(cin, dtype=jnp.int32)[:, None] == jnp.asarray(src, jnp.int32)[None, :]).astype(BF16)
    tr = min(rows, 256)
    groups = _place_groups(src, cin, bounds if bounds is not None else (0, cout))
    return pl.pallas_call(
        functools.partial(_place_kernel, groups=groups),
        out_shape=jax.ShapeDtypeStruct((L, rows, cout), BF16),
        grid=(L, rows // tr),
        in_specs=[
            pl.BlockSpec((1, tr, cin), lambda l, i: (l, i, 0)),
            pl.BlockSpec((cin, cout), lambda l, i: (0, 0)),
        ],
        out_specs=pl.BlockSpec((1, tr, cout), lambda l, i: (l, i, 0)),
        compiler_params=_params(("parallel", "parallel")),
        name=name,
    )(w, place)


def _swap32():
    return np.arange(QK_ROPE).reshape(2, 2, QK_ROPE // 4)[:, ::-1, :].reshape(-1)


def _src_in():
    kr0 = Q_RANK + KV_RANK
    off_ssd = kr0 + QK_ROPE
    off_dt = off_ssd + SSD_INNER + SSD_XBC
    off_cm = off_dt + 2 * SSD_HEADS
    src = -np.ones((IN_ARR,), np.int64)
    src[0:C_DT] = np.arange(0, kr0)
    src[C_DT:C_DT + 2 * SSD_HEADS] = np.arange(off_dt, off_cm)
    src[C_KR + QK_NOPE:C_KR + QK_HEAD] = kr0 + np.arange(QK_ROPE)
    src[C_KR + HEAD_PAD + QK_NOPE:C_KR + HEAD_PAD + QK_HEAD] = kr0 + _swap32()
    src[C_Z:C_CM] = np.arange(off_ssd, off_dt)
    src[C_CM:IN_ARR] = np.arange(off_cm, IN_WIDTH)
    return src


def _src_uq():
    half = MLA_HEADS * HEAD_PAD
    src = -np.ones((2 * half,), np.int64)
    for h in range(MLA_HEADS):
        src[h * HEAD_PAD:h * HEAD_PAD + QK_HEAD] = h * QK_HEAD + np.arange(QK_HEAD)
        lo = half + h * HEAD_PAD + QK_NOPE
        src[lo:lo + QK_ROPE] = h * QK_HEAD + QK_NOPE + _swap32()
    return src


def _src_ukv():
    half = MLA_HEADS * HEAD_PAD
    src = -np.ones((half + MLA_HEADS * V_HEAD,), np.int64)
    for h in range(MLA_HEADS):
        base = h * (QK_NOPE + V_HEAD)
        src[h * HEAD_PAD:h * HEAD_PAD + QK_NOPE] = base + np.arange(QK_NOPE)
        src[half + h * V_HEAD:half + (h + 1) * V_HEAD] = base + QK_NOPE + np.arange(V_HEAD)
    return src


def _ada_kernel(cond_ref, w_ref, b_ref, o_ref):
    c = cond_ref[...]
    a = _silu(c).astype(BF16)
    o_ref[0] = _dot(a, w_ref[0].astype(BF16)) + b_ref[0]


def _ada_call(cond8, w_ada, b_ada):
    tn = ADA_COLS
    nt = (N_MOD * D_MODEL) // tn
    return pl.pallas_call(
        _ada_kernel,
        out_shape=jax.ShapeDtypeStruct((DEPTH, 8, N_MOD * D_MODEL), F32),
        grid=(DEPTH, nt),
        in_specs=[
            pl.BlockSpec((8, D_MODEL), lambda l, j: (0, 0)),
            pl.BlockSpec((1, D_MODEL, tn), lambda l, j: (l, 0, j)),
            pl.BlockSpec((1, 1, tn), lambda l, j: (l, 0, j)),
        ],
        out_specs=pl.BlockSpec((1, 8, tn), lambda l, j: (l, 0, j)),
        compiler_params=_params(("parallel", "parallel")),
        name="ada_mod",
    )(cond8, w_ada, b_ada.reshape(DEPTH, 1, N_MOD * D_MODEL))


def _mod_spec(l, tm, tmod, mod_row0):
    return pl.BlockSpec((None, 1, 1, N_MOD * D_MODEL), lambda i: (l, mod_row0 + (i * tm) // tmod, 0, 0))


def _inproj_kernel(*refs, rope):
    if rope:
        (x_ref, mod_ref, gmix_ref, win_ref, gq_ref, wuq_ref, gkv_ref, wukv_ref, cos_ref, sin_ref,
         q_out, k_out, v_out, ckv_out, kr_out, z_out, xbc_out, dt_out, cm_out) = refs
        cos = cos_ref[...]
        sin = sin_ref[...]
    else:
        (x_ref, mod_ref, gmix_ref, win_ref, gq_ref, wuq_ref, gkv_ref, wukv_ref,
         q_out, k_out, v_out, ckv_out, kr_out, z_out, xbc_out, dt_out, cm_out) = refs
    x = x_ref[...]
    sh1 = mod_ref[0, :, 0:D_MODEL]
    sc1 = mod_ref[0, :, D_MODEL:2 * D_MODEL]
    h = (_rms(x) * gmix_ref[...] * (1.0 + sc1) + sh1).astype(BF16)

    scale = QK_HEAD ** -0.5 * LOG2_E
    half = MLA_HEADS * HEAD_PAD
    q_lat = _dot(h, win_ref[:, C_QLAT:C_QLAT + Q_RANK])
    ncol = C_KR - C_CKV + (2 if rope else 1) * HEAD_PAD
    mid = _dot(h, win_ref[:, C_CKV:C_CKV + ncol])
    z_out[...] = _dot(h, win_ref[:, C_Z:C_Z + SSD_INNER])

    qn = (_rms(q_lat) * gq_ref[...]).astype(BF16)
    qq = _dot(qn, wuq_ref[...])
    ckv = _rms(mid[:, 0:KV_RANK]) * gkv_ref[...]
    ckv_out[...] = ckv
    kv = _dot(ckv.astype(BF16), wukv_ref[...])
    xbc_out[...] = _dot(h, win_ref[:, C_XBC:C_XBC + SSD_XBC])

    for hd in range(MLA_HEADS):
        lo = hd * HEAD_PAD
        qh = qq[:, lo:lo + HEAD_PAD]
        if rope:
            qh = qh * cos + qq[:, half + lo:half + lo + HEAD_PAD] * sin
        q_out[:, lo:lo + HEAD_PAD] = (qh * scale).astype(BF16)
    cm_out[...] = _dot(h, win_ref[:, C_CM:C_CM + 2 * CM_CH])

    dt_out[...] = mid[:, C_DT - C_CKV:C_DT - C_CKV + LANE]
    kr = mid[:, C_KR - C_CKV:C_KR - C_CKV + HEAD_PAD]
    kr_out[...] = kr
    kr_rot = kr
    if rope:
        kr_rot = kr * cos + mid[:, C_KR - C_CKV + HEAD_PAD:C_KR - C_CKV + 2 * HEAD_PAD] * sin
    for hd in range(MLA_HEADS):
        lo = hd * HEAD_PAD
        k_out[:, lo:lo + HEAD_PAD] = (kv[:, lo:lo + HEAD_PAD] + kr_rot).astype(BF16)
    v_out[...] = kv[:, half:half + MLA_HEADS * V_HEAD].astype(BF16)


def _inproj_call(l, x, mod, gmix, win, gq, wuq, gkv, wukv, rope_tabs, *, t, tmod, mod_row0):
    n = x.shape[0]
    tm = TM_IN
    rope = rope_tabs is not None
    half = MLA_HEADS * HEAD_PAD

    def tok(i):
        return (i, 0)

    outs = [
        ((n, half), BF16),
        ((n, half), BF16),
        ((n, MLA_HEADS * V_HEAD), BF16),
        ((n, KV_RANK), F32),
        ((n, HEAD_PAD), F32),
        ((n, SSD_INNER), F32),
        ((n, SSD_XBC), F32),
        ((n, LANE), F32),
        ((n, 2 * CM_CH), F32),
    ]
    in_specs = [
        pl.BlockSpec((tm, D_MODEL), tok),
        _mod_spec(l, tm, tmod, mod_row0),
        _layer_spec((1, D_MODEL), l),
        _layer_spec((D_MODEL, IN_ARR), l, single_buffer=True),
        _layer_spec((1, Q_RANK), l),
        _layer_spec((Q_RANK, 2 * half if rope else half), l, single_buffer=True),
        _layer_spec((1, KV_RANK), l),
        _layer_spec((KV_RANK, MLA_HEADS * (HEAD_PAD + V_HEAD)), l, single_buffer=True),
    ]
    args = [x, mod, gmix, win, gq, wuq, gkv, wukv]
    if rope:
        pos = lambda i: (i % (t // tm), 0)
        in_specs += [pl.BlockSpec((tm, HEAD_PAD), pos), pl.BlockSpec((tm, HEAD_PAD), pos)]
        args += list(rope_tabs)
    return pl.pallas_call(
        functools.partial(_inproj_kernel, rope=rope),
        out_shape=[jax.ShapeDtypeStruct(s, d) for s, d in outs],
        grid=(n // tm,),
        in_specs=in_specs,
        out_specs=[pl.BlockSpec((tm, s[1]), tok) for s, _ in outs],
        compiler_params=_params(("parallel",)),
        name="in_proj_lat" if rope else "in_proj_ctx",
    )(*args)


def _ctxkv_kernel(ckv_ref, kr_ref, wukv_ref, place_ref, k_out, v_out):
    kv = _dot(ckv_ref[0, 0].astype(BF16), wukv_ref[0])
    krp = _dot(kr_ref[0, 0].astype(BF16), place_ref[...])
    half = MLA_HEADS * HEAD_PAD
    k_out[0, 0] = (kv[:, 0:half] + krp).astype(BF16)
    v_out[0, 0] = kv[:, half:half + MLA_HEADS * V_HEAD].astype(BF16)


def _ctxkv_call(cache_ckv, cache_krope, wukv_all, place):
    return pl.pallas_call(
        _ctxkv_kernel,
        out_shape=[
            jax.ShapeDtypeStruct((DEPTH, DEC_BATCH, PAST_LEN, MLA_HEADS * HEAD_PAD), BF16),
            jax.ShapeDtypeStruct((DEPTH, DEC_BATCH, PAST_LEN, MLA_HEADS * V_HEAD), BF16),
        ],
        grid=(DEPTH, DEC_BATCH),
        in_specs=[
            pl.BlockSpec((1, 1, PAST_LEN, KV_RANK), lambda l, b: (b, l, 0, 0)),
            pl.BlockSpec((1, 1, PAST_LEN, QK_ROPE), lambda l, b: (b, l, 0, 0)),
            pl.BlockSpec((1, KV_RANK, MLA_HEADS * (HEAD_PAD + V_HEAD)), lambda l, b: (l, 0, 0)),
            pl.BlockSpec((QK_ROPE, MLA_HEADS * HEAD_PAD), lambda l, b: (0, 0)),
        ],
        out_specs=[
            pl.BlockSpec((1, 1, PAST_LEN, MLA_HEADS * HEAD_PAD), lambda l, b: (l, b, 0, 0)),
            pl.BlockSpec((1, 1, PAST_LEN, MLA_HEADS * V_HEAD), lambda l, b: (l, b, 0, 0)),
        ],
        compiler_params=_params(("parallel", "parallel")),
        name="ctx_kv",
    )(cache_ckv, cache_krope, wukv_all, place)


def _attn_kernel(*refs, has_ctx, ngroups, nparts):
    if has_ctx:
        q_ref, k_ref, v_ref, kc_ref, vc_ref, o_ref = refs
    else:
        q_ref, k_ref, v_ref, o_ref = refs
    gw = HEADS_PER_STEP * V_HEAD
    tq = q_ref.shape[0]
    rp = tq // nparts
    units = [(grp, part, hh) for grp in range(ngroups) for part in range(nparts) for hh in range(HEADS_PER_STEP)]

    def scores(u):
        grp, part, hh = u
        lo = (grp * HEADS_PER_STEP + hh) * HEAD_PAD
        qh = q_ref[part * rp:(part + 1) * rp, lo:lo + HEAD_PAD]
        s = _dot_nt(qh, k_ref[:, lo:lo + HEAD_PAD])
        sc = _dot_nt(qh, kc_ref[0, :, lo:lo + HEAD_PAD]) if has_ctx else None
        return s, sc

    def probs(s, sc):
        m = jnp.max(s, axis=-1, keepdims=True)
        if has_ctx:
            m = jnp.maximum(m, jnp.max(sc, axis=-1, keepdims=True))
        p = jnp.exp2(s - m)
        den = jnp.sum(p, axis=-1, keepdims=True)
        pc = None
        if has_ctx:
            pc = jnp.exp2(sc - m)
            den = den + jnp.sum(pc, axis=-1, keepdims=True)
            pc = pc.astype(BF16)
        return p.astype(BF16), pc, den

    def values(u, p, pc, den, acc):
        grp, part, hh = u
        o = _dot(p, v_ref[:, grp * gw:(grp + 1) * gw])
        if has_ctx:
            o = o + _dot(pc, vc_ref[0, :, grp * gw:(grp + 1) * gw])
        if hh == 0:
            return o, jnp.broadcast_to(den, o.shape)
        mine = lax.broadcasted_iota(jnp.int32, o.shape, 1) >= hh * V_HEAD
        return jnp.where(mine, o, acc[0]), jnp.where(mine, den, acc[1])

    sc_next = scores(units[0])
    acc = None
    for i, u in enumerate(units):
        sc_cur = sc_next
        if i + 1 < len(units):
            sc_next = scores(units[i + 1])
        acc = values(u, *probs(*sc_cur), acc)
        grp, part, hh = u
        if hh == HEADS_PER_STEP - 1:
            o_ref[part * rp:(part + 1) * rp, grp * gw:(grp + 1) * gw] = (acc[0] / acc[1]).astype(BF16)


def _attn_call(l, q, k, v, kc, vc, *, nb, t, tq, ngroups):
    has_ctx = kc is not None
    nq = t // tq
    nh = ngroups * HEADS_PER_STEP
    nsteps = MLA_HEADS // nh
    in_specs = [
        pl.BlockSpec((tq, nh * HEAD_PAD), lambda b, g, qi: (b * nq + qi, g)),
        pl.BlockSpec((t, nh * HEAD_PAD), lambda b, g, qi: (b, g)),
        pl.BlockSpec((t, nh * V_HEAD), lambda b, g, qi: (b, g)),
    ]
    args = [q, k, v]
    if has_ctx:
        in_specs += [
            pl.BlockSpec((None, 1, PAST_LEN, nh * HEAD_PAD), lambda b, g, qi: (l, b, 0, g)),
            pl.BlockSpec((None, 1, PAST_LEN, nh * V_HEAD), lambda b, g, qi: (l, b, 0, g)),
        ]
        args += [kc, vc]
    return pl.pallas_call(
        functools.partial(_attn_kernel, has_ctx=has_ctx, ngroups=ngroups, nparts=ATTN_ROW_PARTS if has_ctx else 1),
        out_shape=jax.ShapeDtypeStruct((nb * t, MLA_HEADS * V_HEAD), BF16),
        grid=(nb, nsteps, nq),
        in_specs=in_specs,
        out_specs=pl.BlockSpec((tq, nh * V_HEAD), lambda b, g, qi: (b * nq + qi, g)),
        compiler_params=_params(("parallel", "parallel", "arbitrary")),
        name="attn_lat" if has_ctx else "attn_ctx",
    )(*args)


def _split3(x):
    hi = x.astype(BF16)
    r = x - hi.astype(F32)
    mid = r.astype(BF16)
    lo = (r - mid.astype(F32)).astype(BF16)
    return hi, mid, lo


def _ssd_kernel(*refs, t, has_h0):
    if has_h0:
        (z_ref, xbc_ref, dt_ref, cw_ref, cb_ref, dtb_ref, alog_ref, d0_ref, d1_ref, gn_ref, h0_ref,
         y_out, hf_out, xpad, xc, dtp_s, acs_s, acst_s, ysc, ssc, dec_s, eacs_s, hin_s, st) = refs
    else:
        (z_ref, xbc_ref, dt_ref, cw_ref, cb_ref, dtb_ref, alog_ref, d0_ref, d1_ref, gn_ref,
         y_out, hf_out, xpad, xc, dtp_s, acs_s, acst_s, ysc, ssc, dec_s, eacs_s, hin_s, st) = refs
    L = SSD_CHUNK
    W = SSD_INNER
    nc = t // L
    halo = SUBLANE

    xpad[0:halo, :] = jnp.zeros((halo, SSD_XBC), F32)
    xpad[halo + t:halo + t + halo, :] = jnp.zeros((halo, SSD_XBC), F32)
    xpad[halo:halo + t, :] = xbc_ref[...]
    cw = cw_ref[...]
    cb = cb_ref[...]
    rc = SSD_CONV_ROWS
    nwin = rc + 2 * halo

    def conv(blk, carry):
        base = pl.multiple_of(blk * rc, rc)
        win = xpad[pl.ds(base, nwin), :]
        acc = cb + cw[SSD_CONV // 2:SSD_CONV // 2 + 1, :] * win[halo:halo + rc, :]
        for kk in range(SSD_CONV):
            if kk != SSD_CONV // 2:
                first = halo - SSD_CONV // 2 + kk
                acc = acc + cw[kk:kk + 1, :] * pltpu.roll(win, nwin - first, axis=0)[0:rc, :]
        xc[pl.ds(base, rc), :] = _silu(acc)
        return carry

    lax.fori_loop(0, t // rc, conv, 0, unroll=2)
    r = CONV_ROWS

    a_all = -jnp.exp(alog_ref[...])
    dtb = dtb_ref[...]
    row_i = lax.broadcasted_iota(jnp.int32, (L, L), 0)
    col_i = lax.broadcasted_iota(jnp.int32, (L, L), 1)
    tri = [(row_i >= col_i), (row_i <= col_i)]
    tri_bf = [jnp.where(m, 1.0, 0.0).astype(BF16) for m in tri]
    grp_rows = [row_i < SSD_STATE, row_i >= SSD_STATE]
    lane_w = lax.broadcasted_iota(jnp.int32, (L, W), 1)
    own_block = (lax.broadcasted_iota(jnp.int32, (L, W), 0) // SSD_STATE) == (lane_w // LANE)
    sp_rows = lax.broadcasted_iota(jnp.int32, (3 * LANE, 2 * W), 0) % LANE
    sp_cols = lax.broadcasted_iota(jnp.int32, (3 * LANE, 2 * W), 1) // SSD_HEAD_DIM
    spread = jnp.where(sp_rows == sp_cols, 1.0, 0.0).astype(BF16)

    def chunk_rows(c):
        return pl.ds(pl.multiple_of(c * L, L), L)

    lane_lo = col_i < SSD_HEAD_DIM

    def columns(m, d):
        return [jnp.broadcast_to(m[:, j:j + 1], (L, LANE)) for j in range(d * SSD_HEADS, (d + 1) * SSD_HEADS)]

    def expand(cols):
        return jnp.concatenate([jnp.where(lane_lo, cols[0], cols[1]), jnp.where(lane_lo, cols[2], cols[3])],
                               axis=1)

    def pre(c, carry):
        rows = chunk_rows(c)
        dtp = jax.nn.softplus(dt_ref[rows, :] + dtb)
        dtp_s[rows, :] = dtp
        x3 = jnp.concatenate(_split3(dtp * a_all), axis=1)
        rf = _dot(tri_bf[0], x3)
        rb = _dot(tri_bf[1], x3)
        acs_f = rf[:, 0:LANE] + rf[:, LANE:2 * LANE] + rf[:, 2 * LANE:3 * LANE]
        acs_b = rb[:, 0:LANE] + rb[:, LANE:2 * LANE] + rb[:, 2 * LANE:3 * LANE]
        acs = jnp.where(col_i < SSD_HEADS, acs_f, acs_b)
        acs_s[rows, :] = acs
        acst_s[pl.ds(pl.multiple_of(c * SUBLANE, SUBLANE), SUBLANE), :] = acs.T[0:SUBLANE, :]
        return carry

    lax.fori_loop(0, nc, pre, 0, unroll=min(nc, 4))

    def main(c, carry):
        rows = chunk_rows(c)
        xs = xc[rows, 0:W]
        cm = xc[rows, W + LANE:W + 2 * LANE].astype(BF16)
        bt = xc[rows, W:W + LANE].T
        btm2 = jnp.concatenate([jnp.where(grp_rows[0], bt, 0.0), jnp.where(grp_rows[1], bt, 0.0)],
                               axis=1).astype(BF16)
        g2 = _dot(cm, btm2)
        bt_bf = bt.astype(BF16)
        dtp = dtp_s[rows, :]
        acs = acs_s[rows, :]
        acst = acst_s[pl.ds(pl.multiple_of(c * SUBLANE, SUBLANE), SUBLANE), :]
        dt_wide = _dot(jnp.concatenate(_split3(dtp), axis=1), spread)
        ysum = None
        for d in range(2):
            acs_cols = columns(acs, d)
            acs_b = expand(acs_cols)
            eacs_s[c, d] = jnp.exp(acs_b)
            xdt = xs * dt_wide[:, d * W:(d + 1) * W]
            xdt_bf = xdt.astype(BF16)
            yd = None
            for h in range(SSD_HEADS):
                j = d * SSD_HEADS + h
                g = h // (SSD_HEADS // SSD_GROUPS)
                diff = acs_cols[h] - acst[j:j + 1, :]
                seg = jnp.exp(jnp.where(tri[d], diff, -jnp.inf))
                res = _dot((g2[:, g * L:(g + 1) * L] * seg).astype(BF16), xdt_bf)
                yd = res if h == 0 else jnp.where(lane_w >= h * SSD_HEAD_DIM, res, yd)
            ysum = yd if d == 0 else ysum + yd
            tot = acs_b[L - 1:L, :] if d == 0 else acs_b[0:1, :]
            s_new = _dot(bt_bf, (xdt * jnp.exp(tot - acs_b)).astype(BF16))
            ssc[c, d] = jnp.where(own_block, s_new, 0.0)
            dec_s[c, d] = jnp.broadcast_to(jnp.exp(tot), (SUBLANE, W))
        ysc[rows, :] = ysum
        return carry

    lax.fori_loop(0, nc, main, 0, unroll=min(nc, 4))

    if has_h0:
        st[...] = h0_ref[0]
    else:
        st[...] = jnp.zeros(st.shape, F32)

    def rec(i, carry):
        for d, c in ((0, i), (1, nc - 1 - i)):
            h_in = st[d]
            hin_s[c, d] = h_in.astype(BF16)
            st[d] = dec_s[c, d][0:1, :] * h_in + ssc[c, d]
        return carry

    lax.fori_loop(0, nc, rec, 0)
    for d in range(2):
        st_t = st[d].T
        for h in range(SSD_HEADS):
            rows = st_t[h * SSD_HEAD_DIM:(h + 1) * SSD_HEAD_DIM, :]
            if h // (SSD_HEADS // SSD_GROUPS) == 1:
                rows = pltpu.roll(rows, LANE - SSD_STATE, axis=1)
            hf_out[0, d, h] = rows[:, 0:SSD_STATE]

    d0 = d0_ref[...]
    d1 = d1_ref[...]
    gn = gn_ref[...]

    def fin(c, carry):
        rows = chunk_rows(c)
        xs = xc[rows, 0:W]
        cm = xc[rows, W + LANE:W + 2 * LANE].astype(BF16)
        y = ysc[rows, :] + d0 * xs + d1 * xs
        for d in range(2):
            y = y + _dot(cm, hin_s[c, d]) * eacs_s[c, d]
        y = y * _silu(z_ref[rows, :])
        y_out[rows, :] = (_rms(y) * gn).astype(BF16)
        return carry

    lax.fori_loop(0, nc, fin, 0, unroll=min(nc, 4))


def _ssd_call(l, z, xbc, dt, cw, cb, dtb, alog, d0, d1, gn, h0, *, nb, t):
    has_h0 = h0 is not None
    nc = t // SSD_CHUNK

    def tok(b):
        return (b, 0)

    in_specs = [
        pl.BlockSpec((t, SSD_INNER), tok),
        pl.BlockSpec((t, SSD_XBC), tok),
        pl.BlockSpec((t, LANE), tok),
        _layer_spec((SUBLANE, SSD_XBC), l),
        _layer_spec((1, SSD_XBC), l),
        _layer_spec((1, LANE), l),
        _layer_spec((1, LANE), l),
        _layer_spec((1, SSD_INNER), l),
        _layer_spec((1, SSD_INNER), l),
        _layer_spec((1, SSD_INNER), l),
    ]
    args = [z, xbc, dt, cw, cb, dtb, alog, d0, d1, gn]
    if has_h0:
        in_specs.append(pl.BlockSpec((None, 1, 2, LANE, SSD_INNER), lambda b: (l, b, 0, 0, 0)))
        args.append(h0)
    return pl.pallas_call(
        functools.partial(_ssd_kernel, t=t, has_h0=has_h0),
        out_shape=[
            jax.ShapeDtypeStruct((nb * t, SSD_INNER), BF16),
            jax.ShapeDtypeStruct((nb, 2, SSD_HEADS, SSD_HEAD_DIM, SSD_STATE), F32),
        ],
        grid=(nb,),
        in_specs=in_specs,
        out_specs=[
            pl.BlockSpec((t, SSD_INNER), tok),
            pl.BlockSpec((1, 2, SSD_HEADS, SSD_HEAD_DIM, SSD_STATE), lambda b: (b, 0, 0, 0, 0)),
        ],
        scratch_shapes=[
            pltpu.VMEM((t + 2 * SUBLANE, SSD_XBC), F32),
            pltpu.VMEM((t, SSD_XBC), F32),
            pltpu.VMEM((t, LANE), F32),
            pltpu.VMEM((t, LANE), F32),
            pltpu.VMEM((nc * SUBLANE, LANE), F32),
            pltpu.VMEM((t, SSD_INNER), F32),
            pltpu.VMEM((nc, 2, LANE, SSD_INNER), F32),
            pltpu.VMEM((nc, 2, SUBLANE, SSD_INNER), F32),
            pltpu.VMEM((nc, 2, LANE, SSD_INNER), F32),
            pltpu.VMEM((nc, 2, LANE, SSD_INNER), BF16),
            pltpu.VMEM((2, LANE, SSD_INNER), F32),
        ],
        compiler_params=_params(("parallel",)),
        name="ssd_lat" if has_h0 else "ssd_ctx",
    )(*args)


CM_HALO = 16


def _glu(v):
    return v[:, 0:CM_CH] * jax.nn.sigmoid(v[:, CM_CH:2 * CM_CH])


def _conv_tile(cur_ref, prev_rows, next_rows, w, bias, lg, lb, gpad, dst, *, seg):
    tm = cur_ref.shape[0]
    halo = CM_HALO
    sub = SUBLANE
    rb = CM_ROWS
    assert halo - CM_WIDTH // 2 == 1 and CM_WIDTH <= 4 * sub
    zeros = jnp.zeros((halo, CM_CH), F32)
    for sgi in range(tm // seg):
        for piece in range(seg // CONV_ROWS):
            r0 = piece * CONV_ROWS
            gpad[sgi, halo + r0:halo + r0 + CONV_ROWS, :] = _glu(
                cur_ref[sgi * seg + r0:sgi * seg + r0 + CONV_ROWS, :])
        gpad[sgi, 0:halo, :] = zeros if prev_rows is None else prev_rows
        gpad[sgi, halo + seg:halo + seg + halo, :] = zeros if next_rows is None else next_rows

    def block(sgi, blk):
        base = blk * rb
        acc = bias
        for j in range(sub):
            pj = None
            for a in range(4):
                kk = sub * a + j
                if kk < CM_WIDTH:
                    term = w[kk:kk + 1, :] * gpad[sgi, base + sub * a:base + sub * a + rb + sub, :]
                    pj = term if pj is None else pj + term
            acc = acc + pj[j + 1:j + 1 + rb, :]
        mu = jnp.mean(acc, axis=-1, keepdims=True)
        xc = acc - mu
        var = jnp.mean(xc * xc, axis=-1, keepdims=True)
        y = xc * lax.rsqrt(var + 1e-5) * lg + lb
        dst[sgi * seg + base:sgi * seg + base + rb, :] = _silu(y).astype(BF16)

    for sgi in range(tm // seg):
        for blk in range(seg // rb):
            block(sgi, blk)


def _ffn_conv_kernel(*refs, final, seg, tiles_per_seq):
    halos = tiles_per_seq > 1
    if halos:
        (x_ref, attn_ref, ssm_ref, cma_ref, cmb_ref, cmn_ref, mod_ref, wout_ref, gffn_ref, wg_ref, wu_ref,
         wd_ref, gfin_ref, cw_ref, cb_ref, lg_ref, lb_ref, o_ref, gpad, conv_s) = refs
    else:
        (x_ref, attn_ref, ssm_ref, cma_ref, cmb_ref, mod_ref, wout_ref, gffn_ref, wg_ref, wu_ref,
         wd_ref, gfin_ref, cw_ref, cb_ref, lg_ref, lb_ref, o_ref, gpad, conv_s) = refs
    i = pl.program_id(0)
    tm = x_ref.shape[0]
    conv_args = dict(w=cw_ref[...], bias=cb_ref[...], lg=lg_ref[...], lb=lb_ref[...], gpad=gpad, dst=conv_s,
                     seg=seg)

    @pl.when(i == 0)
    def _():
        nxt = _glu(cmb_ref[0:CM_HALO, :]) if halos else None
        _conv_tile(cma_ref, None, nxt, **conv_args)

    n_attn = MLA_HEADS * V_HEAD
    mixed = _dot(attn_ref[...], wout_ref[0:n_attn, :])
    mixed = mixed + _dot(ssm_ref[...], wout_ref[n_attn:n_attn + SSD_INNER, :])
    mixed = mixed + _dot(conv_s[...], wout_ref[n_attn + SSD_INNER:n_attn + SSD_INNER + CM_CH, :])
    g1 = mod_ref[0, :, 2 * D_MODEL:3 * D_MODEL]
    sh2 = mod_ref[0, :, 3 * D_MODEL:4 * D_MODEL]
    sc2 = mod_ref[0, :, 4 * D_MODEL:5 * D_MODEL]
    g2 = mod_ref[0, :, 5 * D_MODEL:6 * D_MODEL]
    x = x_ref[...] + g1 * mixed
    h = (_rms(x) * gffn_ref[...] * (1.0 + sc2) + sh2).astype(BF16)

    ff = jnp.zeros(x.shape, F32)
    for jc in range(D_FF // FF_CHUNK):
        cols = slice(jc * FF_CHUNK, (jc + 1) * FF_CHUNK)
        gate = _dot(h, wg_ref[:, cols])
        up = _dot(h, wu_ref[:, cols])
        ff = ff + _dot((_silu(gate) * up).astype(BF16), wd_ref[cols, :])
    x = x + g2 * ff
    if final:
        x = _rms(x) * gfin_ref[...]
    o_ref[...] = x

    if halos:
        j = i + 1
        first = (j % tiles_per_seq) == 0
        last = (j % tiles_per_seq) == tiles_per_seq - 1
        prv = jnp.where(first, 0.0, _glu(cma_ref[tm - CM_HALO:tm, :]))
        nxt = jnp.where(last, 0.0, _glu(cmn_ref[...]))
    else:
        prv = nxt = None
    _conv_tile(cmb_ref, prv, nxt, **conv_args)


def _ffn_conv_call(l, x, attn, ssm, cm, mod, wout, gffn, wg, wu, wd, gfin, cw, cb, lg, lb, *,
                   t, tmod, mod_row0, final):
    n = x.shape[0]
    tm = TM_FFN
    nt = n // tm
    seg = min(t, tm)
    tiles_per_seq = t // seg
    halos = tiles_per_seq > 1
    hb = tm // CM_HALO

    def tok(i):
        return (i, 0)

    in_specs = [
        pl.BlockSpec((tm, D_MODEL), tok),
        pl.BlockSpec((tm, MLA_HEADS * V_HEAD), tok),
        pl.BlockSpec((tm, SSD_INNER), tok),
        pl.BlockSpec((tm, 2 * CM_CH), tok),
        pl.BlockSpec((tm, 2 * CM_CH), lambda i: (jnp.minimum(i + 1, nt - 1), 0)),
    ]
    args = [x, attn, ssm, cm, cm]
    if halos:
        in_specs.append(pl.BlockSpec((CM_HALO, 2 * CM_CH), lambda i: (jnp.minimum((i + 2) * hb, nt * hb - 1), 0)))
        args.append(cm)
    in_specs += [
        _mod_spec(l, tm, tmod, mod_row0),
        _layer_spec((D_MODEL, D_MODEL), l, single_buffer=True),
        _layer_spec((1, D_MODEL), l),
        _layer_spec((D_MODEL, D_FF), l, single_buffer=True),
        _layer_spec((D_MODEL, D_FF), l, single_buffer=True),
        _layer_spec((D_FF, D_MODEL), l, single_buffer=True),
        pl.BlockSpec((1, D_MODEL), lambda i: (0, 0)),
        _layer_spec((4 * SUBLANE, CM_CH), l),
        _layer_spec((1, CM_CH), l),
        _layer_spec((1, CM_CH), l),
        _layer_spec((1, CM_CH), l),
    ]
    args += [mod, wout, gffn, wg, wu, wd, gfin, cw, cb, lg, lb]
    return pl.pallas_call(
        functools.partial(_ffn_conv_kernel, final=final, seg=seg, tiles_per_seq=tiles_per_seq),
        out_shape=jax.ShapeDtypeStruct((n, D_MODEL), F32),
        grid=(nt,),
        in_specs=in_specs,
        out_specs=pl.BlockSpec((tm, D_MODEL), tok),
        scratch_shapes=[
            pltpu.VMEM((tm // seg, seg + 2 * CM_HALO, CM_CH), F32),
            pltpu.VMEM((tm, CM_CH), BF16),
        ],
        compiler_params=_params(("arbitrary",)),
        name="ffn_conv",
    )(*args)


def _rope_tables():
    rows = DEC_SEQ // GRID_W
    row = jnp.repeat(jnp.arange(rows), GRID_W).astype(F32)
    col = (jnp.arange(rows * GRID_W) % GRID_W).astype(F32)
    nf = QK_ROPE // 4
    inv = ROPE_THETA ** (-jnp.arange(nf, dtype=F32) / nf)
    ang = jnp.stack([row[:, None] * inv, col[:, None] * inv], axis=1)
    cos = jnp.cos(ang)
    sin = jnp.sin(ang)
    cos32 = jnp.stack([cos, cos], axis=2).reshape(DEC_SEQ, QK_ROPE)
    sin32 = jnp.stack([-sin, sin], axis=2).reshape(DEC_SEQ, QK_ROPE)
    ones = jnp.ones((DEC_SEQ, QK_NOPE), F32)
    pad1 = jnp.ones((DEC_SEQ, HEAD_PAD - QK_HEAD), F32)
    cos_t = jnp.concatenate([ones, cos32, pad1], axis=1)
    sin_t = jnp.concatenate([0.0 * ones, sin32, 0.0 * pad1], axis=1)
    return cos_t, sin_t


def _kr_placement():
    src = jnp.arange(QK_ROPE)
    cols = jnp.arange(MLA_HEADS * HEAD_PAD)
    hit = (cols[None, :] % HEAD_PAD) == (QK_NOPE + src[:, None])
    return hit.astype(BF16)


def _pack_states(h0):
    lead = h0.shape[:-3]
    s = jnp.moveaxis(h0, -1, -3).reshape(lead + (SSD_STATE, SSD_INNER))
    lane_group = jnp.arange(SSD_INNER) // LANE
    parts = [jnp.where(lane_group == g, s, 0.0) for g in range(SSD_GROUPS)]
    return jnp.concatenate(parts, axis=-2)


def kernel(x_prompt, x_sample, c, cache_ckv, cache_krope, state_ssd, c_ctx, w_ada, b_ada, g_mix, w_in, g_q, w_uq, g_kv, w_ukv, ssd_conv_w, ssd_conv_b, ssd_dt_bias, ssd_a_log, ssd_d, ssd_norm_g, cm_conv_w, cm_conv_b, cm_ln_g, cm_ln_b, w_out, g_ffn, w_gate, w_up, w_down, g_final):
    L = DEPTH
    w_in_pad = jnp.pad(w_in, ((0, 0), (0, 0), (0, IN_PAD - IN_WIDTH))).astype(BF16)
    win = _place_call(w_in_pad, _src_in(), "place_w_in", bounds=(0, C_DT, C_KR, C_Z, C_CM, IN_ARR))
    wuq = _place_call(w_uq, _src_uq(), "place_w_uq")
    wukv = _place_call(w_ukv, _src_ukv(), "place_w_ukv")
    wout = w_out.astype(BF16)
    wg = w_gate.astype(BF16)
    wu = w_up.astype(BF16)
    wd = w_down.astype(BF16)
    rope_tabs = _rope_tables()

    cond8 = jnp.concatenate([c_ctx[None, :], c, jnp.zeros((8 - 1 - DEC_BATCH, D_MODEL), F32)], axis=0)
    mod = _ada_call(cond8, w_ada, b_ada)
    mod = mod.reshape(L, 8, 1, N_MOD * D_MODEL)

    kc_all, vc_all = _ctxkv_call(cache_ckv, cache_krope, wukv, _kr_placement())

    ssd_cw = jnp.concatenate([ssd_conv_w, jnp.zeros((L, SUBLANE - SSD_CONV, SSD_XBC), F32)], axis=1)
    cm_w = jnp.concatenate([cm_conv_w, jnp.zeros((L, 4 * SUBLANE - CM_WIDTH, CM_CH), F32)], axis=1)
    lane_pad = lambda v: jnp.concatenate(
        [v.reshape(L, 1, 2 * SSD_HEADS), jnp.zeros((L, 1, LANE - 2 * SSD_HEADS), F32)], axis=-1)
    dtb = lane_pad(ssd_dt_bias)
    alog = lane_pad(ssd_a_log)
    d_rep = jnp.repeat(ssd_d, SSD_HEAD_DIM, axis=-1)
    d0 = d_rep[:, 0:1, :]
    d1 = d_rep[:, 1:2, :]
    h0_lat = _pack_states(jnp.swapaxes(state_ssd, 0, 1))
    gmix, gq, gkv, gffn = _vec(g_mix), _vec(g_q), _vec(g_kv), _vec(g_ffn)
    scb, sng = _vec(ssd_conv_b), _vec(ssd_norm_g)
    ccb, clg, clb = _vec(cm_conv_b), _vec(cm_ln_g), _vec(cm_ln_b)
    gfin = g_final.reshape(1, D_MODEL)

    streams = [
        dict(x=x_prompt.reshape(N_CTX, D_MODEL), nb=BATCH, t=SEQ, tmod=N_CTX, mod_row0=0, tq=SEQ, ngroups=2,
             rope=None),
        dict(x=x_sample.reshape(N_LAT, D_MODEL), nb=DEC_BATCH, t=DEC_SEQ, tmod=DEC_SEQ, mod_row0=1, tq=TQ_LAT,
             ngroups=1, rope=rope_tabs),
    ]
    ckvs, krs, hss = [], [], []
    for l in range(L):
        for si, s in enumerate(streams):
            lat = si == 1
            nb, t = s["nb"], s["t"]
            q, k, v, ckv, kr, z, xbc, dt, cm = _inproj_call(
                l, s["x"], mod, gmix, win, gq, wuq, gkv, wukv, s["rope"],
                t=t, tmod=s["tmod"], mod_row0=s["mod_row0"])
            attn = _attn_call(l, q, k, v, kc_all if lat else None, vc_all if lat else None,
                              nb=nb, t=t, tq=s["tq"], ngroups=s["ngroups"])
            ssm, hf = _ssd_call(l, z, xbc, dt, ssd_cw, scb, dtb, alog, d0, d1, sng,
                                h0_lat if lat else None, nb=nb, t=t)
            s["x"] = _ffn_conv_call(l, s["x"], attn, ssm, cm, mod, wout, gffn, wg, wu, wd, gfin,
                                    cm_w, ccb, clg, clb, t=t, tmod=s["tmod"], mod_row0=s["mod_row0"],
                                    final=(l == L - 1))
            if not lat:
                ckvs.append(ckv.reshape(BATCH, SEQ, KV_RANK))
                krs.append(kr[:, QK_NOPE:QK_HEAD].reshape(BATCH, SEQ, QK_ROPE))
                hss.append(hf)

    y_prompt = streams[0]["x"].reshape(BATCH, SEQ, D_MODEL)
    y_sample = streams[1]["x"].reshape(DEC_BATCH, DEC_SEQ, D_MODEL)
    return (y_prompt, y_sample, jnp.stack(ckvs, axis=1), jnp.stack(krs, axis=1), jnp.stack(hss, axis=1))
```
